```python
import jax, jax.numpy as jnp
from jax import lax
import numpy as np

D_MODEL = 2048
BATCH = 4
SEQ = 4096
DEPTH = 1
DEC_BATCH = 16
DEC_SEQ = 64
PAST_LEN = 4096

CHUNK = 64
N_HEADS_M = 4
HEAD_DIM_M = 256
N_HEADS_R = 4
HEAD_DIM_R = 256
D_M = N_HEADS_M * HEAD_DIM_M
D_R = N_HEADS_R * HEAD_DIM_R
D_MIX = D_M + D_R
D_IN = 4 * D_M + 4 * D_R + 2 * N_HEADS_M
D_FF = ((8 * D_MODEL + 3 * 256 - 1) // (3 * 256)) * 256
ROPE_BASE = 10000.0
EPS = 1e-6

kernel_name = 'hybrid_mlstm_retention_stream_step'

F32 = jnp.float32


def rmsnorm(x, g):
    xf = x.astype(F32)
    y = xf * lax.rsqrt(jnp.mean(xf * xf, axis=-1, keepdims=True) + EPS) * g.astype(F32)
    return y.astype(x.dtype)


def head_norm(h, g, center):
    if center:
        h = h - jnp.mean(h, axis=-1, keepdims=True)
    h = h * lax.rsqrt(jnp.mean(h * h, axis=-1, keepdims=True) + EPS)
    B, H, T, D = h.shape
    return h.transpose(0, 2, 1, 3).reshape(B, T, H * D) * g.astype(F32)


def rope(x, pos):
    half = x.shape[-1] // 2
    freqs = ROPE_BASE ** (-jnp.arange(half, dtype=F32) / half)
    ang = pos[:, None] * freqs[None, :]
    cos, sin = jnp.cos(ang), jnp.sin(ang)
    x1, x2 = x[..., :half], x[..., half:]
    return jnp.concatenate([x1 * cos - x2 * sin, x2 * cos + x1 * sin], axis=-1)


def to_chunks(x, L):
    B, H, T = x.shape[:3]
    return jnp.moveaxis(x.reshape((B, H, T // L, L) + x.shape[3:]), 2, 0)


def from_chunks(x):
    NC, B, H, L, D = x.shape
    return jnp.moveaxis(x, 0, 2).reshape(B, H, NC * L, D)


def mlstm_chunkwise(q, k, v, log_i, log_f, C0, n0, m0):
    T = q.shape[2]
    L = min(CHUNK, T)
    assert T % L == 0
    mask = jnp.tril(jnp.ones((L, L), dtype=bool))

    def step(carry, inp):
        C, n, m = carry
        qc, kc, vc, ic, fc = inp
        b = jnp.cumsum(fc, axis=-1)
        log_d = jnp.where(mask, b[..., :, None] - b[..., None, :] + ic[..., None, :], -jnp.inf)
        log_inter = b + m[..., None]
        m_t = jnp.maximum(log_inter, jnp.max(log_d, axis=-1))
        w = jnp.exp(log_d - m_t[..., None])
        a = jnp.exp(log_inter - m_t)
        s = jnp.einsum('bhtd,bhsd->bhts', qc, kc) * w
        num = a[..., None] * jnp.einsum('bhtd,bhde->bhte', qc, C) + jnp.einsum('bhts,bhse->bhte', s, vc)
        den = a * jnp.einsum('bhtd,bhd->bht', qc, n) + jnp.sum(s, axis=-1)
        h = num / jnp.maximum(jnp.abs(den), jnp.exp(-m_t))[..., None]
        b_last = b[..., -1]
        log_g = b_last[..., None] - b + ic
        m_new = jnp.maximum(b_last + m, jnp.max(log_g, axis=-1))
        g = jnp.exp(log_g - m_new[..., None])
        dec = jnp.exp(b_last + m - m_new)
        kg = kc * g[..., None]
        C_new = dec[..., None, None] * C + jnp.einsum('bhsd,bhse->bhde', kg, vc)
        n_new = dec[..., None] * n + jnp.sum(kg, axis=2)
        return (C_new, n_new, m_new), h

    xs = (to_chunks(q, L), to_chunks(k, L), to_chunks(v, L), to_chunks(log_i, L), to_chunks(log_f, L))
    (C, n, m), h = lax.scan(step, (C0, n0, m0), xs)
    return from_chunks(h), C, n, m


def retention_chunkwise(q, k, v, log_gamma, R0):
    T = q.shape[2]
    L = min(CHUNK, T)
    assert T % L == 0
    pos = jnp.arange(L, dtype=F32)
    diff = pos[:, None] - pos[None, :]
    lg = log_gamma[:, None, None]
    dmask = jnp.where(diff >= 0, jnp.exp(lg * jnp.maximum(diff, 0.0)), 0.0)
    inter = jnp.exp(log_gamma[:, None] * (pos + 1.0))
    kdec = jnp.exp(log_gamma[:, None] * (L - 1.0 - pos))
    cdec = jnp.exp(log_gamma * L)

    def step(R, inp):
        qc, kc, vc = inp
        s = jnp.einsum('bhtd,bhsd->bhts', qc, kc) * dmask
        o = jnp.einsum('bhts,bhse->bhte', s, vc) + inter[None, :, :, None] * jnp.einsum('bhtd,bhde->bhte', qc, R)
        R_new = cdec[None, :, None, None] * R + jnp.einsum('bhsd,bhse->bhde', kc * kdec[None, :, :, None], vc)
        return R_new, o

    R, o = lax.scan(step, R0, (to_chunks(q, L), to_chunks(k, L), to_chunks(v, L)))
    return from_chunks(o), R


def layer(x, pos, C0, n0, m0, R0, g_norm1, w_in, b_gates, g_mlstm_norm, g_ret_norm,
          w_out, g_norm2, w_gate, w_up, w_down):
    B, T, _ = x.shape
    xn = rmsnorm(x, g_norm1)
    proj = xn @ w_in
    cuts = [D_M, 2 * D_M, 3 * D_M, 4 * D_M, 4 * D_M + D_R, 4 * D_M + 2 * D_R,
            4 * D_M + 3 * D_R, 4 * D_M + 4 * D_R]
    mq, mk, mv, mo, rq, rk, rv, rg, gates = jnp.split(proj, cuts, axis=-1)

    def heads(t, nh):
        return t.reshape(B, T, nh, -1).transpose(0, 2, 1, 3).astype(F32)

    gates = gates.astype(F32) + b_gates.astype(F32)
    log_i = gates[..., :N_HEADS_M].transpose(0, 2, 1)
    log_f = jax.nn.log_sigmoid(gates[..., N_HEADS_M:]).transpose(0, 2, 1)
    h_m, C, n, m = mlstm_chunkwise(heads(mq, N_HEADS_M), heads(mk, N_HEADS_M) * HEAD_DIM_M ** -0.5,
                                   heads(mv, N_HEADS_M), log_i, log_f,
                                   C0.astype(F32), n0.astype(F32), m0.astype(F32))
    h_m = head_norm(h_m, g_mlstm_norm, False) * jax.nn.sigmoid(mo.astype(F32))

    log_gamma = jnp.log(1.0 - jnp.exp2(-5.0 - jnp.arange(N_HEADS_R, dtype=F32)))
    q_r = rope(heads(rq, N_HEADS_R), pos)
    k_r = rope(heads(rk, N_HEADS_R), pos) * HEAD_DIM_R ** -0.5
    h_r, R = retention_chunkwise(q_r, k_r, heads(rv, N_HEADS_R), log_gamma, R0.astype(F32))
    h_r = head_norm(h_r, g_ret_norm, True) * jax.nn.silu(rg.astype(F32))

    mix = jnp.concatenate([h_m, h_r], axis=-1).astype(x.dtype)
    x = x + mix @ w_out
    hn = rmsnorm(x, g_norm2)
    x = x + (jax.nn.silu(hn @ w_gate) * (hn @ w_up)) @ w_down
    return x, C, n, m, R


def trunk(x, pos, C0, n0, m0, R0, g_norm1, w_in, b_gates, g_mlstm_norm, g_ret_norm,
          w_out, g_norm2, w_gate, w_up, w_down, g_final):
    Cs, ns, ms, Rs = [], [], [], []
    for l in range(DEPTH):
        x, C, n, m, R = layer(x, pos, C0[l], n0[l], m0[l], R0[l], g_norm1[l], w_in[l], b_gates[l],
                              g_mlstm_norm[l], g_ret_norm[l], w_out[l], g_norm2[l],
                              w_gate[l], w_up[l], w_down[l])
        Cs.append(C); ns.append(n); ms.append(m); Rs.append(R)
    y = rmsnorm(x, g_final)
    return y, jnp.stack(Cs), jnp.stack(ns), jnp.stack(ms), jnp.stack(Rs)


def setup_inputs(seed: int = 0) -> dict:
    key = jax.random.key(seed)
    ks = jax.random.split(key, 20)
    nrm = jax.random.normal
    b_i = 0.1 * nrm(ks[0], (DEPTH, N_HEADS_M), F32)
    b_f = jnp.linspace(3.0, 6.0, N_HEADS_M, dtype=F32)[None, :] + 0.1 * nrm(ks[1], (DEPTH, N_HEADS_M), F32)
    return {
        'x_prompt': nrm(ks[2], (BATCH, SEQ, D_MODEL), F32),
        'x_sample': nrm(ks[3], (DEC_BATCH, DEC_SEQ, D_MODEL), F32),
        'state_mlstm_C': 0.1 * nrm(ks[4], (DEPTH, DEC_BATCH, N_HEADS_M, HEAD_DIM_M, HEAD_DIM_M), F32),
        'state_mlstm_n': 0.1 * nrm(ks[5], (DEPTH, DEC_BATCH, N_HEADS_M, HEAD_DIM_M), F32),
        'state_mlstm_m': nrm(ks[6], (DEPTH, DEC_BATCH, N_HEADS_M), F32),
        'state_ret': 0.1 * nrm(ks[7], (DEPTH, DEC_BATCH, N_HEADS_R, HEAD_DIM_R, HEAD_DIM_R), F32),
        'g_norm1': 1.0 + 0.02 * nrm(ks[8], (DEPTH, D_MODEL), F32),
        'w_in': nrm(ks[9], (DEPTH, D_MODEL, D_IN), F32) * D_MODEL ** -0.5,
        'b_gates': jnp.concatenate([b_i, b_f], axis=-1),
        'g_mlstm_norm': 1.0 + 0.02 * nrm(ks[10], (DEPTH, D_M), F32),
        'g_ret_norm': 1.0 + 0.02 * nrm(ks[11], (DEPTH, D_R), F32),
        'w_out': nrm(ks[12], (DEPTH, D_MIX, D_MODEL), F32) * D_MIX ** -0.5,
        'g_norm2': 1.0 + 0.02 * nrm(ks[13], (DEPTH, D_MODEL), F32),
        'w_gate': nrm(ks[14], (DEPTH, D_MODEL, D_FF), F32) * D_MODEL ** -0.5,
        'w_up': nrm(ks[15], (DEPTH, D_MODEL, D_FF), F32) * D_MODEL ** -0.5,
        'w_down': nrm(ks[16], (DEPTH, D_FF, D_MODEL), F32) * D_FF ** -0.5,
        'g_final': 1.0 + 0.02 * nrm(ks[17], (D_MODEL,), F32),
    }


def reference(x_prompt, x_sample, state_mlstm_C, state_mlstm_n, state_mlstm_m, state_ret,
              g_norm1, w_in, b_gates, g_mlstm_norm, g_ret_norm, w_out, g_norm2,
              w_gate, w_up, w_down, g_final):
    Bp, Tp, _ = x_prompt.shape
    Ts = x_sample.shape[1]
    pos_p = jnp.arange(Tp, dtype=F32)
    pos_s = PAST_LEN + jnp.arange(Ts, dtype=F32)
    C0 = jnp.zeros((DEPTH, Bp, N_HEADS_M, HEAD_DIM_M, HEAD_DIM_M), F32)
    n0 = jnp.zeros((DEPTH, Bp, N_HEADS_M, HEAD_DIM_M), F32)
    m0 = jnp.zeros((DEPTH, Bp, N_HEADS_M), F32)
    R0 = jnp.zeros((DEPTH, Bp, N_HEADS_R, HEAD_DIM_R, HEAD_DIM_R), F32)
    y_prompt, C_p, n_p, m_p, R_p = trunk(x_prompt, pos_p, C0, n0, m0, R0, g_norm1, w_in, b_gates,
                                         g_mlstm_norm, g_ret_norm, w_out, g_norm2,
                                         w_gate, w_up, w_down, g_final)
    y_sample, C_s, n_s, m_s, R_s = trunk(x_sample, pos_s, state_mlstm_C, state_mlstm_n, state_mlstm_m,
                                         state_ret, g_norm1, w_in, b_gates, g_mlstm_norm, g_ret_norm,
                                         w_out, g_norm2, w_gate, w_up, w_down, g_final)
    return (y_prompt, y_sample, C_p, n_p, m_p, R_p, C_s, n_s, m_s, R_s)
```

```python
import functools

import jax
import jax.numpy as jnp
from jax import lax
from jax.experimental import pallas as pl
from jax.experimental.pallas import tpu as pltpu

F32 = jnp.float32
BF16 = jnp.bfloat16

D_MODEL = 2048
N_HEADS = 4
HEAD_DIM = 256
D_GROUP = N_HEADS * HEAD_DIM
D_PROJ = 8 * D_GROUP
N_GATES = 2 * N_HEADS
D_FF = 5632
ROPE_BASE = 10000.0
PAST_LEN = 4096
EPS = 1e-6
K_SCALE = HEAD_DIM ** -0.5

LANES = 128
SUBLANES = 8
VMEM_LIMIT = 56 * 1024 * 1024

PROMPT_CHUNK = 256

_NT = (((1,), (1,)), ((), ()))
_TN = (((0,), (0,)), ((), ()))


def _dot(a, b):
    return jnp.dot(a, b, preferred_element_type=F32)


def _rms_scale(x):
    return lax.rsqrt(jnp.mean(x * x, axis=-1, keepdims=True) + EPS)


def _inproj_kernel(x_ref, g1_ref, w_ref, wgate_ref, bgate_ref, proj_ref, gt_ref, xn_ref):
    @pl.when(pl.program_id(1) == 0)
    def _():
        x = x_ref[...]
        xn = (x * _rms_scale(x) * g1_ref[...]).astype(BF16)
        xn_ref[...] = xn
        gates = _dot(xn, wgate_ref[...]) + bgate_ref[...]
        gt_ref[...] = gates.T[0:N_GATES, :]

    proj_ref[...] = _dot(xn_ref[...], w_ref[...]).astype(BF16)


def _inproj(x, g1, w_main, w_gate, b_gate, *, tm, tn):
    n = x.shape[0]
    return pl.pallas_call(
        _inproj_kernel,
        grid=(n // tm, D_PROJ // tn),
        in_specs=[
            pl.BlockSpec((tm, D_MODEL), lambda i, j: (i, 0)),
            pl.BlockSpec((1, D_MODEL), lambda i, j: (0, 0)),
            pl.BlockSpec((D_MODEL, tn), lambda i, j: (0, j)),
            pl.BlockSpec((D_MODEL, LANES), lambda i, j: (0, 0)),
            pl.BlockSpec((1, LANES), lambda i, j: (0, 0)),
        ],
        out_specs=[
            pl.BlockSpec((tm, tn), lambda i, j: (i, j)),
            pl.BlockSpec((N_GATES, tm), lambda i, j: (0, i)),
        ],
        out_shape=[
            jax.ShapeDtypeStruct((n, D_PROJ), BF16),
            jax.ShapeDtypeStruct((N_GATES, n), F32),
        ],
        scratch_shapes=[pltpu.VMEM((tm, D_MODEL), BF16)],
        compiler_params=pltpu.CompilerParams(
            dimension_semantics=("arbitrary", "arbitrary"), vmem_limit_bytes=VMEM_LIMIT),
        name="inproj",
    )(x, g1, w_main, w_gate, b_gate)


def _lane_scan(x, combine, identity):
    width = x.shape[1]
    lane = lax.broadcasted_iota(jnp.int32, x.shape, 1)
    shift = 1
    while shift < width:
        shifted = jnp.where(lane >= shift, pltpu.roll(x, shift, axis=1), identity)
        x = combine(x, shifted)
        shift *= 2
    return x


def _mlstm_kernel(*refs, chunk, has_init):
    if has_init:
        (q_ref, k_ref, v_ref, o_ref, gt_ref, gain_ref, c0_ref, n0_ref, m0_ref,
         mix_ref, c_ref, n_ref, m_ref) = refs
    else:
        q_ref, k_ref, v_ref, o_ref, gt_ref, gain_ref, mix_ref, c_ref, n_ref, m_ref = refs
    L = chunk

    @pl.when(pl.program_id(1) == 0)
    def _():
        if has_init:
            c_ref[...] = c0_ref[...]
            n_ref[...] = n0_ref[...]
            m_ref[...] = m0_ref[...]
        else:
            c_ref[...] = jnp.zeros_like(c_ref)
            n_ref[...] = jnp.zeros_like(n_ref)
            m_ref[...] = jnp.zeros_like(m_ref)

    gates = gt_ref[0]
    b = _lane_scan(jax.nn.log_sigmoid(gates), jnp.add, 0.0)
    b = pltpu.roll(b, N_HEADS, axis=0)
    u = gates - b
    m_prev = m_ref[0][:, 0:1]
    big_m = jnp.maximum(m_prev, _lane_scan(u, jnp.maximum, -jnp.inf))
    a = jnp.exp(m_prev - big_m)
    e = jnp.exp(-(b + big_m))
    m_last = big_m[:, L - 1:L]
    g = jnp.exp(u - m_last)
    dec = a[:, L - 1:L]
    m_new = b[:, L - 1:L] + m_last
    rows = jnp.concatenate(
        [big_m, a, e, g, jnp.zeros((LANES - 4 * SUBLANES, gates.shape[1]), F32)], axis=0)
    cols = rows.T

    t_idx = lax.broadcasted_iota(jnp.int32, (L, L), 0)
    s_idx = lax.broadcasted_iota(jnp.int32, (L, L), 1)
    causal = s_idx <= t_idx

    for h in range(N_HEADS):
        sl = slice(h * HEAD_DIM, (h + 1) * HEAD_DIM)
        qh = q_ref[:, sl]
        kh = k_ref[:, sl] * jnp.asarray(K_SCALE, BF16)
        vh = v_ref[:, sl]
        m_col = cols[:L, h:h + 1]
        a_col = cols[:L, SUBLANES + h:SUBLANES + h + 1]
        e_col = cols[:L, 2 * SUBLANES + h:2 * SUBLANES + h + 1]
        g_col = cols[:L, 3 * SUBLANES + h:3 * SUBLANES + h + 1]
        u_row = u[h:h + 1, :L]

        s = lax.dot_general(qh, kh, _NT, preferred_element_type=F32)
        sw = s * jnp.exp(jnp.where(causal, u_row - m_col, -jnp.inf))
        c_h = c_ref[0, h]
        n_h = n_ref[0, h:h + 1, :]
        num = a_col * _dot(qh, c_h.astype(BF16)) + _dot(sw.astype(BF16), vh)
        qn = jnp.sum(qh.astype(F32) * n_h, axis=-1, keepdims=True)
        den = a_col * qn + jnp.sum(sw, axis=-1, keepdims=True)
        hh = num * (1.0 / jnp.maximum(jnp.abs(den), e_col))
        y = hh * _rms_scale(hh) * gain_ref[:, sl] * jax.nn.sigmoid(o_ref[:, sl].astype(F32))
        mix_ref[:, sl] = y.astype(BF16)

        kg = kh.astype(F32) * g_col
        dec_h = dec[h:h + 1, :]
        c_ref[0, h] = dec_h * c_h + lax.dot_general(
            kg.astype(BF16), vh, _TN, preferred_element_type=F32)
        n_ref[0, h:h + 1, :] = dec_h * n_h + jnp.sum(kg, axis=0, keepdims=True)

    m_ref[0] = jnp.broadcast_to(m_new, (SUBLANES, LANES))


def _mlstm(proj, gt3, gain, init, *, batch, seq, chunk):
    nc = seq // chunk
    lp = gt3.shape[2] // nc
    has_init = init is not None
    tok = lambda col: pl.BlockSpec((chunk, D_GROUP), lambda b, c, col=col: (b * nc + c, col))
    state_specs = [
        pl.BlockSpec((1, N_HEADS, HEAD_DIM, HEAD_DIM), lambda b, c: (b, 0, 0, 0)),
        pl.BlockSpec((1, N_HEADS, HEAD_DIM), lambda b, c: (b, 0, 0)),
        pl.BlockSpec((1, SUBLANES, LANES), lambda b, c: (b, 0, 0)),
    ]
    in_specs = [tok(0), tok(1), tok(2), tok(3),
                pl.BlockSpec((1, SUBLANES, lp), lambda b, c: (b, 0, c)),
                pl.BlockSpec((1, D_GROUP), lambda b, c: (0, 0))]
    args = [proj, proj, proj, proj, gt3, gain]
    if has_init:
        in_specs += state_specs
        args += list(init)
    return pl.pallas_call(
        functools.partial(_mlstm_kernel, chunk=chunk, has_init=has_init),
        grid=(batch, nc),
        in_specs=in_specs,
        out_specs=[pl.BlockSpec((chunk, D_GROUP), lambda b, c: (b * nc + c, 0))] + state_specs,
        out_shape=[
            jax.ShapeDtypeStruct((batch * seq, D_GROUP), BF16),
            jax.ShapeDtypeStruct((batch, N_HEADS, HEAD_DIM, HEAD_DIM), F32),
            jax.ShapeDtypeStruct((batch, N_HEADS, HEAD_DIM), F32),
            jax.ShapeDtypeStruct((batch, SUBLANES, LANES), F32),
        ],
        compiler_params=pltpu.CompilerParams(
            dimension_semantics=("arbitrary", "arbitrary"), vmem_limit_bytes=VMEM_LIMIT),
        name="mlstm",
    )(*args)


def _ret_kernel(*refs, chunk, has_init):
    if has_init:
        (lg_ref, q_ref, k_ref, v_ref, g_ref, cos_ref, sin_ref, gain_ref, r0_ref,
         mix_ref, r_ref) = refs
    else:
        lg_ref, q_ref, k_ref, v_ref, g_ref, cos_ref, sin_ref, gain_ref, mix_ref, r_ref = refs
    L = chunk
    half = HEAD_DIM // 2

    @pl.when(pl.program_id(1) == 0)
    def _():
        if has_init:
            r_ref[...] = r0_ref[...]
        else:
            r_ref[...] = jnp.zeros_like(r_ref)

    cos = cos_ref[...]
    sin = sin_ref[...]
    cos_k = cos * K_SCALE
    sin_k = sin * K_SCALE
    t_idx = lax.broadcasted_iota(jnp.int32, (L, L), 0)
    s_idx = lax.broadcasted_iota(jnp.int32, (L, L), 1)
    diff = t_idx - s_idx
    diff_pos = jnp.maximum(diff, 0).astype(F32)
    t_col = lax.broadcasted_iota(jnp.int32, (L, 1), 0).astype(F32)

    for h in range(N_HEADS):
        lo = slice(h * HEAD_DIM, h * HEAD_DIM + half)
        hi = slice(h * HEAD_DIM + half, (h + 1) * HEAD_DIM)
        sl = slice(h * HEAD_DIM, (h + 1) * HEAD_DIM)
        lg = lg_ref[h]
        q1 = q_ref[:, lo].astype(F32)
        q2 = q_ref[:, hi].astype(F32)
        k1 = k_ref[:, lo].astype(F32)
        k2 = k_ref[:, hi].astype(F32)
        qr = jnp.concatenate([q1 * cos - q2 * sin, q2 * cos + q1 * sin], axis=-1).astype(BF16)
        kr = jnp.concatenate([k1 * cos_k - k2 * sin_k, k2 * cos_k + k1 * sin_k], axis=-1)
        vh = v_ref[:, sl]

        dmask = jnp.where(diff >= 0, jnp.exp(lg * diff_pos), 0.0)
        inter = jnp.exp(lg * (t_col + 1.0))
        kdec = jnp.exp(lg * (L - 1.0 - t_col))
        cdec = jnp.exp(lg * jnp.full((1, 1), L, F32))

        s = lax.dot_general(qr, kr.astype(BF16), _NT, preferred_element_type=F32)
        r_h = r_ref[0, h]
        o = _dot((s * dmask).astype(BF16), vh) + inter * _dot(qr, r_h.astype(BF16))
        r_ref[0, h] = cdec * r_h + lax.dot_general(
            (kr * kdec).astype(BF16), vh, _TN, preferred_element_type=F32)

        oc = o - jnp.mean(o, axis=-1, keepdims=True)
        y = oc * _rms_scale(oc) * gain_ref[:, sl] * jax.nn.silu(g_ref[:, sl].astype(F32))
        mix_ref[:, sl] = y.astype(BF16)


def _retention(proj, log_gamma, cos, sin, gain, init, *, batch, seq, chunk):
    nc = seq // chunk
    has_init = init is not None
    tok = lambda col: pl.BlockSpec((chunk, D_GROUP), lambda b, c, col=col: (b * nc + c, col))
    state_spec = pl.BlockSpec((1, N_HEADS, HEAD_DIM, HEAD_DIM), lambda b, c: (b, 0, 0, 0))
    rope_spec = pl.BlockSpec((chunk, HEAD_DIM // 2), lambda b, c: (c, 0))
    in_specs = [pl.BlockSpec(memory_space=pltpu.SMEM),
                tok(4), tok(5), tok(6), tok(7), rope_spec, rope_spec,
                pl.BlockSpec((1, D_GROUP), lambda b, c: (0, 0))]
    args = [log_gamma, proj, proj, proj, proj, cos, sin, gain]
    if has_init:
        in_specs.append(state_spec)
        args.append(init)
    return pl.pallas_call(
        functools.partial(_ret_kernel, chunk=chunk, has_init=has_init),
        grid=(batch, nc),
        in_specs=in_specs,
        out_specs=[pl.BlockSpec((chunk, D_GROUP), lambda b, c: (b * nc + c, 0)), state_spec],
        out_shape=[
            jax.ShapeDtypeStruct((batch * seq, D_GROUP), BF16),
            jax.ShapeDtypeStruct((batch, N_HEADS, HEAD_DIM, HEAD_DIM), F32),
        ],
        compiler_params=pltpu.CompilerParams(
            dimension_semantics=("arbitrary", "arbitrary"), vmem_limit_bytes=VMEM_LIMIT),
        name="retention",
    )(*args)


def _outproj_kernel(x_ref, mm_ref, mr_ref, wm_ref, wr_ref, g2_ref, x1_ref, hn_ref):
    x1 = x_ref[...] + _dot(mm_ref[...], wm_ref[...]) + _dot(mr_ref[...], wr_ref[...])
    x1_ref[...] = x1
    hn_ref[...] = (x1 * _rms_scale(x1) * g2_ref[...]).astype(BF16)


def _outproj(x, mix_m, mix_r, w_out, g2, *, tm):
    n = x.shape[0]
    return pl.pallas_call(
        _outproj_kernel,
        grid=(n // tm,),
        in_specs=[
            pl.BlockSpec((tm, D_MODEL), lambda i: (i, 0)),
            pl.BlockSpec((tm, D_GROUP), lambda i: (i, 0)),
            pl.BlockSpec((tm, D_GROUP), lambda i: (i, 0)),
            pl.BlockSpec((D_GROUP, D_MODEL), lambda i: (0, 0)),
            pl.BlockSpec((D_GROUP, D_MODEL), lambda i: (1, 0)),
            pl.BlockSpec((1, D_MODEL), lambda i: (0, 0)),
        ],
        out_specs=[
            pl.BlockSpec((tm, D_MODEL), lambda i: (i, 0)),
            pl.BlockSpec((tm, D_MODEL), lambda i: (i, 0)),
        ],
        out_shape=[
            jax.ShapeDtypeStruct((n, D_MODEL), F32),
            jax.ShapeDtypeStruct((n, D_MODEL), BF16),
        ],
        compiler_params=pltpu.CompilerParams(
            dimension_semantics=("arbitrary",), vmem_limit_bytes=VMEM_LIMIT),
        name="outproj",
    )(x, mix_m, mix_r, w_out, w_out, g2)


def _ffn_kernel(hn_ref, x1_ref, wg_ref, wu_ref, wd_ref, gf_ref, y_ref, acc_ref):
    f = pl.program_id(1)
    hn = hn_ref[...]
    gate = _dot(hn, wg_ref[...])
    up = _dot(hn, wu_ref[...])
    part = _dot((jax.nn.silu(gate) * up).astype(BF16), wd_ref[...])

    @pl.when(f == 0)
    def _():
        acc_ref[...] = part

    @pl.when(f > 0)
    def _():
        acc_ref[...] += part

    @pl.when(f == pl.num_programs(1) - 1)
    def _():
        x2 = x1_ref[...] + acc_ref[...]
        y_ref[...] = x2 * _rms_scale(x2) * gf_ref[...]


def _ffn(hn, x1, w_gate, w_up, w_down, g_final, *, tm, tf):
    n = hn.shape[0]
    return pl.pallas_call(
        _ffn_kernel,
        grid=(n // tm, D_FF // tf),
        in_specs=[
            pl.BlockSpec((tm, D_MODEL), lambda i, f: (i, 0)),
            pl.BlockSpec((tm, D_MODEL), lambda i, f: (i, 0)),
            pl.BlockSpec((D_MODEL, tf), lambda i, f: (0, f)),
            pl.BlockSpec((D_MODEL, tf), lambda i, f: (0, f)),
            pl.BlockSpec((tf, D_MODEL), lambda i, f: (f, 0)),
            pl.BlockSpec((1, D_MODEL), lambda i, f: (0, 0)),
        ],
        out_specs=pl.BlockSpec((tm, D_MODEL), lambda i, f: (i, 0)),
        out_shape=jax.ShapeDtypeStruct((n, D_MODEL), F32),
        scratch_shapes=[pltpu.VMEM((tm, D_MODEL), F32)],
        compiler_params=pltpu.CompilerParams(
            dimension_semantics=("arbitrary", "arbitrary"), vmem_limit_bytes=VMEM_LIMIT),
        name="ffn",
    )(hn, x1, w_gate, w_up, w_down, g_final)


def _rope_tables(pos):
    half = HEAD_DIM // 2
    freqs = ROPE_BASE ** (-jnp.arange(half, dtype=F32) / half)
    ang = pos[:, None] * freqs[None, :]
    return jnp.cos(ang), jnp.sin(ang)


def _trunk(x, pos, init, weights, *, chunk):
    batch, seq, _ = x.shape
    n = batch * seq
    (g1, w_main, w_gatecols, b_gate, gain_m, gain_r, w_out, g2, w_gate, w_up, w_down,
     g_final, log_gamma) = weights
    x2d = x.reshape(n, D_MODEL)

    proj, gt = _inproj(x2d, g1, w_main, w_gatecols, b_gate, tm=1024, tn=1024)
    gt3 = gt.reshape(N_GATES, batch, seq).transpose(1, 0, 2)
    if seq < LANES:
        gt3 = jnp.pad(gt3, ((0, 0), (0, 0), (0, LANES - seq)))

    if init is None:
        m_init = r_init = None
    else:
        c0, n0, m0, r0 = init
        m0 = jnp.pad(m0, ((0, 0), (0, SUBLANES - N_HEADS)))
        m0 = jnp.broadcast_to(m0[:, :, None], (batch, SUBLANES, LANES))
        m_init, r_init = (c0, n0, m0), r0

    mix_m, c_new, n_new, m_new = _mlstm(proj, gt3, gain_m, m_init,
                                        batch=batch, seq=seq, chunk=chunk)
    cos, sin = _rope_tables(pos)
    mix_r, r_new = _retention(proj, log_gamma, cos, sin, gain_r, r_init,
                              batch=batch, seq=seq, chunk=chunk)

    x1, hn = _outproj(x2d, mix_m, mix_r, w_out, g2, tm=512)
    y = _ffn(hn, x1, w_gate, w_up, w_down, g_final, tm=512, tf=512)
    return (y.reshape(batch, seq, D_MODEL), c_new[None], n_new[None],
            m_new[:, :N_HEADS, 0][None], r_new[None])


def kernel(x_prompt, x_sample, state_mlstm_C, state_mlstm_n, state_mlstm_m, state_ret,
           g_norm1, w_in, b_gates, g_mlstm_norm, g_ret_norm, w_out, g_norm2,
           w_gate, w_up, w_down, g_final):
    w_in0 = w_in[0]
    w_gatecols = jnp.pad(w_in0[:, D_PROJ:], ((0, 0), (0, LANES - N_GATES))).astype(BF16)
    b_gate = jnp.pad(b_gates[0], (0, LANES - N_GATES)).reshape(1, LANES)
    log_gamma = jnp.log(1.0 - jnp.exp2(-5.0 - jnp.arange(N_HEADS, dtype=F32)))
    weights = (
        g_norm1[0].reshape(1, D_MODEL), w_in0[:, :D_PROJ].astype(BF16), w_gatecols, b_gate,
        g_mlstm_norm[0].reshape(1, D_GROUP), g_ret_norm[0].reshape(1, D_GROUP),
        w_out[0].astype(BF16), g_norm2[0].reshape(1, D_MODEL),
        w_gate[0].astype(BF16), w_up[0].astype(BF16), w_down[0].astype(BF16),
        g_final.reshape(1, D_MODEL), log_gamma,
    )
    seq_p = x_prompt.shape[1]
    seq_s = x_sample.shape[1]
    pos_p = jnp.arange(seq_p, dtype=F32)
    pos_s = PAST_LEN + jnp.arange(seq_s, dtype=F32)

    y_p, c_p, n_p, m_p, r_p = _trunk(x_prompt, pos_p, None, weights, chunk=PROMPT_CHUNK)
    init_s = (state_mlstm_C[0], state_mlstm_n[0], state_mlstm_m[0], state_ret[0])
    y_s, c_s, n_s, m_s, r_s = _trunk(x_sample, pos_s, init_s, weights, chunk=seq_s)
    return (y_p, y_s, c_p, n_p, m_p, r_p, c_s, n_s, m_s, r_s)
```

```python
import functools

import jax
import jax.numpy as jnp
from jax import lax
from jax.experimental import pallas as pl
from jax.experimental.pallas import tpu as pltpu

F32 = jnp.float32
BF16 = jnp.bfloat16

D_MODEL = 2048
N_HEADS = 4
HEAD_DIM = 256
D_GROUP = N_HEADS * HEAD_DIM
D_PROJ = 8 * D_GROUP
N_GATES = 2 * N_HEADS
D_FF = 5632
ROPE_BASE = 10000.0
PAST_LEN = 4096
EPS = 1e-6
K_SCALE = HEAD_DIM ** -0.5

LANES = 128
SUBLANES = 8
VMEM_LIMIT = 56 * 1024 * 1024

PROMPT_CHUNK = 256

_NT = (((1,), (1,)), ((), ()))
_TN = (((0,), (0,)), ((), ()))


def _dot(a, b):
    return jnp.dot(a, b, preferred_element_type=F32)


def _rms_scale(x):
    return lax.rsqrt(jnp.mean(x * x, axis=-1, keepdims=True) + EPS)


def _inproj_kernel(x_ref, g1_ref, w_ref, wgate_ref, bgate_ref, proj_ref, gt_ref, xn_ref):
    @pl.when(pl.program_id(1) == 0)
    def _():
        x = x_ref[...]
        xn = (x * _rms_scale(x) * g1_ref[...]).astype(BF16)
        xn_ref[...] = xn
        gates = _dot(xn, wgate_ref[...]) + bgate_ref[...]
        gt_ref[...] = gates.T[0:N_GATES, :]

    proj_ref[...] = _dot(xn_ref[...], w_ref[...]).astype(BF16)


def _inproj(x, g1, w_main, w_gate, b_gate, *, tm, tn):
    n = x.shape[0]
    return pl.pallas_call(
        _inproj_kernel,
        grid=(n // tm, D_PROJ // tn),
        in_specs=[
            pl.BlockSpec((tm, D_MODEL), lambda i, j: (i, 0)),
            pl.BlockSpec((1, D_MODEL), lambda i, j: (0, 0)),
            pl.BlockSpec((D_MODEL, tn), lambda i, j: (0, j)),
            pl.BlockSpec((D_MODEL, LANES), lambda i, j: (0, 0)),
            pl.BlockSpec((1, LANES), lambda i, j: (0, 0)),
        ],
        out_specs=[
            pl.BlockSpec((tm, tn), lambda i, j: (i, j)),
            pl.BlockSpec((N_GATES, tm), lambda i, j: (0, i)),
        ],
        out_shape=[
            jax.ShapeDtypeStruct((n, D_PROJ), BF16),
            jax.ShapeDtypeStruct((N_GATES, n), F32),
        ],
        scratch_shapes=[pltpu.VMEM((tm, D_MODEL), BF16)],
        compiler_params=pltpu.CompilerParams(
            dimension_semantics=("arbitrary", "arbitrary"), vmem_limit_bytes=VMEM_LIMIT),
        name="inproj",
    )(x, g1, w_main, w_gate, b_gate)


def _lane_scan(x, combine, identity):
    width = x.shape[1]
    lane = lax.broadcasted_iota(jnp.int32, x.shape, 1)
    shift = 1
    while shift < width:
        shifted = jnp.where(lane >= shift, pltpu.roll(x, shift, axis=1), identity)
        x = combine(x, shifted)
        shift *= 2
    return x


def _mlstm_kernel(*refs, chunk, has_init):
    if has_init:
        (q_ref, k_ref, v_ref, o_ref, gt_ref, gain_ref, c0_ref, n0_ref, m0_ref,
         mix_ref, c_ref, n_ref, m_ref) = refs
    else:
        q_ref, k_ref, v_ref, o_ref, gt_ref, gain_ref, mix_ref, c_ref, n_ref, m_ref = refs
    L = chunk

    @pl.when(pl.program_id(1) == 0)
    def _():
        if has_init:
            c_ref[...] = c0_ref[...]
            n_ref[...] = n0_ref[...]
            m_ref[...] = m0_ref[...]
        else:
            c_ref[...] = jnp.zeros_like(c_ref)
            n_ref[...] = jnp.zeros_like(n_ref)
            m_ref[...] = jnp.zeros_like(m_ref)

    gates = gt_ref[0]
    b = _lane_scan(jax.nn.log_sigmoid(gates), jnp.add, 0.0)
    b = pltpu.roll(b, N_HEADS, axis=0)
    u = gates - b
    m_prev = m_ref[0][:, 0:1]
    big_m = jnp.maximum(m_prev, _lane_scan(u, jnp.maximum, -jnp.inf))
    a = jnp.exp(m_prev - big_m)
    e = jnp.exp(-(b + big_m))
    m_last = big_m[:, L - 1:L]
    g = jnp.exp(u - m_last)
    dec = a[:, L - 1:L]
    m_new = b[:, L - 1:L] + m_last
    rows = jnp.concatenate(
        [big_m, a, e, g, jnp.zeros((LANES - 4 * SUBLANES, gates.shape[1]), F32)], axis=0)
    cols = rows.T

    t_idx = lax.broadcasted_iota(jnp.int32, (L, L), 0)
    s_idx = lax.broadcasted_iota(jnp.int32, (L, L), 1)
    causal = s_idx <= t_idx

    for h in range(N_HEADS):
        sl = slice(h * HEAD_DIM, (h + 1) * HEAD_DIM)
        qh = q_ref[:, sl]
        kh = k_ref[:, sl] * jnp.asarray(K_SCALE, BF16)
        vh = v_ref[:, sl]
        m_col = cols[:L, h:h + 1]
        a_col = cols[:L, SUBLANES + h:SUBLANES + h + 1]
        e_col = cols[:L, 2 * SUBLANES + h:2 * SUBLANES + h + 1]
        g_col = cols[:L, 3 * SUBLANES + h:3 * SUBLANES + h + 1]
        u_row = u[h:h + 1, :L]

        s = lax.dot_general(qh, kh, _NT, preferred_element_type=F32)
        sw = s * jnp.exp(jnp.where(causal, u_row - m_col, -jnp.inf))
        c_h = c_ref[0, h]
        n_h = n_ref[0, h:h + 1, :]
        num = a_col * _dot(qh, c_h.astype(BF16)) + _dot(sw.astype(BF16), vh)
        qn = jnp.sum(qh.astype(F32) * n_h, axis=-1, keepdims=True)
        den = a_col * qn + jnp.sum(sw, axis=-1, keepdims=True)
        hh = num * (1.0 / jnp.maximum(jnp.abs(den), e_col))
        y = hh * _rms_scale(hh) * gain_ref[:, sl] * jax.nn.sigmoid(o_ref[:, sl].astype(F32))
        mix_ref[:, sl] = y.astype(BF16)

        kg = kh.astype(F32) * g_col
        dec_h = dec[h:h + 1, :]
        c_ref[0, h] = dec_h * c_h + lax.dot_general(
            kg.astype(BF16), vh, _TN, preferred_element_type=F32)
        n_ref[0, h:h + 1, :] = dec_h * n_h + jnp.sum(kg, axis=0, keepdims=True)

    m_ref[0] = jnp.broadcast_to(m_new, (SUBLANES, LANES))


def _mlstm(proj, gt3, gain, init, *, batch, seq, chunk):
    nc = seq // chunk
    lp = gt3.shape[2] // nc
    has_init = init is not None
    tok = lambda col: pl.BlockSpec((chunk, D_GROUP), lambda b, c, col=col: (b * nc + c, col))
    state_specs = [
        pl.BlockSpec((1, N_HEADS, HEAD_DIM, HEAD_DIM), lambda b, c: (b, 0, 0, 0)),
        pl.BlockSpec((1, N_HEADS, HEAD_DIM), lambda b, c: (b, 0, 0)),
        pl.BlockSpec((1, SUBLANES, LANES), lambda b, c: (b, 0, 0)),
    ]
    in_specs = [tok(0), tok(1), tok(2), tok(3),
                pl.BlockSpec((1, SUBLANES, lp), lambda b, c: (b, 0, c)),
                pl.BlockSpec((1, D_GROUP), lambda b, c: (0, 0))]
    args = [proj, proj, proj, proj, gt3, gain]
    if has_init:
        in_specs += state_specs
        args += list(init)
    return pl.pallas_call(
        functools.partial(_mlstm_kernel, chunk=chunk, has_init=has_init),
        grid=(batch, nc),
        in_specs=in_specs,
        out_specs=[pl.BlockSpec((chunk, D_GROUP), lambda b, c: (b * nc + c, 0))] + state_specs,
        out_shape=[
            jax.ShapeDtypeStruct((batch * seq, D_GROUP), BF16),
            jax.ShapeDtypeStruct((batch, N_HEADS, HEAD_DIM, HEAD_DIM), F32),
            jax.ShapeDtypeStruct((batch, N_HEADS, HEAD_DIM), F32),
            jax.ShapeDtypeStruct((batch, SUBLANES, LANES), F32),
        ],
        compiler_params=pltpu.CompilerParams(
            dimension_semantics=("arbitrary", "arbitrary"), vmem_limit_bytes=VMEM_LIMIT),
        name="mlstm",
    )(*args)


def _ret_kernel(*refs, chunk, has_init):
    if has_init:
        (lg_ref, q_ref, k_ref, v_ref, g_ref, cos_ref, sin_ref, gain_ref, r0_ref,
         mix_ref, r_ref) = refs
    else:
        lg_ref, q_ref, k_ref, v_ref, g_ref, cos_ref, sin_ref, gain_ref, mix_ref, r_ref = refs
    L = chunk
    half = HEAD_DIM // 2

    @pl.when(pl.program_id(1) == 0)
    def _():
        if has_init:
            r_ref[...] = r0_ref[...]
        else:
            r_ref[...] = jnp.zeros_like(r_ref)

    cos = cos_ref[...]
    sin = sin_ref[...]
    cos_k = cos * K_SCALE
    sin_k = sin * K_SCALE
    t_idx = lax.broadcasted_iota(jnp.int32, (L, L), 0)
    s_idx = lax.broadcasted_iota(jnp.int32, (L, L), 1)
    diff = t_idx - s_idx
    diff_pos = jnp.maximum(diff, 0).astype(F32)
    t_col = lax.broadcasted_iota(jnp.int32, (L, 1), 0).astype(F32)

    for h in range(N_HEADS):
        lo = slice(h * HEAD_DIM, h * HEAD_DIM + half)
        hi = slice(h * HEAD_DIM + half, (h + 1) * HEAD_DIM)
        sl = slice(h * HEAD_DIM, (h + 1) * HEAD_DIM)
        lg = lg_ref[h]
        q1 = q_ref[:, lo].astype(F32)
        q2 = q_ref[:, hi].astype(F32)
        k1 = k_ref[:, lo].astype(F32)
        k2 = k_ref[:, hi].astype(F32)
        qr = jnp.concatenate([q1 * cos - q2 * sin, q2 * cos + q1 * sin], axis=-1).astype(BF16)
        kr = jnp.concatenate([k1 * cos_k - k2 * sin_k, k2 * cos_k + k1 * sin_k], axis=-1)
        vh = v_ref[:, sl]

        dmask = jnp.where(diff >= 0, jnp.exp(lg * diff_pos), 0.0)
        inter = jnp.exp(lg * (t_col + 1.0))
        kdec = jnp.exp(lg * (L - 1.0 - t_col))
        cdec = jnp.exp(lg * jnp.full((1, 1), L, F32))

        s = lax.dot_general(qr, kr.astype(BF16), _NT, preferred_element_type=F32)
        r_h = r_ref[0, h]
        o = _dot((s * dmask).astype(BF16), vh) + inter * _dot(qr, r_h.astype(BF16))
        r_ref[0, h] = cdec * r_h + lax.dot_general(
            (kr * kdec).astype(BF16), vh, _TN, preferred_element_type=F32)

        oc = o - jnp.mean(o, axis=-1, keepdims=True)
        y = oc * _rms_scale(oc) * gain_ref[:, sl] * jax.nn.silu(g_ref[:, sl].astype(F32))
        mix_ref[:, sl] = y.astype(BF16)


def _retention(proj, log_gamma, cos, sin, gain, init, *, batch, seq, chunk):
    nc = seq // chunk
    has_init = init is not None
    tok = lambda col: pl.BlockSpec((chunk, D_GROUP), lambda b, c, col=col: (b * nc + c, col))
    state_spec = pl.BlockSpec((1, N_HEADS, HEAD_DIM, HEAD_DIM), lambda b, c: (b, 0, 0, 0))
    rope_spec = pl.BlockSpec((chunk, HEAD_DIM // 2), lambda b, c: (c, 0))
    in_specs = [pl.BlockSpec(memory_space=pltpu.SMEM),
                tok(4), tok(5), tok(6), tok(7), rope_spec, rope_spec,
                pl.BlockSpec((1, D_GROUP), lambda b, c: (0, 0))]
    args = [log_gamma, proj, proj, proj, proj, cos, sin, gain]
    if has_init:
        in_specs.append(state_spec)
        args.append(init)
    return pl.pallas_call(
        functools.partial(_ret_kernel, chunk=chunk, has_init=has_init),
        grid=(batch, nc),
        in_specs=in_specs,
        out_specs=[pl.BlockSpec((chunk, D_GROUP), lambda b, c: (b * nc + c, 0)), state_spec],
        out_shape=[
            jax.ShapeDtypeStruct((batch * seq, D_GROUP), BF16),
            jax.ShapeDtypeStruct((batch, N_HEADS, HEAD_DIM, HEAD_DIM), F32),
        ],
        compiler_params=pltpu.CompilerParams(
            dimension_semantics=("arbitrary", "arbitrary"), vmem_limit_bytes=VMEM_LIMIT),
        name="retention",
    )(*args)


def _outproj_kernel(x_ref, mm_ref, mr_ref, wm_ref, wr_ref, g2_ref, x1_ref, hn_ref):
    x1 = x_ref[...] + _dot(mm_ref[...], wm_ref[...]) + _dot(mr_ref[...], wr_ref[...])
    x1_ref[...] = x1
    hn_ref[...] = (x1 * _rms_scale(x1) * g2_ref[...]).astype(BF16)


def _outproj(x, mix_m, mix_r, w_out, g2, *, tm):
    n = x.shape[0]
    return pl.pallas_call(
        _outproj_kernel,
        grid=(n // tm,),
        in_specs=[
            pl.BlockSpec((tm, D_MODEL), lambda i: (i, 0)),
            pl.BlockSpec((tm, D_GROUP), lambda i: (i, 0)),
            pl.BlockSpec((tm, D_GROUP), lambda i: (i, 0)),
            pl.BlockSpec((D_GROUP, D_MODEL), lambda i: (0, 0)),
            pl.BlockSpec((D_GROUP, D_MODEL), lambda i: (1, 0)),
            pl.BlockSpec((1, D_MODEL), lambda i: (0, 0)),
        ],
        out_specs=[
            pl.BlockSpec((tm, D_MODEL), lambda i: (i, 0)),
            pl.BlockSpec((tm, D_MODEL), lambda i: (i, 0)),
        ],
        out_shape=[
            jax.ShapeDtypeStruct((n, D_MODEL), F32),
            jax.ShapeDtypeStruct((n, D_MODEL), BF16),
        ],
        compiler_params=pltpu.CompilerParams(
            dimension_semantics=("arbitrary",), vmem_limit_bytes=VMEM_LIMIT),
        name="outproj",
    )(x, mix_m, mix_r, w_out, w_out, g2)


def _ffn_kernel(hn_ref, x1_ref, wg_ref, wu_ref, wd_ref, gf_ref, y_ref):
    f = pl.program_id(1)

    @pl.when(f == 0)
    def _():
        y_ref[...] = x1_ref[...]

    hn = hn_ref[...]
    gate = _dot(hn, wg_ref[...])
    up = _dot(hn, wu_ref[...])
    y_ref[...] += _dot((jax.nn.silu(gate) * up).astype(BF16), wd_ref[...])

    @pl.when(f == pl.num_programs(1) - 1)
    def _():
        x2 = y_ref[...]
        y_ref[...] = x2 * _rms_scale(x2) * gf_ref[...]


def _ffn(hn, x1, w_gate, w_up, w_down, g_final, *, tm, tf):
    n = hn.shape[0]
    return pl.pallas_call(
        _ffn_kernel,
        grid=(n // tm, D_FF // tf),
        in_specs=[
            pl.BlockSpec((tm, D_MODEL), lambda i, f: (i, 0)),
            pl.BlockSpec((tm, D_MODEL), lambda i, f: (i, 0)),
            pl.BlockSpec((D_MODEL, tf), lambda i, f: (0, f)),
            pl.BlockSpec((D_MODEL, tf), lambda i, f: (0, f)),
            pl.BlockSpec((tf, D_MODEL), lambda i, f: (f, 0)),
            pl.BlockSpec((1, D_MODEL), lambda i, f: (0, 0)),
        ],
        out_specs=pl.BlockSpec((tm, D_MODEL), lambda i, f: (i, 0)),
        out_shape=jax.ShapeDtypeStruct((n, D_MODEL), F32),
        compiler_params=pltpu.CompilerParams(
            dimension_semantics=("arbitrary", "arbitrary"), vmem_limit_bytes=VMEM_LIMIT),
        name="ffn",
    )(hn, x1, w_gate, w_up, w_down, g_final)


def _rope_tables(pos):
    half = HEAD_DIM // 2
    freqs = ROPE_BASE ** (-jnp.arange(half, dtype=F32) / half)
    ang = pos[:, None] * freqs[None, :]
    return jnp.cos(ang), jnp.sin(ang)


def _trunk(x, pos, init, weights, *, chunk):
    batch, seq, _ = x.shape
    n = batch * seq
    (g1, w_main, w_gatecols, b_gate, gain_m, gain_r, w_out, g2, w_gate, w_up, w_down,
     g_final, log_gamma) = weights
    x2d = x.reshape(n, D_MODEL)

    proj, gt = _inproj(x2d, g1, w_main, w_gatecols, b_gate, tm=1024, tn=2048)
    gt3 = gt.reshape(N_GATES, batch, seq).transpose(1, 0, 2)
    if seq < LANES:
        gt3 = jnp.pad(gt3, ((0, 0), (0, 0), (0, LANES - seq)))

    if init is None:
        m_init = r_init = None
    else:
        c0, n0, m0, r0 = init
        m0 = jnp.pad(m0, ((0, 0), (0, SUBLANES - N_HEADS)))
        m0 = jnp.broadcast_to(m0[:, :, None], (batch, SUBLANES, LANES))
        m_init, r_init = (c0, n0, m0), r0

    mix_m, c_new, n_new, m_new = _mlstm(proj, gt3, gain_m, m_init,
                                        batch=batch, seq=seq, chunk=chunk)
    cos, sin = _rope_tables(pos)
    mix_r, r_new = _retention(proj, log_gamma, cos, sin, gain_r, r_init,
                              batch=batch, seq=seq, chunk=chunk)

    x1, hn = _outproj(x2d, mix_m, mix_r, w_out, g2, tm=512)
    y = _ffn(hn, x1, w_gate, w_up, w_down, g_final, tm=512, tf=512)
    return (y.reshape(batch, seq, D_MODEL), c_new[None], n_new[None],
            m_new[:, :N_HEADS, 0][None], r_new[None])


def kernel(x_prompt, x_sample, state_mlstm_C, state_mlstm_n, state_mlstm_m, state_ret,
           g_norm1, w_in, b_gates, g_mlstm_norm, g_ret_norm, w_out, g_norm2,
           w_gate, w_up, w_down, g_final):
    w_in0 = w_in[0].astype(BF16)
    w_gatecols = jnp.pad(w_in0[:, D_PROJ:], ((0, 0), (0, LANES - N_GATES)))
    b_gate = jnp.pad(b_gates[0], (0, LANES - N_GATES)).reshape(1, LANES)
    log_gamma = jnp.log(1.0 - jnp.exp2(-5.0 - jnp.arange(N_HEADS, dtype=F32)))
    weights = (
        g_norm1[0].reshape(1, D_MODEL), w_in0, w_gatecols, b_gate,
        g_mlstm_norm[0].reshape(1, D_GROUP), g_ret_norm[0].reshape(1, D_GROUP),
        w_out[0].astype(BF16), g_norm2[0].reshape(1, D_MODEL),
        w_gate[0].astype(BF16), w_up[0].astype(BF16), w_down[0].astype(BF16),
        g_final.reshape(1, D_MODEL), log_gamma,
    )
    seq_p = x_prompt.shape[1]
    seq_s = x_sample.shape[1]
    pos_p = jnp.arange(seq_p, dtype=F32)
    pos_s = PAST_LEN + jnp.arange(seq_s, dtype=F32)

    y_p, c_p, n_p, m_p, r_p = _trunk(x_prompt, pos_p, None, weights, chunk=PROMPT_CHUNK)
    init_s = (state_mlstm_C[0], state_mlstm_n[0], state_mlstm_m[0], state_ret[0])
    y_s, c_s, n_s, m_s, r_s = _trunk(x_sample, pos_s, init_s, weights, chunk=seq_s)
    return (y_p, y_s, c_p, n_p, m_p, r_p, c_s, n_s, m_s, r_s)
```

```python
import functools

import jax
import jax.numpy as jnp
from jax import lax
from jax.experimental import pallas as pl
from jax.experimental.pallas import tpu as pltpu

F32 = jnp.float32
BF16 = jnp.bfloat16

D_MODEL = 2048
N_HEADS = 4
HEAD_DIM = 256
D_GROUP = N_HEADS * HEAD_DIM
D_PROJ = 8 * D_GROUP
N_GATES = 2 * N_HEADS
D_FF = 5632
ROPE_BASE = 10000.0
PAST_LEN = 4096
EPS = 1e-6
K_SCALE = HEAD_DIM ** -0.5

LANES = 128
SUBLANES = 8
VMEM_LIMIT = 56 * 1024 * 1024

PROMPT_CHUNK = 256

_NT = (((1,), (1,)), ((), ()))
_TN = (((0,), (0,)), ((), ()))


def _dot(a, b):
    return jnp.dot(a, b, preferred_element_type=F32)


def _rms_scale(x):
    return lax.rsqrt(jnp.mean(x * x, axis=-1, keepdims=True) + EPS)


def _inproj_kernel(x_ref, g1_ref, w_ref, wgate_ref, bgate_ref, proj_ref, gt_ref, xn_ref):
    @pl.when(pl.program_id(1) == 0)
    def _():
        x = x_ref[...]
        xn = (x * _rms_scale(x) * g1_ref[...]).astype(BF16)
        xn_ref[...] = xn
        gates = _dot(xn, wgate_ref[...]) + bgate_ref[...]
        gt_ref[...] = gates.T[0:N_GATES, :]

    proj_ref[...] = _dot(xn_ref[...], w_ref[...]).astype(BF16)


def _inproj(x, g1, w_main, w_gate, b_gate, *, tm, tn):
    n = x.shape[0]
    return pl.pallas_call(
        _inproj_kernel,
        grid=(n // tm, D_PROJ // tn),
        in_specs=[
            pl.BlockSpec((tm, D_MODEL), lambda i, j: (i, 0)),
            pl.BlockSpec((1, D_MODEL), lambda i, j: (0, 0)),
            pl.BlockSpec((D_MODEL, tn), lambda i, j: (0, j)),
            pl.BlockSpec((D_MODEL, LANES), lambda i, j: (0, 0)),
            pl.BlockSpec((1, LANES), lambda i, j: (0, 0)),
        ],
        out_specs=[
            pl.BlockSpec((tm, tn), lambda i, j: (i, j)),
            pl.BlockSpec((N_GATES, tm), lambda i, j: (0, i)),
        ],
        out_shape=[
            jax.ShapeDtypeStruct((n, D_PROJ), BF16),
            jax.ShapeDtypeStruct((N_GATES, n), F32),
        ],
        scratch_shapes=[pltpu.VMEM((tm, D_MODEL), BF16)],
        compiler_params=pltpu.CompilerParams(
            dimension_semantics=("arbitrary", "arbitrary"), vmem_limit_bytes=VMEM_LIMIT),
        name="inproj",
    )(x, g1, w_main, w_gate, b_gate)


def _lane_scan(x, combine, identity):
    width = x.shape[1]
    lane = lax.broadcasted_iota(jnp.int32, x.shape, 1)
    shift = 1
    while shift < width:
        shifted = jnp.where(lane >= shift, pltpu.roll(x, shift, axis=1), identity)
        x = combine(x, shifted)
        shift *= 2
    return x


def _mlstm_kernel(*refs, chunk, has_init):
    if has_init:
        (q_ref, k_ref, v_ref, o_ref, gt_ref, gain_ref, c0_ref, n0_ref, m0_ref,
         mix_ref, c_ref, n_ref, m_ref) = refs
    else:
        q_ref, k_ref, v_ref, o_ref, gt_ref, gain_ref, mix_ref, c_ref, n_ref, m_ref = refs
    L = chunk

    @pl.when(pl.program_id(1) == 0)
    def _():
        if has_init:
            c_ref[...] = c0_ref[...]
            n_ref[...] = n0_ref[...]
            m_ref[...] = m0_ref[...]
        else:
            c_ref[...] = jnp.zeros_like(c_ref)
            n_ref[...] = jnp.zeros_like(n_ref)
            m_ref[...] = jnp.zeros_like(m_ref)

    gates = gt_ref[0]
    b = _lane_scan(jax.nn.log_sigmoid(gates), jnp.add, 0.0)
    b = pltpu.roll(b, N_HEADS, axis=0)
    u = gates - b
    m_prev = m_ref[0][:, 0:1]
    big_m = jnp.maximum(m_prev, _lane_scan(u, jnp.maximum, -jnp.inf))
    a = jnp.exp(m_prev - big_m)
    e = jnp.exp(-(b + big_m))
    m_last = big_m[:, L - 1:L]
    g = jnp.exp(u - m_last)
    dec = a[:, L - 1:L]
    m_new = b[:, L - 1:L] + m_last
    rows = jnp.concatenate(
        [big_m, a, e, g, jnp.zeros((LANES - 4 * SUBLANES, gates.shape[1]), F32)], axis=0)
    cols = rows.T

    t_idx = lax.broadcasted_iota(jnp.int32, (L, L), 0)
    s_idx = lax.broadcasted_iota(jnp.int32, (L, L), 1)
    causal = s_idx <= t_idx

    for h in range(N_HEADS):
        sl = slice(h * HEAD_DIM, (h + 1) * HEAD_DIM)
        qh = q_ref[:, sl]
        kh = k_ref[:, sl] * jnp.asarray(K_SCALE, BF16)
        vh = v_ref[:, sl]
        m_col = cols[:L, h:h + 1]
        a_col = cols[:L, SUBLANES + h:SUBLANES + h + 1]
        e_col = cols[:L, 2 * SUBLANES + h:2 * SUBLANES + h + 1]
        g_col = cols[:L, 3 * SUBLANES + h:3 * SUBLANES + h + 1]
        u_row = u[h:h + 1, :L]

        s = lax.dot_general(qh, kh, _NT, preferred_element_type=F32)
        sw = s * jnp.exp(jnp.where(causal, u_row - m_col, -jnp.inf))
        c_h = c_ref[0, h]
        n_h = n_ref[0, h:h + 1, :]
        num = a_col * _dot(qh, c_h.astype(BF16)) + _dot(sw.astype(BF16), vh)
        qn = jnp.sum(qh.astype(F32) * n_h, axis=-1, keepdims=True)
        den = a_col * qn + jnp.sum(sw, axis=-1, keepdims=True)
        hh = num * (1.0 / jnp.maximum(jnp.abs(den), e_col))
        y = hh * _rms_scale(hh) * gain_ref[:, sl] * jax.nn.sigmoid(o_ref[:, sl].astype(F32))
        mix_ref[:, sl] = y.astype(BF16)

        kg = kh.astype(F32) * g_col
        dec_h = dec[h:h + 1, :]
        c_ref[0, h] = dec_h * c_h + lax.dot_general(
            kg.astype(BF16), vh, _TN, preferred_element_type=F32)
        n_ref[0, h:h + 1, :] = dec_h * n_h + jnp.sum(kg, axis=0, keepdims=True)

    m_ref[0] = jnp.broadcast_to(m_new, (SUBLANES, LANES))


def _mlstm(proj, gt3, gain, init, *, batch, seq, chunk):
    nc = seq // chunk
    lp = gt3.shape[2] // nc
    has_init = init is not None
    tok = lambda col: pl.BlockSpec((chunk, D_GROUP), lambda b, c, col=col: (b * nc + c, col))
    state_specs = [
        pl.BlockSpec((1, N_HEADS, HEAD_DIM, HEAD_DIM), lambda b, c: (b, 0, 0, 0)),
        pl.BlockSpec((1, N_HEADS, HEAD_DIM), lambda b, c: (b, 0, 0)),
        pl.BlockSpec((1, SUBLANES, LANES), lambda b, c: (b, 0, 0)),
    ]
    in_specs = [tok(0), tok(1), tok(2), tok(3),
                pl.BlockSpec((1, SUBLANES, lp), lambda b, c: (b, 0, c)),
                pl.BlockSpec((1, D_GROUP), lambda b, c: (0, 0))]
    args = [proj, proj, proj, proj, gt3, gain]
    if has_init:
        in_specs += state_specs
        args += list(init)
    return pl.pallas_call(
        functools.partial(_mlstm_kernel, chunk=chunk, has_init=has_init),
        grid=(batch, nc),
        in_specs=in_specs,
        out_specs=[pl.BlockSpec((chunk, D_GROUP), lambda b, c: (b * nc + c, 0))] + state_specs,
        out_shape=[
            jax.ShapeDtypeStruct((batch * seq, D_GROUP), BF16),
            jax.ShapeDtypeStruct((batch, N_HEADS, HEAD_DIM, HEAD_DIM), F32),
            jax.ShapeDtypeStruct((batch, N_HEADS, HEAD_DIM), F32),
            jax.ShapeDtypeStruct((batch, SUBLANES, LANES), F32),
        ],
        compiler_params=pltpu.CompilerParams(
            dimension_semantics=("arbitrary", "arbitrary"), vmem_limit_bytes=VMEM_LIMIT),
        name="mlstm",
    )(*args)


def _ret_kernel(*refs, chunk, has_init):
    if has_init:
        (lg_ref, q_ref, k_ref, v_ref, g_ref, cos_ref, sin_ref, gain_ref, r0_ref,
         mix_ref, r_ref) = refs
    else:
        lg_ref, q_ref, k_ref, v_ref, g_ref, cos_ref, sin_ref, gain_ref, mix_ref, r_ref = refs
    L = chunk
    half = HEAD_DIM // 2

    @pl.when(pl.program_id(1) == 0)
    def _():
        if has_init:
            r_ref[...] = r0_ref[...]
        else:
            r_ref[...] = jnp.zeros_like(r_ref)

    cos = cos_ref[...]
    sin = sin_ref[...]
    cos_k = cos * K_SCALE
    sin_k = sin * K_SCALE
    t_idx = lax.broadcasted_iota(jnp.int32, (L, L), 0)
    s_idx = lax.broadcasted_iota(jnp.int32, (L, L), 1)
    diff = t_idx - s_idx
    diff_pos = jnp.maximum(diff, 0).astype(F32)
    t_col = lax.broadcasted_iota(jnp.int32, (L, 1), 0).astype(F32)

    for h in range(N_HEADS):
        lo = slice(h * HEAD_DIM, h * HEAD_DIM + half)
        hi = slice(h * HEAD_DIM + half, (h + 1) * HEAD_DIM)
        sl = slice(h * HEAD_DIM, (h + 1) * HEAD_DIM)
        lg = lg_ref[h]
        q1 = q_ref[:, lo].astype(F32)
        q2 = q_ref[:, hi].astype(F32)
        k1 = k_ref[:, lo].astype(F32)
        k2 = k_ref[:, hi].astype(F32)
        qr = jnp.concatenate([q1 * cos - q2 * sin, q2 * cos + q1 * sin], axis=-1).astype(BF16)
        kr = jnp.concatenate([k1 * cos_k - k2 * sin_k, k2 * cos_k + k1 * sin_k], axis=-1)
        vh = v_ref[:, sl]

        dmask = jnp.where(diff >= 0, jnp.exp(lg * diff_pos), 0.0)
        inter = jnp.exp(lg * (t_col + 1.0))
        kdec = jnp.exp(lg * (L - 1.0 - t_col))
        cdec = jnp.exp(lg * jnp.full((1, 1), L, F32))

        s = lax.dot_general(qr, kr.astype(BF16), _NT, preferred_element_type=F32)
        r_h = r_ref[0, h]
        o = _dot((s * dmask).astype(BF16), vh) + inter * _dot(qr, r_h.astype(BF16))
        r_ref[0, h] = cdec * r_h + lax.dot_general(
            (kr * kdec).astype(BF16), vh, _TN, preferred_element_type=F32)

        oc = o - jnp.mean(o, axis=-1, keepdims=True)
        y = oc * _rms_scale(oc) * gain_ref[:, sl] * jax.nn.silu(g_ref[:, sl].astype(F32))
        mix_ref[:, sl] = y.astype(BF16)


def _retention(proj, log_gamma, cos, sin, gain, init, *, batch, seq, chunk):
    nc = seq // chunk
    has_init = init is not None
    tok = lambda col: pl.BlockSpec((chunk, D_GROUP), lambda b, c, col=col: (b * nc + c, col))
    state_spec = pl.BlockSpec((1, N_HEADS, HEAD_DIM, HEAD_DIM), lambda b, c: (b, 0, 0, 0))
    rope_spec = pl.BlockSpec((chunk, HEAD_DIM // 2), lambda b, c: (c, 0))
    in_specs = [pl.BlockSpec(memory_space=pltpu.SMEM),
                tok(4), tok(5), tok(6), tok(7), rope_spec, rope_spec,
                pl.BlockSpec((1, D_GROUP), lambda b, c: (0, 0))]
    args = [log_gamma, proj, proj, proj, proj, cos, sin, gain]
    if has_init:
        in_specs.append(state_spec)
        args.append(init)
    return pl.pallas_call(
        functools.partial(_ret_kernel, chunk=chunk, has_init=has_init),
        grid=(batch, nc),
        in_specs=in_specs,
        out_specs=[pl.BlockSpec((chunk, D_GROUP), lambda b, c: (b * nc + c, 0)), state_spec],
        out_shape=[
            jax.ShapeDtypeStruct((batch * seq, D_GROUP), BF16),
            jax.ShapeDtypeStruct((batch, N_HEADS, HEAD_DIM, HEAD_DIM), F32),
        ],
        compiler_params=pltpu.CompilerParams(
            dimension_semantics=("arbitrary", "arbitrary"), vmem_limit_bytes=VMEM_LIMIT),
        name="retention",
    )(*args)


def _outproj_kernel(x_ref, mm_ref, mr_ref, wm_ref, wr_ref, g2_ref, x1_ref, hn_ref):
    x1 = x_ref[...] + _dot(mm_ref[...], wm_ref[...]) + _dot(mr_ref[...], wr_ref[...])
    x1_ref[...] = x1
    hn_ref[...] = (x1 * _rms_scale(x1) * g2_ref[...]).astype(BF16)


def _outproj(x, mix_m, mix_r, w_out, g2, *, tm):
    n = x.shape[0]
    return pl.pallas_call(
        _outproj_kernel,
        grid=(n // tm,),
        in_specs=[
            pl.BlockSpec((tm, D_MODEL), lambda i: (i, 0)),
            pl.BlockSpec((tm, D_GROUP), lambda i: (i, 0)),
            pl.BlockSpec((tm, D_GROUP), lambda i: (i, 0)),
            pl.BlockSpec((D_GROUP, D_MODEL), lambda i: (0, 0)),
            pl.BlockSpec((D_GROUP, D_MODEL), lambda i: (1, 0)),
            pl.BlockSpec((1, D_MODEL), lambda i: (0, 0)),
        ],
        out_specs=[
            pl.BlockSpec((tm, D_MODEL), lambda i: (i, 0)),
            pl.BlockSpec((tm, D_MODEL), lambda i: (i, 0)),
        ],
        out_shape=[
            jax.ShapeDtypeStruct((n, D_MODEL), F32),
            jax.ShapeDtypeStruct((n, D_MODEL), BF16),
        ],
        compiler_params=pltpu.CompilerParams(
            dimension_semantics=("arbitrary",), vmem_limit_bytes=VMEM_LIMIT),
        name="outproj",
    )(x, mix_m, mix_r, w_out, w_out, g2)


def _ffn_kernel(hn_ref, x1_hbm, wg_ref, wu_ref, wd_ref, gf_ref, y_ref, x1_buf, x1_sem):
    i = pl.program_id(0)
    f = pl.program_id(1)
    tm = x1_buf.shape[0]

    def x1_copy(tile):
        return pltpu.make_async_copy(x1_hbm.at[pl.ds(tile * tm, tm), :], x1_buf, x1_sem)

    @pl.when((f == 0) & (i == 0))
    def _():
        x1_copy(0).start()

    @pl.when(f == 0)
    def _():
        x1_copy(i).wait()
        y_ref[...] = x1_buf[...]

    @pl.when((f == 1) & (i + 1 < pl.num_programs(0)))
    def _():
        x1_copy(i + 1).start()

    hn = hn_ref[...]
    gate = _dot(hn, wg_ref[...])
    up = _dot(hn, wu_ref[...])
    y_ref[...] += _dot((jax.nn.silu(gate) * up).astype(BF16), wd_ref[...])

    @pl.when(f == pl.num_programs(1) - 1)
    def _():
        x2 = y_ref[...]
        y_ref[...] = x2 * _rms_scale(x2) * gf_ref[...]


def _ffn(hn, x1, w_gate, w_up, w_down, g_final, *, tm, tf):
    n = hn.shape[0]
    return pl.pallas_call(
        _ffn_kernel,
        grid=(n // tm, D_FF // tf),
        in_specs=[
            pl.BlockSpec((tm, D_MODEL), lambda i, f: (i, 0)),
            pl.BlockSpec(memory_space=pl.ANY),
            pl.BlockSpec((D_MODEL, tf), lambda i, f: (0, f)),
            pl.BlockSpec((D_MODEL, tf), lambda i, f: (0, f)),
            pl.BlockSpec((tf, D_MODEL), lambda i, f: (f, 0)),
            pl.BlockSpec((1, D_MODEL), lambda i, f: (0, 0)),
        ],
        out_specs=pl.BlockSpec((tm, D_MODEL), lambda i, f: (i, 0)),
        out_shape=jax.ShapeDtypeStruct((n, D_MODEL), F32),
        scratch_shapes=[pltpu.VMEM((tm, D_MODEL), F32), pltpu.SemaphoreType.DMA(())],
        compiler_params=pltpu.CompilerParams(
            dimension_semantics=("arbitrary", "arbitrary"), vmem_limit_bytes=VMEM_LIMIT),
        name="ffn",
    )(hn, x1, w_gate, w_up, w_down, g_final)


def _rope_tables(pos):
    half = HEAD_DIM // 2
    freqs = ROPE_BASE ** (-jnp.arange(half, dtype=F32) / half)
    ang = pos[:, None] * freqs[None, :]
    return jnp.cos(ang), jnp.sin(ang)


def _trunk(x, pos, init, weights, *, chunk):
    batch, seq, _ = x.shape
    n = batch * seq
    (g1, w_main, w_gatecols, b_gate, gain_m, gain_r, w_out, g2, w_gate, w_up, w_down,
     g_final, log_gamma) = weights
    x2d = x.reshape(n, D_MODEL)

    proj, gt = _inproj(x2d, g1, w_main, w_gatecols, b_gate, tm=1024, tn=2048)
    gt3 = gt.reshape(N_GATES, batch, seq).transpose(1, 0, 2)
    if seq < LANES:
        gt3 = jnp.pad(gt3, ((0, 0), (0, 0), (0, LANES - seq)))

    if init is None:
        m_init = r_init = None
    else:
        c0, n0, m0, r0 = init
        m0 = jnp.pad(m0, ((0, 0), (0, SUBLANES - N_HEADS)))
        m0 = jnp.broadcast_to(m0[:, :, None], (batch, SUBLANES, LANES))
        m_init, r_init = (c0, n0, m0), r0

    mix_m, c_new, n_new, m_new = _mlstm(proj, gt3, gain_m, m_init,
                                        batch=batch, seq=seq, chunk=chunk)
    cos, sin = _rope_tables(pos)
    mix_r, r_new = _retention(proj, log_gamma, cos, sin, gain_r, r_init,
                              batch=batch, seq=seq, chunk=chunk)

    x1, hn = _outproj(x2d, mix_m, mix_r, w_out, g2, tm=512)
    y = _ffn(hn, x1, w_gate, w_up, w_down, g_final, tm=1024, tf=512)
    return (y.reshape(batch, seq, D_MODEL), c_new[None], n_new[None],
            m_new[:, :N_HEADS, 0][None], r_new[None])


def kernel(x_prompt, x_sample, state_mlstm_C, state_mlstm_n, state_mlstm_m, state_ret,
           g_norm1, w_in, b_gates, g_mlstm_norm, g_ret_norm, w_out, g_norm2,
           w_gate, w_up, w_down, g_final):
    w_in0 = w_in[0].astype(BF16)
    w_gatecols = jnp.pad(w_in0[:, D_PROJ:], ((0, 0), (0, LANES - N_GATES)))
    b_gate = jnp.pad(b_gates[0], (0, LANES - N_GATES)).reshape(1, LANES)
    log_gamma = jnp.log(1.0 - jnp.exp2(-5.0 - jnp.arange(N_HEADS, dtype=F32)))
    weights = (
        g_norm1[0].reshape(1, D_MODEL), w_in0, w_gatecols, b_gate,
        g_mlstm_norm[0].reshape(1, D_GROUP), g_ret_norm[0].reshape(1, D_GROUP),
        w_out[0].astype(BF16), g_norm2[0].reshape(1, D_MODEL),
        w_gate[0].astype(BF16), w_up[0].astype(BF16), w_down[0].astype(BF16),
        g_final.reshape(1, D_MODEL), log_gamma,
    )
    seq_p = x_prompt.shape[1]
    seq_s = x_sample.shape[1]
    pos_p = jnp.arange(seq_p, dtype=F32)
    pos_s = PAST_LEN + jnp.arange(seq_s, dtype=F32)

    y_p, c_p, n_p, m_p, r_p = _trunk(x_prompt, pos_p, None, weights, chunk=PROMPT_CHUNK)
    init_s = (state_mlstm_C[0], state_mlstm_n[0], state_mlstm_m[0], state_ret[0])
    y_s, c_s, n_s, m_s, r_s = _trunk(x_sample, pos_s, init_s, weights, chunk=seq_s)
    return (y_p, y_s, c_p, n_p, m_p, r_p, c_s, n_s, m_s, r_s)
```

```python
import functools

import jax
import jax.numpy as jnp
from jax import lax
from jax.experimental import pallas as pl
from jax.experimental.pallas import tpu as pltpu

F32 = jnp.float32
BF16 = jnp.bfloat16

D_MODEL = 2048
N_HEADS = 4
HEAD_DIM = 256
D_GROUP = N_HEADS * HEAD_DIM
D_PROJ = 8 * D_GROUP
N_GATES = 2 * N_HEADS
D_FF = 5632
ROPE_BASE = 10000.0
PAST_LEN = 4096
EPS = 1e-6
K_SCALE = HEAD_DIM ** -0.5

LANES = 128
SUBLANES = 8
VMEM_LIMIT = 56 * 1024 * 1024

PROMPT_CHUNK = 256
INPROJ_TM = 1024
INPROJ_TN = 2048

_NT = (((1,), (1,)), ((), ()))
_TN = (((0,), (0,)), ((), ()))


def _dot(a, b):
    return jnp.dot(a, b, preferred_element_type=F32)


def _rms_scale(x):
    return lax.rsqrt(jnp.mean(x * x, axis=-1, keepdims=True) + EPS)


def _params(n_axes):
    return pltpu.CompilerParams(
        dimension_semantics=("arbitrary",) * n_axes, vmem_limit_bytes=VMEM_LIMIT)


def _inproj_prologue(x_ref, g1_ref, wgate_ref, bgate_ref, gt_ref, xn_ref):
    x = x_ref[...]
    xn = (x * _rms_scale(x) * g1_ref[...]).astype(BF16)
    xn_ref[...] = xn
    gates = _dot(xn, wgate_ref[...]) + bgate_ref[...]
    gt_ref[...] = gates.T[0:N_GATES, :]


def _inproj_kernel(x_ref, g1_ref, w_ref, wgate_ref, bgate_ref, proj_ref, gt_ref, xn_ref):
    @pl.when(pl.program_id(1) == 0)
    def _():
        _inproj_prologue(x_ref, g1_ref, wgate_ref, bgate_ref, gt_ref, xn_ref)

    proj_ref[...] = _dot(xn_ref[...], w_ref[...]).astype(BF16)


def _inproj(x, g1, w_main, w_gate, b_gate, *, tm, tn):
    n = x.shape[0]
    return pl.pallas_call(
        _inproj_kernel,
        grid=(n // tm, D_PROJ // tn),
        in_specs=[
            pl.BlockSpec((tm, D_MODEL), lambda i, j: (i, 0)),
            pl.BlockSpec((1, D_MODEL), lambda i, j: (0, 0)),
            pl.BlockSpec((D_MODEL, tn), lambda i, j: (0, j)),
            pl.BlockSpec((D_MODEL, LANES), lambda i, j: (0, 0)),
            pl.BlockSpec((1, LANES), lambda i, j: (0, 0)),
        ],
        out_specs=[
            pl.BlockSpec((tm, tn), lambda i, j: (i, j)),
            pl.BlockSpec((N_GATES, tm), lambda i, j: (0, i)),
        ],
        out_shape=[
            jax.ShapeDtypeStruct((n, D_PROJ), BF16),
            jax.ShapeDtypeStruct((N_GATES, n), F32),
        ],
        scratch_shapes=[pltpu.VMEM((tm, D_MODEL), BF16)],
        compiler_params=_params(2),
        name="inproj",
    )(x, g1, w_main, w_gate, b_gate)


def _lane_scan(x, combine, identity):
    width = x.shape[1]
    lane = lax.broadcasted_iota(jnp.int32, x.shape, 1)
    shift = 1
    while shift < width:
        shifted = jnp.where(lane >= shift, pltpu.roll(x, shift, axis=1), identity)
        x = combine(x, shifted)
        shift *= 2
    return x


def _mlstm_phases(q_ref, k_ref, v_ref, o_ref, gates, gain_ref, mix_ref, c_ref, n_ref, m_ref, L):
    b = _lane_scan(jax.nn.log_sigmoid(gates), jnp.add, 0.0)
    b = pltpu.roll(b, N_HEADS, axis=0)
    u = gates - b
    m_prev = m_ref[0][:, 0:1]
    big_m = jnp.maximum(m_prev, _lane_scan(u, jnp.maximum, -jnp.inf))
    a = jnp.exp(m_prev - big_m)
    e = jnp.exp(-(b + big_m))
    m_last = big_m[:, L - 1:L]
    g = jnp.exp(u - m_last)
    dec = a[:, L - 1:L]
    m_new = b[:, L - 1:L] + m_last
    rows = jnp.concatenate(
        [big_m, a, e, g, jnp.zeros((LANES - 4 * SUBLANES, gates.shape[1]), F32)], axis=0)
    cols = rows.T

    t_idx = lax.broadcasted_iota(jnp.int32, (L, L), 0)
    s_idx = lax.broadcasted_iota(jnp.int32, (L, L), 1)
    causal = s_idx <= t_idx

    for h in range(N_HEADS):
        sl = slice(h * HEAD_DIM, (h + 1) * HEAD_DIM)
        qh = q_ref[:, sl]
        kh = k_ref[:, sl] * jnp.asarray(K_SCALE, BF16)
        vh = v_ref[:, sl]
        m_col = cols[:L, h:h + 1]
        a_col = cols[:L, SUBLANES + h:SUBLANES + h + 1]
        e_col = cols[:L, 2 * SUBLANES + h:2 * SUBLANES + h + 1]
        g_col = cols[:L, 3 * SUBLANES + h:3 * SUBLANES + h + 1]
        u_row = u[h:h + 1, :L]

        s = lax.dot_general(qh, kh, _NT, preferred_element_type=F32)
        yield
        sw = s * jnp.exp(jnp.where(causal, u_row - m_col, -jnp.inf))
        c_h = c_ref[0, h]
        n_h = n_ref[0, h:h + 1, :]
        num = a_col * _dot(qh, c_h.astype(BF16)) + _dot(sw.astype(BF16), vh)
        qn = jnp.sum(qh.astype(F32) * n_h, axis=-1, keepdims=True)
        den = a_col * qn + jnp.sum(sw, axis=-1, keepdims=True)
        hh = num * (1.0 / jnp.maximum(jnp.abs(den), e_col))
        y = hh * _rms_scale(hh) * gain_ref[:, sl] * jax.nn.sigmoid(o_ref[:, sl].astype(F32))
        mix_ref[:, sl] = y.astype(BF16)

        kg = kh.astype(F32) * g_col
        dec_h = dec[h:h + 1, :]
        c_ref[0, h] = dec_h * c_h + lax.dot_general(
            kg.astype(BF16), vh, _TN, preferred_element_type=F32)
        n_ref[0, h:h + 1, :] = dec_h * n_h + jnp.sum(kg, axis=0, keepdims=True)
        if h == N_HEADS - 1:
            m_ref[0] = jnp.broadcast_to(m_new, (SUBLANES, LANES))
        yield


def _run(phases):
    for _ in phases:
        pass


def _mlstm_state_specs(index_map3, index_map4):
    return [
        pl.BlockSpec((1, N_HEADS, HEAD_DIM, HEAD_DIM), index_map4),
        pl.BlockSpec((1, N_HEADS, HEAD_DIM), index_map3),
        pl.BlockSpec((1, SUBLANES, LANES), index_map3),
    ]


def _mlstm_state_shapes(batch):
    return [
        jax.ShapeDtypeStruct((batch, N_HEADS, HEAD_DIM, HEAD_DIM), F32),
        jax.ShapeDtypeStruct((batch, N_HEADS, HEAD_DIM), F32),
        jax.ShapeDtypeStruct((batch, SUBLANES, LANES), F32),
    ]


def _mlstm_kernel(q_ref, k_ref, v_ref, o_ref, gt_ref, gain_ref, c0_ref, n0_ref, m0_ref,
                  mix_ref, c_ref, n_ref, m_ref, *, chunk):
    c_ref[...] = c0_ref[...]
    n_ref[...] = n0_ref[...]
    m_ref[...] = m0_ref[...]
    _run(_mlstm_phases(q_ref, k_ref, v_ref, o_ref, gt_ref[0], gain_ref, mix_ref, c_ref, n_ref,
                       m_ref, chunk))


def _mlstm_single_chunk(proj, gt3, gain, init, *, batch, seq):
    tok = lambda col: pl.BlockSpec((seq, D_GROUP), lambda b, col=col: (b, col))
    state_specs = _mlstm_state_specs(lambda b: (b, 0, 0), lambda b: (b, 0, 0, 0))
    return pl.pallas_call(
        functools.partial(_mlstm_kernel, chunk=seq),
        grid=(batch,),
        in_specs=[tok(0), tok(1), tok(2), tok(3),
                  pl.BlockSpec((1, SUBLANES, gt3.shape[2]), lambda b: (b, 0, 0)),
                  pl.BlockSpec((1, D_GROUP), lambda b: (0, 0))] + state_specs,
        out_specs=[pl.BlockSpec((seq, D_GROUP), lambda b: (b, 0))] + state_specs,
        out_shape=[jax.ShapeDtypeStruct((batch * seq, D_GROUP), BF16)] + _mlstm_state_shapes(batch),
        compiler_params=_params(1),
        name="mlstm",
    )(proj, proj, proj, proj, gt3, gain, *init)


def _ret_phases(lg_ref, q_ref, k_ref, v_ref, g_ref, cos_ref, sin_ref, gain_ref, mix_ref, r_ref, L):
    half = HEAD_DIM // 2
    cos = cos_ref[...]
    sin = sin_ref[...]
    cos_k = cos * K_SCALE
    sin_k = sin * K_SCALE
    t_idx = lax.broadcasted_iota(jnp.int32, (L, L), 0)
    s_idx = lax.broadcasted_iota(jnp.int32, (L, L), 1)
    diff = t_idx - s_idx
    diff_pos = jnp.maximum(diff, 0).astype(F32)
    t_col = lax.broadcasted_iota(jnp.int32, (L, 1), 0).astype(F32)

    for h in range(N_HEADS):
        lo = slice(h * HEAD_DIM, h * HEAD_DIM + half)
        hi = slice(h * HEAD_DIM + half, (h + 1) * HEAD_DIM)
        sl = slice(h * HEAD_DIM, (h + 1) * HEAD_DIM)
        lg = lg_ref[h]
        q1 = q_ref[:, lo].astype(F32)
        q2 = q_ref[:, hi].astype(F32)
        k1 = k_ref[:, lo].astype(F32)
        k2 = k_ref[:, hi].astype(F32)
        qr = jnp.concatenate([q1 * cos - q2 * sin, q2 * cos + q1 * sin], axis=-1).astype(BF16)
        kr = jnp.concatenate([k1 * cos_k - k2 * sin_k, k2 * cos_k + k1 * sin_k], axis=-1)
        vh = v_ref[:, sl]

        dmask = jnp.where(diff >= 0, jnp.exp(lg * diff_pos), 0.0)
        inter = jnp.exp(lg * (t_col + 1.0))
        kdec = jnp.exp(lg * (L - 1.0 - t_col))
        cdec = jnp.exp(lg * jnp.full((1, 1), L, F32))

        s = lax.dot_general(qr, kr.astype(BF16), _NT, preferred_element_type=F32)
        yield
        r_h = r_ref[0, h]
        o = _dot((s * dmask).astype(BF16), vh) + inter * _dot(qr, r_h.astype(BF16))
        r_ref[0, h] = cdec * r_h + lax.dot_general(
            (kr * kdec).astype(BF16), vh, _TN, preferred_element_type=F32)

        oc = o - jnp.mean(o, axis=-1, keepdims=True)
        y = oc * _rms_scale(oc) * gain_ref[:, sl] * jax.nn.silu(g_ref[:, sl].astype(F32))
        mix_ref[:, sl] = y.astype(BF16)
        yield


def _ret_kernel(lg_ref, q_ref, k_ref, v_ref, g_ref, cos_ref, sin_ref, gain_ref, r0_ref,
                mix_ref, r_ref, *, chunk):
    r_ref[...] = r0_ref[...]
    _run(_ret_phases(lg_ref, q_ref, k_ref, v_ref, g_ref, cos_ref, sin_ref, gain_ref, mix_ref,
                     r_ref, chunk))


def _retention_single_chunk(proj, log_gamma, cos, sin, gain, init, *, batch, seq):
    tok = lambda col: pl.BlockSpec((seq, D_GROUP), lambda b, col=col: (b, col))
    state_spec = pl.BlockSpec((1, N_HEADS, HEAD_DIM, HEAD_DIM), lambda b: (b, 0, 0, 0))
    rope_spec = pl.BlockSpec((seq, HEAD_DIM // 2), lambda b: (0, 0))
    return pl.pallas_call(
        functools.partial(_ret_kernel, chunk=seq),
        grid=(batch,),
        in_specs=[pl.BlockSpec(memory_space=pltpu.SMEM),
                  tok(4), tok(5), tok(6), tok(7), rope_spec, rope_spec,
                  pl.BlockSpec((1, D_GROUP), lambda b: (0, 0)), state_spec],
        out_specs=[pl.BlockSpec((seq, D_GROUP), lambda b: (b, 0)), state_spec],
        out_shape=[
            jax.ShapeDtypeStruct((batch * seq, D_GROUP), BF16),
            jax.ShapeDtypeStruct((batch, N_HEADS, HEAD_DIM, HEAD_DIM), F32),
        ],
        compiler_params=_params(1),
        name="retention",
    )(log_gamma, proj, proj, proj, proj, cos, sin, gain, init)


def _stage_kernel(*refs, do_inproj, do_mix, n_col_tiles, chunk):
    refs = list(refs)
    take = lambda n: [refs.pop(0) for _ in range(n)]
    if do_inproj:
        x_ref, g1_ref, w_ref, wgate_ref, bgate_ref = take(5)
    if do_mix:
        (lg_ref, mq_ref, mk_ref, mv_ref, mo_ref, gtin_ref, gain_m_ref,
         rq_ref, rk_ref, rv_ref, rg_ref, cos_ref, sin_ref, gain_r_ref) = take(14)
    if do_inproj:
        proj_ref, gt_ref = take(2)
    if do_mix:
        mixm_ref, c_ref, n_ref, m_ref, mixr_ref, r_ref = take(6)
    if do_inproj:
        (xn_ref,) = take(1)
    step = pl.program_id(0)

    if do_mix:
        @pl.when(step == 0)
        def _():
            c_ref[...] = jnp.zeros_like(c_ref)
            n_ref[...] = jnp.zeros_like(n_ref)
            m_ref[...] = jnp.zeros_like(m_ref)
            r_ref[...] = jnp.zeros_like(r_ref)

    if do_inproj:
        @pl.when(step % n_col_tiles == 0)
        def _():
            _inproj_prologue(x_ref, g1_ref, wgate_ref, bgate_ref, gt_ref, xn_ref)

    piece = w_ref.shape[1] // N_HEADS if do_inproj else 0
    mixers = []
    if do_mix:
        mixers = [
            _mlstm_phases(mq_ref, mk_ref, mv_ref, mo_ref, gtin_ref[...], gain_m_ref,
                          mixm_ref, c_ref, n_ref, m_ref, chunk),
            _ret_phases(lg_ref, rq_ref, rk_ref, rv_ref, rg_ref, cos_ref, sin_ref, gain_r_ref,
                        mixr_ref, r_ref, chunk),
        ]
    for h in range(N_HEADS):
        for phases in mixers:
            next(phases)
        if do_inproj:
            cols = slice(h * piece, (h + 1) * piece)
            proj_ref[:, cols] = _dot(xn_ref[...], w_ref[:, cols]).astype(BF16)
        for phases in mixers:
            next(phases)


def _prompt_stage(x2d, b_in, mix_in, w, rope, *, seq):
    do_inproj = b_in is not None
    do_mix = mix_in is not None
    tm, tn = INPROJ_TM, INPROJ_TN
    n_col_tiles = D_PROJ // tn
    steps = (seq // tm) * n_col_tiles
    chunk = seq // steps
    const = lambda shape: pl.BlockSpec(shape, lambda s: (0,) * len(shape))
    in_specs, args, out_specs, out_shape, scratch = [], [], [], [], []
    if do_inproj:
        row0 = b_in * (seq // tm)
        in_specs += [
            pl.BlockSpec((tm, D_MODEL), lambda s: (row0 + s // n_col_tiles, 0),
                         pipeline_mode=pl.Buffered(1)),
            const((1, D_MODEL)),
            pl.BlockSpec((D_MODEL, tn), lambda s: (0, s % n_col_tiles)),
            const((D_MODEL, LANES)),
            const((1, LANES)),
        ]
        args += [x2d, w["g1"], w["w_in"], w["w_gatecols"], w["b_gate"]]
    if do_mix:
        proj_prev, gt_prev = mix_in
        cos, sin = rope
        tok = lambda col: pl.BlockSpec((chunk, D_GROUP), lambda s, col=col: (s, col))
        rope_spec = pl.BlockSpec((chunk, HEAD_DIM // 2), lambda s: (s, 0))
        in_specs += [pl.BlockSpec(memory_space=pltpu.SMEM),
                     tok(0), tok(1), tok(2), tok(3),
                     pl.BlockSpec((N_GATES, chunk), lambda s: (0, s)),
                     const((1, D_GROUP)),
                     tok(4), tok(5), tok(6), tok(7), rope_spec, rope_spec,
                     const((1, D_GROUP))]
        args += [w["log_gamma"], proj_prev, proj_prev, proj_prev, proj_prev, gt_prev,
                 w["gain_m"], proj_prev, proj_prev, proj_prev, proj_prev, cos, sin, w["gain_r"]]
    if do_inproj:
        out_specs += [
            pl.BlockSpec((tm, tn), lambda s: (s // n_col_tiles, s % n_col_tiles)),
            pl.BlockSpec((N_GATES, tm), lambda s: (0, s // n_col_tiles)),
        ]
        out_shape += [jax.ShapeDtypeStruct((seq, D_PROJ), BF16),
                      jax.ShapeDtypeStruct((N_GATES, seq), F32)]
        scratch += [pltpu.VMEM((tm, D_MODEL), BF16)]
    if do_mix:
        mix_spec = pl.BlockSpec((chunk, D_GROUP), lambda s: (s, 0))
        mix_shape = jax.ShapeDtypeStruct((seq, D_GROUP), BF16)
        state4 = pl.BlockSpec((1, N_HEADS, HEAD_DIM, HEAD_DIM), lambda s: (0, 0, 0, 0))
        out_specs += ([mix_spec] + _mlstm_state_specs(lambda s: (0, 0, 0), lambda s: (0, 0, 0, 0))
                      + [mix_spec, state4])
        out_shape += ([mix_shape] + _mlstm_state_shapes(1)
                      + [mix_shape, jax.ShapeDtypeStruct((1, N_HEADS, HEAD_DIM, HEAD_DIM), F32)])
    return pl.pallas_call(
        functools.partial(_stage_kernel, do_inproj=do_inproj, do_mix=do_mix,
                          n_col_tiles=n_col_tiles, chunk=chunk),
        grid=(steps,),
        in_specs=in_specs,
        out_specs=out_specs,
        out_shape=out_shape,
        scratch_shapes=scratch,
        compiler_params=_params(1),
        name="stage_" + ("i" if do_inproj else "") + ("m" if do_mix else ""),
    )(*args)


def _outproj_kernel(*refs, n_src, tiles_per_src):
    x_ref = refs[0]
    mm_refs = refs[1:1 + n_src]
    mr_refs = refs[1 + n_src:1 + 2 * n_src]
    wm_ref, wr_ref, g2_ref, x1_ref, hn_ref = refs[1 + 2 * n_src:]
    src = pl.program_id(0) // tiles_per_src
    mix_m = mm_refs[0][...]
    mix_r = mr_refs[0][...]
    for k in range(1, n_src):
        mix_m = jnp.where(src == k, mm_refs[k][...], mix_m)
        mix_r = jnp.where(src == k, mr_refs[k][...], mix_r)
    x1 = x_ref[...] + _dot(mix_m, wm_ref[...]) + _dot(mix_r, wr_ref[...])
    x1_ref[...] = x1
    hn_ref[...] = (x1 * _rms_scale(x1) * g2_ref[...]).astype(BF16)


def _outproj(x, mix_m_list, mix_r_list, w_out, g2, *, tm):
    n = x.shape[0]
    n_src = len(mix_m_list)
    tiles_per_src = mix_m_list[0].shape[0] // tm
    mix_spec = lambda k: pl.BlockSpec(
        (tm, D_GROUP), lambda i, k=k: (jnp.clip(i - k * tiles_per_src, 0, tiles_per_src - 1), 0))
    w_spec = lambda half: pl.BlockSpec((D_GROUP, D_MODEL), lambda i, half=half: (half, 0),
                                       pipeline_mode=pl.Buffered(1))
    return pl.pallas_call(
        functools.partial(_outproj_kernel, n_src=n_src, tiles_per_src=tiles_per_src),
        grid=(n // tm,),
        in_specs=([pl.BlockSpec((tm, D_MODEL), lambda i: (i, 0))]
                  + [mix_spec(k) for k in range(n_src)] * 2
                  + [w_spec(0), w_spec(1), pl.BlockSpec((1, D_MODEL), lambda i: (0, 0))]),
        out_specs=[
            pl.BlockSpec((tm, D_MODEL), lambda i: (i, 0)),
            pl.BlockSpec((tm, D_MODEL), lambda i: (i, 0)),
        ],
        out_shape=[
            jax.ShapeDtypeStruct((n, D_MODEL), F32),
            jax.ShapeDtypeStruct((n, D_MODEL), BF16),
        ],
        compiler_params=_params(1),
        name="outproj",
    )(x, *mix_m_list, *mix_r_list, w_out, w_out, g2)


def _ffn_kernel(hn_ref, x1_hbm, wg_ref, wu_ref, wd_ref, gf_ref, y_ref, x1_buf, x1_sem):
    i = pl.program_id(0)
    f = pl.program_id(1)
    tm = x1_buf.shape[0]

    def x1_copy(tile):
        return pltpu.make_async_copy(x1_hbm.at[pl.ds(tile * tm, tm), :], x1_buf, x1_sem)

    @pl.when((f == 0) & (i == 0))
    def _():
        x1_copy(0).start()

    @pl.when(f == 0)
    def _():
        x1_copy(i).wait()
        y_ref[...] = x1_buf[...]

    @pl.when((f == 1) & (i + 1 < pl.num_programs(0)))
    def _():
        x1_copy(i + 1).start()

    hn = hn_ref[...]
    gate = _dot(hn, wg_ref[...])
    up = _dot(hn, wu_ref[...])
    y_ref[...] += _dot((jax.nn.silu(gate) * up).astype(BF16), wd_ref[...])

    @pl.when(f == pl.num_programs(1) - 1)
    def _():
        x2 = y_ref[...]
        y_ref[...] = x2 * _rms_scale(x2) * gf_ref[...]


def _ffn(hn, x1, w_gate, w_up, w_down, g_final, *, tm, tf):
    n = hn.shape[0]
    return pl.pallas_call(
        _ffn_kernel,
        grid=(n // tm, D_FF // tf),
        in_specs=[
            pl.BlockSpec((tm, D_MODEL), lambda i, f: (i, 0)),
            pl.BlockSpec(memory_space=pl.ANY),
            pl.BlockSpec((D_MODEL, tf), lambda i, f: (0, f)),
            pl.BlockSpec((D_MODEL, tf), lambda i, f: (0, f)),
            pl.BlockSpec((tf, D_MODEL), lambda i, f: (f, 0)),
            pl.BlockSpec((1, D_MODEL), lambda i, f: (0, 0)),
        ],
        out_specs=pl.BlockSpec((tm, D_MODEL), lambda i, f: (i, 0)),
        out_shape=jax.ShapeDtypeStruct((n, D_MODEL), F32),
        scratch_shapes=[pltpu.VMEM((tm, D_MODEL), F32), pltpu.SemaphoreType.DMA(())],
        compiler_params=_params(2),
        name="ffn",
    )(hn, x1, w_gate, w_up, w_down, g_final)


def _rope_tables(pos):
    half = HEAD_DIM // 2
    freqs = ROPE_BASE ** (-jnp.arange(half, dtype=F32) / half)
    ang = pos[:, None] * freqs[None, :]
    return jnp.cos(ang), jnp.sin(ang)


def _tail(x2d, mix_m_list, mix_r_list, w):
    x1, hn = _outproj(x2d, mix_m_list, mix_r_list, w["w_out"], w["g2"], tm=512)
    return _ffn(hn, x1, w["w_gate"], w["w_up"], w["w_down"], w["g_final"], tm=1024, tf=512)


def _prompt_trunk(x, w):
    batch, seq, _ = x.shape
    x2d = x.reshape(batch * seq, D_MODEL)
    rope = _rope_tables(jnp.arange(seq, dtype=F32))
    mixed = []
    mix_in = None
    for b_in in list(range(batch)) + [None]:
        outs = _prompt_stage(x2d, b_in, mix_in, w, rope, seq=seq)
        if mix_in is not None:
            mixed.append(outs[-6:])
        mix_in = tuple(outs[:2]) if b_in is not None else None
    mix_m, c_new, n_new, m_new, mix_r, r_new = [list(t) for t in zip(*mixed)]
    y = _tail(x2d, mix_m, mix_r, w)
    cat = lambda parts: jnp.concatenate(parts, axis=0)[None]
    return (y.reshape(batch, seq, D_MODEL), cat(c_new), cat(n_new),
            cat(m_new)[:, :, :N_HEADS, 0], cat(r_new))


def _sample_trunk(x, init, w):
    batch, seq, _ = x.shape
    x2d = x.reshape(batch * seq, D_MODEL)
    proj, gt = _inproj(x2d, w["g1"], w["w_in"], w["w_gatecols"], w["b_gate"],
                       tm=INPROJ_TM, tn=INPROJ_TN)
    gt3 = gt.reshape(N_GATES, batch, seq).transpose(1, 0, 2)
    gt3 = jnp.pad(gt3, ((0, 0), (0, 0), (0, max(0, LANES - seq))))
    c0, n0, m0, r0 = init
    m0 = jnp.pad(m0, ((0, 0), (0, SUBLANES - N_HEADS)))
    m0 = jnp.broadcast_to(m0[:, :, None], (batch, SUBLANES, LANES))
    mix_m, c_new, n_new, m_new = _mlstm_single_chunk(proj, gt3, w["gain_m"], (c0, n0, m0),
                                                     batch=batch, seq=seq)
    cos, sin = _rope_tables(PAST_LEN + jnp.arange(seq, dtype=F32))
    mix_r, r_new = _retention_single_chunk(proj, w["log_gamma"], cos, sin, w["gain_r"], r0,
                                           batch=batch, seq=seq)
    y = _tail(x2d, [mix_m], [mix_r], w)
    return (y.reshape(batch, seq, D_MODEL), c_new[None], n_new[None],
            m_new[:, :N_HEADS, 0][None], r_new[None])


def kernel(x_prompt, x_sample, state_mlstm_C, state_mlstm_n, state_mlstm_m, state_ret,
           g_norm1, w_in, b_gates, g_mlstm_norm, g_ret_norm, w_out, g_norm2,
           w_gate, w_up, w_down, g_final):
    w_in0 = w_in[0].astype(BF16)
    w = dict(
        g1=g_norm1[0].reshape(1, D_MODEL),
        w_in=w_in0,
        w_gatecols=jnp.pad(w_in0[:, D_PROJ:], ((0, 0), (0, LANES - N_GATES))),
        b_gate=jnp.pad(b_gates[0], (0, LANES - N_GATES)).reshape(1, LANES),
        gain_m=g_mlstm_norm[0].reshape(1, D_GROUP),
        gain_r=g_ret_norm[0].reshape(1, D_GROUP),
        w_out=w_out[0].astype(BF16),
        g2=g_norm2[0].reshape(1, D_MODEL),
        w_gate=w_gate[0].astype(BF16),
        w_up=w_up[0].astype(BF16),
        w_down=w_down[0].astype(BF16),
        g_final=g_final.reshape(1, D_MODEL),
        log_gamma=jnp.log(1.0 - jnp.exp2(-5.0 - jnp.arange(N_HEADS, dtype=F32))),
    )
    y_p, c_p, n_p, m_p, r_p = _prompt_trunk(x_prompt, w)
    init_s = (state_mlstm_C[0], state_mlstm_n[0], state_mlstm_m[0], state_ret[0])
    y_s, c_s, n_s, m_s, r_s = _sample_trunk(x_sample, init_s, w)
    return (y_p, y_s, c_p, n_p, m_p, r_p, c_s, n_s, m_s, r_s)
```

```python
import functools

import jax
import jax.numpy as jnp
from jax import lax
from jax.experimental import pallas as pl
from jax.experimental.pallas import tpu as pltpu

F32 = jnp.float32
BF16 = jnp.bfloat16

D_MODEL = 2048
N_HEADS = 4
HEAD_DIM = 256
D_GROUP = N_HEADS * HEAD_DIM
D_PROJ = 8 * D_GROUP
N_GATES = 2 * N_HEADS
D_FF = 5632
ROPE_BASE = 10000.0
PAST_LEN = 4096
EPS = 1e-6
K_SCALE = HEAD_DIM ** -0.5

LANES = 128
SUBLANES = 8
VMEM_LIMIT = 60 * 1024 * 1024

PROMPT_CHUNK = 256
INPROJ_TM = 1024
INPROJ_TN = 2048

_NT = (((1,), (1,)), ((), ()))
_TN = (((0,), (0,)), ((), ()))


def _dot(a, b):
    return jnp.dot(a, b, preferred_element_type=F32)


def _rms_scale(x):
    return lax.rsqrt(jnp.mean(x * x, axis=-1, keepdims=True) + EPS)


def _params(n_axes):
    return pltpu.CompilerParams(
        dimension_semantics=("arbitrary",) * n_axes, vmem_limit_bytes=VMEM_LIMIT)


def _inproj_prologue(x_ref, g1_ref, wgate_ref, bgate_ref, gt_ref, xn_ref):
    x = x_ref[...]
    xn = (x * _rms_scale(x) * g1_ref[...]).astype(BF16)
    xn_ref[...] = xn
    gates = _dot(xn, wgate_ref[...]) + bgate_ref[...]
    gt_ref[...] = gates.T[0:N_GATES, :]


def _inproj_kernel(x_ref, g1_ref, w_ref, wgate_ref, bgate_ref, proj_ref, gt_ref, xn_ref):
    @pl.when(pl.program_id(1) == 0)
    def _():
        _inproj_prologue(x_ref, g1_ref, wgate_ref, bgate_ref, gt_ref, xn_ref)

    proj_ref[...] = _dot(xn_ref[...], w_ref[...]).astype(BF16)


def _inproj(x, g1, w_main, w_gate, b_gate, *, tm, tn):
    n = x.shape[0]
    return pl.pallas_call(
        _inproj_kernel,
        grid=(n // tm, D_PROJ // tn),
        in_specs=[
            pl.BlockSpec((tm, D_MODEL), lambda i, j: (i, 0)),
            pl.BlockSpec((1, D_MODEL), lambda i, j: (0, 0)),
            pl.BlockSpec((D_MODEL, tn), lambda i, j: (0, j)),
            pl.BlockSpec((D_MODEL, LANES), lambda i, j: (0, 0)),
            pl.BlockSpec((1, LANES), lambda i, j: (0, 0)),
        ],
        out_specs=[
            pl.BlockSpec((tm, tn), lambda i, j: (i, j)),
            pl.BlockSpec((N_GATES, tm), lambda i, j: (0, i)),
        ],
        out_shape=[
            jax.ShapeDtypeStruct((n, D_PROJ), BF16),
            jax.ShapeDtypeStruct((N_GATES, n), F32),
        ],
        scratch_shapes=[pltpu.VMEM((tm, D_MODEL), BF16)],
        compiler_params=_params(2),
        name="inproj",
    )(x, g1, w_main, w_gate, b_gate)


def _lane_scan(x, combine, identity):
    width = x.shape[1]
    lane = lax.broadcasted_iota(jnp.int32, x.shape, 1)
    shift = 1
    while shift < width:
        shifted = jnp.where(lane >= shift, pltpu.roll(x, shift, axis=1), identity)
        x = combine(x, shifted)
        shift *= 2
    return x


def _mlstm_phases(q_ref, k_ref, v_ref, o_ref, gates, gain_ref, mix_ref, c_ref, n_ref, m_ref, L):
    b = _lane_scan(jax.nn.log_sigmoid(gates), jnp.add, 0.0)
    b = pltpu.roll(b, N_HEADS, axis=0)
    u = gates - b
    m_prev = m_ref[0][:, 0:1]
    big_m = jnp.maximum(m_prev, _lane_scan(u, jnp.maximum, -jnp.inf))
    a = jnp.exp(m_prev - big_m)
    e = jnp.exp(-(b + big_m))
    m_last = big_m[:, L - 1:L]
    g = jnp.exp(u - m_last)
    dec = a[:, L - 1:L]
    m_new = b[:, L - 1:L] + m_last
    rows = jnp.concatenate(
        [big_m, a, e, g, jnp.zeros((LANES - 4 * SUBLANES, gates.shape[1]), F32)], axis=0)
    cols = rows.T

    t_idx = lax.broadcasted_iota(jnp.int32, (L, L), 0)
    s_idx = lax.broadcasted_iota(jnp.int32, (L, L), 1)
    causal = s_idx <= t_idx

    for h in range(N_HEADS):
        sl = slice(h * HEAD_DIM, (h + 1) * HEAD_DIM)
        qh = q_ref[:, sl]
        kh = k_ref[:, sl] * jnp.asarray(K_SCALE, BF16)
        vh = v_ref[:, sl]
        m_col = cols[:L, h:h + 1]
        a_col = cols[:L, SUBLANES + h:SUBLANES + h + 1]
        e_col = cols[:L, 2 * SUBLANES + h:2 * SUBLANES + h + 1]
        g_col = cols[:L, 3 * SUBLANES + h:3 * SUBLANES + h + 1]
        u_row = u[h:h + 1, :L]

        s = lax.dot_general(qh, kh, _NT, preferred_element_type=F32)
        yield
        sw = s * jnp.exp(jnp.where(causal, u_row - m_col, -jnp.inf))
        c_h = c_ref[0, h]
        n_h = n_ref[0, h:h + 1, :]
        num = a_col * _dot(qh, c_h.astype(BF16)) + _dot(sw.astype(BF16), vh)
        qn = jnp.sum(qh.astype(F32) * n_h, axis=-1, keepdims=True)
        den = a_col * qn + jnp.sum(sw, axis=-1, keepdims=True)
        hh = num * (1.0 / jnp.maximum(jnp.abs(den), e_col))
        y = hh * _rms_scale(hh) * gain_ref[:, sl] * jax.nn.sigmoid(o_ref[:, sl].astype(F32))
        mix_ref[:, sl] = y.astype(BF16)

        kg = kh.astype(F32) * g_col
        dec_h = dec[h:h + 1, :]
        c_ref[0, h] = dec_h * c_h + lax.dot_general(
            kg.astype(BF16), vh, _TN, preferred_element_type=F32)
        n_ref[0, h:h + 1, :] = dec_h * n_h + jnp.sum(kg, axis=0, keepdims=True)
        if h == N_HEADS - 1:
            m_ref[0] = jnp.broadcast_to(m_new, (SUBLANES, LANES))
        yield


def _run(phases):
    for _ in phases:
        pass


def _mlstm_state_specs(index_map3, index_map4):
    return [
        pl.BlockSpec((1, N_HEADS, HEAD_DIM, HEAD_DIM), index_map4),
        pl.BlockSpec((1, N_HEADS, HEAD_DIM), index_map3),
        pl.BlockSpec((1, SUBLANES, LANES), index_map3),
    ]


def _mlstm_state_shapes(batch):
    return [
        jax.ShapeDtypeStruct((batch, N_HEADS, HEAD_DIM, HEAD_DIM), F32),
        jax.ShapeDtypeStruct((batch, N_HEADS, HEAD_DIM), F32),
        jax.ShapeDtypeStruct((batch, SUBLANES, LANES), F32),
    ]


def _mlstm_kernel(q_ref, k_ref, v_ref, o_ref, gt_ref, gain_ref, c0_ref, n0_ref, m0_ref,
                  mix_ref, c_ref, n_ref, m_ref, *, chunk):
    c_ref[...] = c0_ref[...]
    n_ref[...] = n0_ref[...]
    m_ref[...] = m0_ref[...]
    _run(_mlstm_phases(q_ref, k_ref, v_ref, o_ref, gt_ref[0], gain_ref, mix_ref, c_ref, n_ref,
                       m_ref, chunk))


def _mlstm_single_chunk(proj, gt3, gain, init, *, batch, seq):
    tok = lambda col: pl.BlockSpec((seq, D_GROUP), lambda b, col=col: (b, col))
    state_specs = _mlstm_state_specs(lambda b: (b, 0, 0), lambda b: (b, 0, 0, 0))
    return pl.pallas_call(
        functools.partial(_mlstm_kernel, chunk=seq),
        grid=(batch,),
        in_specs=[tok(0), tok(1), tok(2), tok(3),
                  pl.BlockSpec((1, SUBLANES, gt3.shape[2]), lambda b: (b, 0, 0)),
                  pl.BlockSpec((1, D_GROUP), lambda b: (0, 0))] + state_specs,
        out_specs=[pl.BlockSpec((seq, D_GROUP), lambda b: (b, 0))] + state_specs,
        out_shape=[jax.ShapeDtypeStruct((batch * seq, D_GROUP), BF16)] + _mlstm_state_shapes(batch),
        compiler_params=_params(1),
        name="mlstm",
    )(proj, proj, proj, proj, gt3, gain, *init)


def _ret_phases(lg_ref, q_ref, k_ref, v_ref, g_ref, cos_ref, sin_ref, gain_ref, mix_ref, r_ref, L):
    half = HEAD_DIM // 2
    cos = cos_ref[...]
    sin = sin_ref[...]
    cos_k = cos * K_SCALE
    sin_k = sin * K_SCALE
    t_idx = lax.broadcasted_iota(jnp.int32, (L, L), 0)
    s_idx = lax.broadcasted_iota(jnp.int32, (L, L), 1)
    diff = t_idx - s_idx
    diff_pos = jnp.maximum(diff, 0).astype(F32)
    t_col = lax.broadcasted_iota(jnp.int32, (L, 1), 0).astype(F32)

    for h in range(N_HEADS):
        lo = slice(h * HEAD_DIM, h * HEAD_DIM + half)
        hi = slice(h * HEAD_DIM + half, (h + 1) * HEAD_DIM)
        sl = slice(h * HEAD_DIM, (h + 1) * HEAD_DIM)
        lg = lg_ref[h]
        q1 = q_ref[:, lo].astype(F32)
        q2 = q_ref[:, hi].astype(F32)
        k1 = k_ref[:, lo].astype(F32)
        k2 = k_ref[:, hi].astype(F32)
        qr = jnp.concatenate([q1 * cos - q2 * sin, q2 * cos + q1 * sin], axis=-1).astype(BF16)
        kr = jnp.concatenate([k1 * cos_k - k2 * sin_k, k2 * cos_k + k1 * sin_k], axis=-1)
        vh = v_ref[:, sl]

        dmask = jnp.where(diff >= 0, jnp.exp(lg * diff_pos), 0.0)
        inter = jnp.exp(lg * (t_col + 1.0))
        kdec = jnp.exp(lg * (L - 1.0 - t_col))
        cdec = jnp.exp(lg * jnp.full((1, 1), L, F32))

        s = lax.dot_general(qr, kr.astype(BF16), _NT, preferred_element_type=F32)
        yield
        r_h = r_ref[0, h]
        o = _dot((s * dmask).astype(BF16), vh) + inter * _dot(qr, r_h.astype(BF16))
        r_ref[0, h] = cdec * r_h + lax.dot_general(
            (kr * kdec).astype(BF16), vh, _TN, preferred_element_type=F32)

        oc = o - jnp.mean(o, axis=-1, keepdims=True)
        y = oc * _rms_scale(oc) * gain_ref[:, sl] * jax.nn.silu(g_ref[:, sl].astype(F32))
        mix_ref[:, sl] = y.astype(BF16)
        yield


def _ret_kernel(lg_ref, q_ref, k_ref, v_ref, g_ref, cos_ref, sin_ref, gain_ref, r0_ref,
                mix_ref, r_ref, *, chunk):
    r_ref[...] = r0_ref[...]
    _run(_ret_phases(lg_ref, q_ref, k_ref, v_ref, g_ref, cos_ref, sin_ref, gain_ref, mix_ref,
                     r_ref, chunk))


def _retention_single_chunk(proj, log_gamma, cos, sin, gain, init, *, batch, seq):
    tok = lambda col: pl.BlockSpec((seq, D_GROUP), lambda b, col=col: (b, col))
    state_spec = pl.BlockSpec((1, N_HEADS, HEAD_DIM, HEAD_DIM), lambda b: (b, 0, 0, 0))
    rope_spec = pl.BlockSpec((seq, HEAD_DIM // 2), lambda b: (0, 0))
    return pl.pallas_call(
        functools.partial(_ret_kernel, chunk=seq),
        grid=(batch,),
        in_specs=[pl.BlockSpec(memory_space=pltpu.SMEM),
                  tok(4), tok(5), tok(6), tok(7), rope_spec, rope_spec,
                  pl.BlockSpec((1, D_GROUP), lambda b: (0, 0)), state_spec],
        out_specs=[pl.BlockSpec((seq, D_GROUP), lambda b: (b, 0)), state_spec],
        out_shape=[
            jax.ShapeDtypeStruct((batch * seq, D_GROUP), BF16),
            jax.ShapeDtypeStruct((batch, N_HEADS, HEAD_DIM, HEAD_DIM), F32),
        ],
        compiler_params=_params(1),
        name="retention",
    )(log_gamma, proj, proj, proj, proj, cos, sin, gain, init)


def _prompt_kernel(lg_ref, x_ref, g1_ref, w_ref, wgate_ref, bgate_ref, gain_m_ref, cos_ref,
                   sin_ref, gain_r_ref, mixm_ref, c_ref, n_ref, m_ref, mixr_ref, r_ref,
                   xn_scr, proj_scr, gt_scr, *, chunks_per_seq, chunk):
    t = pl.program_id(0)
    n_chunks = pl.num_programs(0) - 1
    piece = D_PROJ // N_HEADS

    def inproj_pieces(slot):
        x = x_ref[...]
        xn_scr[...] = (x * _rms_scale(x) * g1_ref[...]).astype(BF16)
        gates = _dot(xn_scr[...], wgate_ref[...]) + bgate_ref[...]
        gt_scr[slot] = gates.T[0:N_GATES, :]
        for p in range(N_HEADS):
            cols = pl.ds(p * piece, piece)
            proj_scr[slot, :, cols] = _dot(xn_scr[...], w_ref[:, cols]).astype(BF16)
            yield

    def mixers(slot):
        group = lambda k: proj_scr.at[slot, :, pl.ds(k * D_GROUP, D_GROUP)]
        return [
            _mlstm_phases(group(0), group(1), group(2), group(3), gt_scr[slot], gain_m_ref,
                          mixm_ref, c_ref, n_ref, m_ref, chunk),
            _ret_phases(lg_ref, group(4), group(5), group(6), group(7), cos_ref, sin_ref,
                        gain_r_ref, mixr_ref, r_ref, chunk),
        ]

    @pl.when((t >= 1) & ((t - 1) % chunks_per_seq == 0))
    def _():
        c_ref[...] = jnp.zeros_like(c_ref)
        n_ref[...] = jnp.zeros_like(n_ref)
        m_ref[...] = jnp.zeros_like(m_ref)
        r_ref[...] = jnp.zeros_like(r_ref)

    @pl.when(t == 0)
    def _():
        _run(inproj_pieces(0))

    @pl.when((t >= 1) & (t < n_chunks))
    def _():
        slot = t % 2
        pieces = inproj_pieces(slot)
        heads = mixers(1 - slot)
        for _ in range(N_HEADS):
            for phases in heads:
                next(phases)
            next(pieces)
            for phases in heads:
                next(phases)

    @pl.when(t == n_chunks)
    def _():
        for phases in mixers((n_chunks - 1) % 2):
            _run(phases)


def _prompt_front(x2d, w, rope, *, batch, seq):
    chunk = PROMPT_CHUNK
    chunks_per_seq = seq // chunk
    n_chunks = batch * chunks_per_seq
    cos, sin = rope
    const = lambda shape, **kw: pl.BlockSpec(shape, lambda t: (0,) * len(shape), **kw)
    mixed = lambda t: jnp.maximum(t - 1, 0)
    rope_spec = pl.BlockSpec((chunk, HEAD_DIM // 2), lambda t: (mixed(t) % chunks_per_seq, 0))
    mix_spec = pl.BlockSpec((chunk, D_GROUP), lambda t: (mixed(t), 0))
    mix_shape = jax.ShapeDtypeStruct((batch * seq, D_GROUP), BF16)
    seq_of = lambda t: mixed(t) // chunks_per_seq
    state4 = pl.BlockSpec((1, N_HEADS, HEAD_DIM, HEAD_DIM), lambda t: (seq_of(t), 0, 0, 0))
    state_specs = _mlstm_state_specs(lambda t: (seq_of(t), 0, 0), lambda t: (seq_of(t), 0, 0, 0))
    return pl.pallas_call(
        functools.partial(_prompt_kernel, chunks_per_seq=chunks_per_seq, chunk=chunk),
        grid=(n_chunks + 1,),
        in_specs=[
            pl.BlockSpec(memory_space=pltpu.SMEM),
            pl.BlockSpec((chunk, D_MODEL), lambda t: (jnp.minimum(t, n_chunks - 1), 0)),
            const((1, D_MODEL)),
            const((D_MODEL, D_PROJ), pipeline_mode=pl.Buffered(1)),
            const((D_MODEL, LANES), pipeline_mode=pl.Buffered(1)),
            const((1, LANES)),
            const((1, D_GROUP)),
            rope_spec, rope_spec,
            const((1, D_GROUP)),
        ],
        out_specs=[mix_spec] + state_specs + [mix_spec, state4],
        out_shape=([mix_shape] + _mlstm_state_shapes(batch)
                   + [mix_shape, jax.ShapeDtypeStruct((batch, N_HEADS, HEAD_DIM, HEAD_DIM), F32)]),
        scratch_shapes=[
            pltpu.VMEM((chunk, D_MODEL), BF16),
            pltpu.VMEM((2, chunk, D_PROJ), BF16),
            pltpu.VMEM((2, N_GATES, chunk), F32),
        ],
        compiler_params=_params(1),
        name="prompt_front",
    )(w["log_gamma"], x2d, w["g1"], w["w_in"], w["w_gatecols"], w["b_gate"], w["gain_m"],
      cos, sin, w["gain_r"])


def _outproj_kernel(x_ref, mm_ref, mr_ref, wm_ref, wr_ref, g2_ref, x1_ref, hn_ref):
    x1 = x_ref[...] + _dot(mm_ref[...], wm_ref[...]) + _dot(mr_ref[...], wr_ref[...])
    x1_ref[...] = x1
    hn_ref[...] = (x1 * _rms_scale(x1) * g2_ref[...]).astype(BF16)


def _outproj(x, mix_m, mix_r, w_out, g2, *, tm):
    n = x.shape[0]
    w_spec = lambda half: pl.BlockSpec((D_GROUP, D_MODEL), lambda i, half=half: (half, 0),
                                       pipeline_mode=pl.Buffered(1))
    return pl.pallas_call(
        _outproj_kernel,
        grid=(n // tm,),
        in_specs=[
            pl.BlockSpec((tm, D_MODEL), lambda i: (i, 0)),
            pl.BlockSpec((tm, D_GROUP), lambda i: (i, 0)),
            pl.BlockSpec((tm, D_GROUP), lambda i: (i, 0)),
            w_spec(0), w_spec(1),
            pl.BlockSpec((1, D_MODEL), lambda i: (0, 0)),
        ],
        out_specs=[
            pl.BlockSpec((tm, D_MODEL), lambda i: (i, 0)),
            pl.BlockSpec((tm, D_MODEL), lambda i: (i, 0)),
        ],
        out_shape=[
            jax.ShapeDtypeStruct((n, D_MODEL), F32),
            jax.ShapeDtypeStruct((n, D_MODEL), BF16),
        ],
        compiler_params=_params(1),
        name="outproj",
    )(x, mix_m, mix_r, w_out, w_out, g2)


def _ffn_kernel(hn_ref, x1_hbm, wg_ref, wu_ref, wd_ref, gf_ref, y_ref, x1_buf, x1_sem):
    i = pl.program_id(0)
    f = pl.program_id(1)
    tm = x1_buf.shape[0]

    def x1_copy(tile):
        return pltpu.make_async_copy(x1_hbm.at[pl.ds(tile * tm, tm), :], x1_buf, x1_sem)

    @pl.when((f == 0) & (i == 0))
    def _():
        x1_copy(0).start()

    @pl.when(f == 0)
    def _():
        x1_copy(i).wait()
        y_ref[...] = x1_buf[...]

    @pl.when((f == 1) & (i + 1 < pl.num_programs(0)))
    def _():
        x1_copy(i + 1).start()

    hn = hn_ref[...]
    gate = _dot(hn, wg_ref[...])
    up = _dot(hn, wu_ref[...])
    y_ref[...] += _dot((jax.nn.silu(gate) * up).astype(BF16), wd_ref[...])

    @pl.when(f == pl.num_programs(1) - 1)
    def _():
        x2 = y_ref[...]
        y_ref[...] = x2 * _rms_scale(x2) * gf_ref[...]


def _ffn(hn, x1, w_gate, w_up, w_down, g_final, *, tm, tf):
    n = hn.shape[0]
    return pl.pallas_call(
        _ffn_kernel,
        grid=(n // tm, D_FF // tf),
        in_specs=[
            pl.BlockSpec((tm, D_MODEL), lambda i, f: (i, 0)),
            pl.BlockSpec(memory_space=pl.ANY),
            pl.BlockSpec((D_MODEL, tf), lambda i, f: (0, f)),
            pl.BlockSpec((D_MODEL, tf), lambda i, f: (0, f)),
            pl.BlockSpec((tf, D_MODEL), lambda i, f: (f, 0)),
            pl.BlockSpec((1, D_MODEL), lambda i, f: (0, 0)),
        ],
        out_specs=pl.BlockSpec((tm, D_MODEL), lambda i, f: (i, 0)),
        out_shape=jax.ShapeDtypeStruct((n, D_MODEL), F32),
        scratch_shapes=[pltpu.VMEM((tm, D_MODEL), F32), pltpu.SemaphoreType.DMA(())],
        compiler_params=_params(2),
        name="ffn",
    )(hn, x1, w_gate, w_up, w_down, g_final)


def _rope_tables(pos):
    half = HEAD_DIM // 2
    freqs = ROPE_BASE ** (-jnp.arange(half, dtype=F32) / half)
    ang = pos[:, None] * freqs[None, :]
    return jnp.cos(ang), jnp.sin(ang)


def _tail(x2d, mix_m, mix_r, w):
    x1, hn = _outproj(x2d, mix_m, mix_r, w["w_out"], w["g2"], tm=512)
    return _ffn(hn, x1, w["w_gate"], w["w_up"], w["w_down"], w["g_final"], tm=1024, tf=512)


def _prompt_trunk(x, w):
    batch, seq, _ = x.shape
    x2d = x.reshape(batch * seq, D_MODEL)
    rope = _rope_tables(jnp.arange(seq, dtype=F32))
    mix_m, c_new, n_new, m_new, mix_r, r_new = _prompt_front(x2d, w, rope, batch=batch, seq=seq)
    y = _tail(x2d, mix_m, mix_r, w)
    return (y.reshape(batch, seq, D_MODEL), c_new[None], n_new[None],
            m_new[:, :N_HEADS, 0][None], r_new[None])


def _sample_trunk(x, init, w):
    batch, seq, _ = x.shape
    x2d = x.reshape(batch * seq, D_MODEL)
    proj, gt = _inproj(x2d, w["g1"], w["w_in"], w["w_gatecols"], w["b_gate"],
                       tm=INPROJ_TM, tn=INPROJ_TN)
    gt3 = gt.reshape(N_GATES, batch, seq).transpose(1, 0, 2)
    gt3 = jnp.pad(gt3, ((0, 0), (0, 0), (0, max(0, LANES - seq))))
    c0, n0, m0, r0 = init
    m0 = jnp.pad(m0, ((0, 0), (0, SUBLANES - N_HEADS)))
    m0 = jnp.broadcast_to(m0[:, :, None], (batch, SUBLANES, LANES))
    mix_m, c_new, n_new, m_new = _mlstm_single_chunk(proj, gt3, w["gain_m"], (c0, n0, m0),
                                                     batch=batch, seq=seq)
    cos, sin = _rope_tables(PAST_LEN + jnp.arange(seq, dtype=F32))
    mix_r, r_new = _retention_single_chunk(proj, w["log_gamma"], cos, sin, w["gain_r"], r0,
                                           batch=batch, seq=seq)
    y = _tail(x2d, mix_m, mix_r, w)
    return (y.reshape(batch, seq, D_MODEL), c_new[None], n_new[None],
            m_new[:, :N_HEADS, 0][None], r_new[None])


def kernel(x_prompt, x_sample, state_mlstm_C, state_mlstm_n, state_mlstm_m, state_ret,
           g_norm1, w_in, b_gates, g_mlstm_norm, g_ret_norm, w_out, g_norm2,
           w_gate, w_up, w_down, g_final):
    w_in0 = w_in[0].astype(BF16)
    w = dict(
        g1=g_norm1[0].reshape(1, D_MODEL),
        w_in=w_in0,
        w_gatecols=jnp.pad(w_in0[:, D_PROJ:], ((0, 0), (0, LANES - N_GATES))),
        b_gate=jnp.pad(b_gates[0], (0, LANES - N_GATES)).reshape(1, LANES),
        gain_m=g_mlstm_norm[0].reshape(1, D_GROUP),
        gain_r=g_ret_norm[0].reshape(1, D_GROUP),
        w_out=w_out[0].astype(BF16),
        g2=g_norm2[0].reshape(1, D_MODEL),
        w_gate=w_gate[0].astype(BF16),
        w_up=w_up[0].astype(BF16),
        w_down=w_down[0].astype(BF16),
        g_final=g_final.reshape(1, D_MODEL),
        log_gamma=jnp.log(1.0 - jnp.exp2(-5.0 - jnp.arange(N_HEADS, dtype=F32))),
    )
    y_p, c_p, n_p, m_p, r_p = _prompt_trunk(x_prompt, w)
    init_s = (state_mlstm_C[0], state_mlstm_n[0], state_mlstm_m[0], state_ret[0])
    y_s, c_s, n_s, m_s, r_s = _sample_trunk(x_sample, init_s, w)
    return (y_p, y_s, c_p, n_p, m_p, r_p, c_s, n_s, m_s, r_s)
```

```python
import functools

import jax
import jax.numpy as jnp
from jax import lax
from jax.experimental import pallas as pl
from jax.experimental.pallas import tpu as pltpu

F32 = jnp.float32
BF16 = jnp.bfloat16

D_MODEL = 2048
N_HEADS = 4
HEAD_DIM = 256
D_GROUP = N_HEADS * HEAD_DIM
D_PROJ = 8 * D_GROUP
N_GATES = 2 * N_HEADS
D_FF = 5632
ROPE_BASE = 10000.0
PAST_LEN = 4096
EPS = 1e-6
K_SCALE = HEAD_DIM ** -0.5

LANES = 128
SUBLANES = 8
VMEM_LIMIT = 60 * 1024 * 1024

PROMPT_CHUNK = 256
INPROJ_TM = 1024
INPROJ_TN = 2048

_NT = (((1,), (1,)), ((), ()))
_TN = (((0,), (0,)), ((), ()))


def _dot(a, b):
    return jnp.dot(a, b, preferred_element_type=F32)


def _rms_scale(x):
    return lax.rsqrt(jnp.mean(x * x, axis=-1, keepdims=True) + EPS)


def _params(n_axes):
    return pltpu.CompilerParams(
        dimension_semantics=("arbitrary",) * n_axes, vmem_limit_bytes=VMEM_LIMIT)


def _inproj_prologue(x_ref, g1_ref, wgate_ref, bgate_ref, gt_ref, xn_ref):
    x = x_ref[...]
    xn = (x * _rms_scale(x) * g1_ref[...]).astype(BF16)
    xn_ref[...] = xn
    gates = _dot(xn, wgate_ref[...]) + bgate_ref[...]
    gt_ref[...] = gates.T[0:N_GATES, :]


def _inproj_kernel(x_ref, g1_ref, w_ref, wgate_ref, bgate_ref, proj_ref, gt_ref, *rest):
    *w16_refs, xn_ref = rest

    @pl.when(pl.program_id(1) == 0)
    def _():
        _inproj_prologue(x_ref, g1_ref, wgate_ref, bgate_ref, gt_ref, xn_ref)

    w = w_ref[...].astype(BF16)
    for w16_ref in w16_refs:
        w16_ref[...] = w
    proj_ref[...] = _dot(xn_ref[...], w).astype(BF16)


def _inproj(x, g1, w_main, w_gate, b_gate, *, tm, tn):
    n = x.shape[0]
    emit = w_main.dtype != BF16
    assert not emit or n == tm
    w_spec = pl.BlockSpec((D_MODEL, tn), lambda i, j: (0, j))
    return pl.pallas_call(
        _inproj_kernel,
        grid=(n // tm, D_PROJ // tn),
        in_specs=[
            pl.BlockSpec((tm, D_MODEL), lambda i, j: (i, 0)),
            pl.BlockSpec((1, D_MODEL), lambda i, j: (0, 0)),
            w_spec,
            pl.BlockSpec((D_MODEL, LANES), lambda i, j: (0, 0)),
            pl.BlockSpec((1, LANES), lambda i, j: (0, 0)),
        ],
        out_specs=[
            pl.BlockSpec((tm, tn), lambda i, j: (i, j)),
            pl.BlockSpec((N_GATES, tm), lambda i, j: (0, i)),
        ] + [w_spec] * emit,
        out_shape=[
            jax.ShapeDtypeStruct((n, D_PROJ), BF16),
            jax.ShapeDtypeStruct((N_GATES, n), F32),
        ] + [jax.ShapeDtypeStruct((D_MODEL, D_PROJ), BF16)] * emit,
        scratch_shapes=[pltpu.VMEM((tm, D_MODEL), BF16)],
        compiler_params=_params(2),
        name="inproj",
    )(x, g1, w_main, w_gate, b_gate)


def _lane_scan(x, combine, identity):
    width = x.shape[1]
    lane = lax.broadcasted_iota(jnp.int32, x.shape, 1)
    shift = 1
    while shift < width:
        shifted = jnp.where(lane >= shift, pltpu.roll(x, shift, axis=1), identity)
        x = combine(x, shifted)
        shift *= 2
    return x


def _mlstm_phases(q_ref, k_ref, v_ref, o_ref, gates, gain_ref, mix_ref, c_ref, n_ref, m_ref, L):
    b = _lane_scan(jax.nn.log_sigmoid(gates), jnp.add, 0.0)
    b = pltpu.roll(b, N_HEADS, axis=0)
    u = gates - b
    m_prev = m_ref[0][:, 0:1]
    big_m = jnp.maximum(m_prev, _lane_scan(u, jnp.maximum, -jnp.inf))
    a = jnp.exp(m_prev - big_m)
    e = jnp.exp(-(b + big_m))
    m_last = big_m[:, L - 1:L]
    g = jnp.exp(u - m_last)
    dec = a[:, L - 1:L]
    m_new = b[:, L - 1:L] + m_last
    rows = jnp.concatenate(
        [big_m, a, e, g, jnp.zeros((LANES - 4 * SUBLANES, gates.shape[1]), F32)], axis=0)
    cols = rows.T

    t_idx = lax.broadcasted_iota(jnp.int32, (L, L), 0)
    s_idx = lax.broadcasted_iota(jnp.int32, (L, L), 1)
    causal = s_idx <= t_idx

    for h in range(N_HEADS):
        sl = slice(h * HEAD_DIM, (h + 1) * HEAD_DIM)
        qh = q_ref[:, sl]
        kh = k_ref[:, sl] * jnp.asarray(K_SCALE, BF16)
        vh = v_ref[:, sl]
        m_col = cols[:L, h:h + 1]
        a_col = cols[:L, SUBLANES + h:SUBLANES + h + 1]
        e_col = cols[:L, 2 * SUBLANES + h:2 * SUBLANES + h + 1]
        g_col = cols[:L, 3 * SUBLANES + h:3 * SUBLANES + h + 1]
        u_row = u[h:h + 1, :L]

        s = lax.dot_general(qh, kh, _NT, preferred_element_type=F32)
        yield
        sw = s * jnp.exp(jnp.where(causal, u_row - m_col, -jnp.inf))
        c_h = c_ref[0, h]
        n_h = n_ref[0, h:h + 1, :]
        num = a_col * _dot(qh, c_h.astype(BF16)) + _dot(sw.astype(BF16), vh)
        qn = jnp.sum(qh.astype(F32) * n_h, axis=-1, keepdims=True)
        den = a_col * qn + jnp.sum(sw, axis=-1, keepdims=True)
        hh = num * (1.0 / jnp.maximum(jnp.abs(den), e_col))
        y = hh * _rms_scale(hh) * gain_ref[:, sl] * jax.nn.sigmoid(o_ref[:, sl].astype(F32))
        mix_ref[:, sl] = y.astype(BF16)

        kg = kh.astype(F32) * g_col
        dec_h = dec[h:h + 1, :]
        c_ref[0, h] = dec_h * c_h + lax.dot_general(
            kg.astype(BF16), vh, _TN, preferred_element_type=F32)
        n_ref[0, h:h + 1, :] = dec_h * n_h + jnp.sum(kg, axis=0, keepdims=True)
        if h == N_HEADS - 1:
            m_ref[0] = jnp.broadcast_to(m_new, (SUBLANES, LANES))
        yield


def _run(phases):
    for _ in phases:
        pass


def _mlstm_state_specs(index_map3, index_map4):
    return [
        pl.BlockSpec((1, N_HEADS, HEAD_DIM, HEAD_DIM), index_map4),
        pl.BlockSpec((1, N_HEADS, HEAD_DIM), index_map3),
        pl.BlockSpec((1, SUBLANES, LANES), index_map3),
    ]


def _mlstm_state_shapes(batch):
    return [
        jax.ShapeDtypeStruct((batch, N_HEADS, HEAD_DIM, HEAD_DIM), F32),
        jax.ShapeDtypeStruct((batch, N_HEADS, HEAD_DIM), F32),
        jax.ShapeDtypeStruct((batch, SUBLANES, LANES), F32),
    ]


def _mlstm_kernel(q_ref, k_ref, v_ref, o_ref, gt_ref, gain_ref, c0_ref, n0_ref, m0_ref,
                  mix_ref, c_ref, n_ref, m_ref, *, chunk):
    c_ref[...] = c0_ref[...]
    n_ref[...] = n0_ref[...]
    m_ref[...] = m0_ref[...]
    _run(_mlstm_phases(q_ref, k_ref, v_ref, o_ref, gt_ref[0], gain_ref, mix_ref, c_ref, n_ref,
                       m_ref, chunk))


def _mlstm_single_chunk(proj, gt3, gain, init, *, batch, seq):
    tok = lambda col: pl.BlockSpec((seq, D_GROUP), lambda b, col=col: (b, col))
    state_specs = _mlstm_state_specs(lambda b: (b, 0, 0), lambda b: (b, 0, 0, 0))
    return pl.pallas_call(
        functools.partial(_mlstm_kernel, chunk=seq),
        grid=(batch,),
        in_specs=[tok(0), tok(1), tok(2), tok(3),
                  pl.BlockSpec((1, SUBLANES, gt3.shape[2]), lambda b: (b, 0, 0)),
                  pl.BlockSpec((1, D_GROUP), lambda b: (0, 0))] + state_specs,
        out_specs=[pl.BlockSpec((seq, D_GROUP), lambda b: (b, 0))] + state_specs,
        out_shape=[jax.ShapeDtypeStruct((batch * seq, D_GROUP), BF16)] + _mlstm_state_shapes(batch),
        compiler_params=_params(1),
        name="mlstm",
    )(proj, proj, proj, proj, gt3, gain, *init)


def _ret_phases(lg_ref, q_ref, k_ref, v_ref, g_ref, cos_ref, sin_ref, gain_ref, mix_ref, r_ref, L):
    half = HEAD_DIM // 2
    cos = cos_ref[...]
    sin = sin_ref[...]
    cos_k = cos * K_SCALE
    sin_k = sin * K_SCALE
    t_idx = lax.broadcasted_iota(jnp.int32, (L, L), 0)
    s_idx = lax.broadcasted_iota(jnp.int32, (L, L), 1)
    diff = t_idx - s_idx
    diff_pos = jnp.maximum(diff, 0).astype(F32)
    t_col = lax.broadcasted_iota(jnp.int32, (L, 1), 0).astype(F32)

    for h in range(N_HEADS):
        lo = slice(h * HEAD_DIM, h * HEAD_DIM + half)
        hi = slice(h * HEAD_DIM + half, (h + 1) * HEAD_DIM)
        sl = slice(h * HEAD_DIM, (h + 1) * HEAD_DIM)
        lg = lg_ref[h]
        q1 = q_ref[:, lo].astype(F32)
        q2 = q_ref[:, hi].astype(F32)
        k1 = k_ref[:, lo].astype(F32)
        k2 = k_ref[:, hi].astype(F32)
        qr = jnp.concatenate([q1 * cos - q2 * sin, q2 * cos + q1 * sin], axis=-1).astype(BF16)
        kr = jnp.concatenate([k1 * cos_k - k2 * sin_k, k2 * cos_k + k1 * sin_k], axis=-1)
        vh = v_ref[:, sl]

        dmask = jnp.where(diff >= 0, jnp.exp(lg * diff_pos), 0.0)
        inter = jnp.exp(lg * (t_col + 1.0))
        kdec = jnp.exp(lg * (L - 1.0 - t_col))
        cdec = jnp.exp(lg * jnp.full((1, 1), L, F32))

        s = lax.dot_general(qr, kr.astype(BF16), _NT, preferred_element_type=F32)
        yield
        r_h = r_ref[0, h]
        o = _dot((s * dmask).astype(BF16), vh) + inter * _dot(qr, r_h.astype(BF16))
        r_ref[0, h] = cdec * r_h + lax.dot_general(
            (kr * kdec).astype(BF16), vh, _TN, preferred_element_type=F32)

        oc = o - jnp.mean(o, axis=-1, keepdims=True)
        y = oc * _rms_scale(oc) * gain_ref[:, sl] * jax.nn.silu(g_ref[:, sl].astype(F32))
        mix_ref[:, sl] = y.astype(BF16)
        yield


def _ret_kernel(lg_ref, q_ref, k_ref, v_ref, g_ref, cos_ref, sin_ref, gain_ref, r0_ref,
                mix_ref, r_ref, *, chunk):
    r_ref[...] = r0_ref[...]
    _run(_ret_phases(lg_ref, q_ref, k_ref, v_ref, g_ref, cos_ref, sin_ref, gain_ref, mix_ref,
                     r_ref, chunk))


def _retention_single_chunk(proj, log_gamma, cos, sin, gain, init, *, batch, seq):
    tok = lambda col: pl.BlockSpec((seq, D_GROUP), lambda b, col=col: (b, col))
    state_spec = pl.BlockSpec((1, N_HEADS, HEAD_DIM, HEAD_DIM), lambda b: (b, 0, 0, 0))
    rope_spec = pl.BlockSpec((seq, HEAD_DIM // 2), lambda b: (0, 0))
    return pl.pallas_call(
        functools.partial(_ret_kernel, chunk=seq),
        grid=(batch,),
        in_specs=[pl.BlockSpec(memory_space=pltpu.SMEM),
                  tok(4), tok(5), tok(6), tok(7), rope_spec, rope_spec,
                  pl.BlockSpec((1, D_GROUP), lambda b: (0, 0)), state_spec],
        out_specs=[pl.BlockSpec((seq, D_GROUP), lambda b: (b, 0)), state_spec],
        out_shape=[
            jax.ShapeDtypeStruct((batch * seq, D_GROUP), BF16),
            jax.ShapeDtypeStruct((batch, N_HEADS, HEAD_DIM, HEAD_DIM), F32),
        ],
        compiler_params=_params(1),
        name="retention",
    )(log_gamma, proj, proj, proj, proj, cos, sin, gain, init)


def _prompt_kernel(lg_ref, x_ref, g1_ref, w_ref, wgate_ref, bgate_ref, gain_m_ref, cos_ref,
                   sin_ref, gain_r_ref, mixm_ref, c_ref, n_ref, m_ref, mixr_ref, r_ref,
                   xn_scr, proj_scr, gt_scr, *, chunks_per_seq, chunk):
    t = pl.program_id(0)
    n_chunks = pl.num_programs(0) - 1
    piece = D_PROJ // N_HEADS

    def inproj_pieces(slot):
        x = x_ref[...]
        xn_scr[...] = (x * _rms_scale(x) * g1_ref[...]).astype(BF16)
        gates = _dot(xn_scr[...], wgate_ref[...]) + bgate_ref[...]
        gt_scr[slot] = gates.T[0:N_GATES, :]
        for p in range(N_HEADS):
            cols = pl.ds(p * piece, piece)
            proj_scr[slot, :, cols] = _dot(xn_scr[...], w_ref[:, cols]).astype(BF16)
            yield

    def mixers(slot):
        group = lambda k: proj_scr.at[slot, :, pl.ds(k * D_GROUP, D_GROUP)]
        return [
            _mlstm_phases(group(0), group(1), group(2), group(3), gt_scr[slot], gain_m_ref,
                          mixm_ref, c_ref, n_ref, m_ref, chunk),
            _ret_phases(lg_ref, group(4), group(5), group(6), group(7), cos_ref, sin_ref,
                        gain_r_ref, mixr_ref, r_ref, chunk),
        ]

    @pl.when((t >= 1) & ((t - 1) % chunks_per_seq == 0))
    def _():
        c_ref[...] = jnp.zeros_like(c_ref)
        n_ref[...] = jnp.zeros_like(n_ref)
        m_ref[...] = jnp.zeros_like(m_ref)
        r_ref[...] = jnp.zeros_like(r_ref)

    @pl.when(t == 0)
    def _():
        _run(inproj_pieces(0))

    @pl.when((t >= 1) & (t < n_chunks))
    def _():
        slot = t % 2
        pieces = inproj_pieces(slot)
        heads = mixers(1 - slot)
        for _ in range(N_HEADS):
            for phases in heads:
                next(phases)
            next(pieces)
            for phases in heads:
                next(phases)

    @pl.when(t == n_chunks)
    def _():
        for phases in mixers((n_chunks - 1) % 2):
            _run(phases)


def _prompt_front(x2d, w, rope, *, batch, seq):
    chunk = PROMPT_CHUNK
    chunks_per_seq = seq // chunk
    n_chunks = batch * chunks_per_seq
    cos, sin = rope
    const = lambda shape, **kw: pl.BlockSpec(shape, lambda t: (0,) * len(shape), **kw)
    mixed = lambda t: jnp.maximum(t - 1, 0)
    rope_spec = pl.BlockSpec((chunk, HEAD_DIM // 2), lambda t: (mixed(t) % chunks_per_seq, 0))
    mix_spec = pl.BlockSpec((chunk, D_GROUP), lambda t: (mixed(t), 0))
    mix_shape = jax.ShapeDtypeStruct((batch * seq, D_GROUP), BF16)
    seq_of = lambda t: mixed(t) // chunks_per_seq
    state4 = pl.BlockSpec((1, N_HEADS, HEAD_DIM, HEAD_DIM), lambda t: (seq_of(t), 0, 0, 0))
    state_specs = _mlstm_state_specs(lambda t: (seq_of(t), 0, 0), lambda t: (seq_of(t), 0, 0, 0))
    return pl.pallas_call(
        functools.partial(_prompt_kernel, chunks_per_seq=chunks_per_seq, chunk=chunk),
        grid=(n_chunks + 1,),
        in_specs=[
            pl.BlockSpec(memory_space=pltpu.SMEM),
            pl.BlockSpec((chunk, D_MODEL), lambda t: (jnp.minimum(t, n_chunks - 1), 0)),
            const((1, D_MODEL)),
            const((D_MODEL, D_PROJ), pipeline_mode=pl.Buffered(1)),
            const((D_MODEL, LANES), pipeline_mode=pl.Buffered(1)),
            const((1, LANES)),
            const((1, D_GROUP)),
            rope_spec, rope_spec,
            const((1, D_GROUP)),
        ],
        out_specs=[mix_spec] + state_specs + [mix_spec, state4],
        out_shape=([mix_shape] + _mlstm_state_shapes(batch)
                   + [mix_shape, jax.ShapeDtypeStruct((batch, N_HEADS, HEAD_DIM, HEAD_DIM), F32)]),
        scratch_shapes=[
            pltpu.VMEM((chunk, D_MODEL), BF16),
            pltpu.VMEM((2, chunk, D_PROJ), BF16),
            pltpu.VMEM((2, N_GATES, chunk), F32),
        ],
        compiler_params=_params(1),
        name="prompt_front",
    )(w["log_gamma"], x2d, w["g1"], w["w_in"], w["w_gatecols"], w["b_gate"], w["gain_m"],
      cos, sin, w["gain_r"])


def _outproj_kernel(x_ref, mm_ref, mr_ref, wm_ref, wr_ref, g2_ref, x1_ref, hn_ref):
    x1 = x_ref[...] + _dot(mm_ref[...], wm_ref[...]) + _dot(mr_ref[...], wr_ref[...])
    x1_ref[...] = x1
    hn_ref[...] = (x1 * _rms_scale(x1) * g2_ref[...]).astype(BF16)


def _outproj(x, mix_m, mix_r, w_out, g2, *, tm):
    n = x.shape[0]
    w_spec = lambda half: pl.BlockSpec((D_GROUP, D_MODEL), lambda i, half=half: (half, 0),
                                       pipeline_mode=pl.Buffered(1))
    return pl.pallas_call(
        _outproj_kernel,
        grid=(n // tm,),
        in_specs=[
            pl.BlockSpec((tm, D_MODEL), lambda i: (i, 0)),
            pl.BlockSpec((tm, D_GROUP), lambda i: (i, 0)),
            pl.BlockSpec((tm, D_GROUP), lambda i: (i, 0)),
            w_spec(0), w_spec(1),
            pl.BlockSpec((1, D_MODEL), lambda i: (0, 0)),
        ],
        out_specs=[
            pl.BlockSpec((tm, D_MODEL), lambda i: (i, 0)),
            pl.BlockSpec((tm, D_MODEL), lambda i: (i, 0)),
        ],
        out_shape=[
            jax.ShapeDtypeStruct((n, D_MODEL), F32),
            jax.ShapeDtypeStruct((n, D_MODEL), BF16),
        ],
        compiler_params=_params(1),
        name="outproj",
    )(x, mix_m, mix_r, w_out, w_out, g2)


def _ffn_kernel(hn_ref, x1_hbm, wg_ref, wu_ref, wd_ref, gf_ref, y_ref, *rest):
    *w16_refs, x1_buf, x1_sem = rest
    i = pl.program_id(0)
    f = pl.program_id(1)
    last = pl.num_programs(1) - 1
    tm = x1_buf.shape[0]

    def x1_copy(tile):
        return pltpu.make_async_copy(x1_hbm.at[pl.ds(tile * tm, tm), :], x1_buf, x1_sem)

    @pl.when((f == 0) & (i == 0))
    def _():
        x1_copy(0).start()

    @pl.when(f == 0)
    def _():
        x1_copy(i).wait()

    @pl.when((f == 1) & (i + 1 < pl.num_programs(0)))
    def _():
        x1_copy(i + 1).start()

    def step(first, final):
        weights = [ref[...].astype(BF16) for ref in (wg_ref, wu_ref, wd_ref)]
        for w16_ref, w16 in zip(w16_refs, weights):
            w16_ref[...] = w16
        wg, wu, wd = weights
        hn = hn_ref[...]
        part = _dot((jax.nn.silu(_dot(hn, wg)) * _dot(hn, wu)).astype(BF16), wd)
        if first:
            y_ref[...] = x1_buf[...] + part
        else:
            y_ref[...] += part
        if final:
            x2 = y_ref[...]
            y_ref[...] = x2 * _rms_scale(x2) * gf_ref[...]

    pl.when(f == 0)(functools.partial(step, True, False))
    pl.when((f > 0) & (f < last))(functools.partial(step, False, False))
    pl.when(f == last)(functools.partial(step, False, True))


def _ffn(hn, x1, w_gate, w_up, w_down, g_final, *, tm, tf):
    n = hn.shape[0]
    emit = w_gate.dtype != BF16
    assert not emit or n == tm
    assert D_FF // tf >= 3
    w_specs = [
        pl.BlockSpec((D_MODEL, tf), lambda i, f: (0, f)),
        pl.BlockSpec((D_MODEL, tf), lambda i, f: (0, f)),
        pl.BlockSpec((tf, D_MODEL), lambda i, f: (f, 0)),
    ]
    w_shapes = [jax.ShapeDtypeStruct(wt.shape, BF16) for wt in (w_gate, w_up, w_down)]
    return pl.pallas_call(
        _ffn_kernel,
        grid=(n // tm, D_FF // tf),
        in_specs=[
            pl.BlockSpec((tm, D_MODEL), lambda i, f: (i, 0)),
            pl.BlockSpec(memory_space=pl.ANY),
            *w_specs,
            pl.BlockSpec((1, D_MODEL), lambda i, f: (0, 0)),
        ],
        out_specs=[pl.BlockSpec((tm, D_MODEL), lambda i, f: (i, 0))] + w_specs * emit,
        out_shape=[jax.ShapeDtypeStruct((n, D_MODEL), F32)] + w_shapes * emit,
        scratch_shapes=[pltpu.VMEM((tm, D_MODEL), F32), pltpu.SemaphoreType.DMA(())],
        compiler_params=_params(2),
        name="ffn",
    )(hn, x1, w_gate, w_up, w_down, g_final)


def _rope_tables(pos):
    half = HEAD_DIM // 2
    freqs = ROPE_BASE ** (-jnp.arange(half, dtype=F32) / half)
    ang = pos[:, None] * freqs[None, :]
    return jnp.cos(ang), jnp.sin(ang)


def _tail(x2d, mix_m, mix_r, w, *, tf):
    x1, hn = _outproj(x2d, mix_m, mix_r, w["w_out"], w["g2"], tm=512)
    return _ffn(hn, x1, w["w_gate"], w["w_up"], w["w_down"], w["g_final"], tm=1024, tf=tf)


def _prompt_trunk(x, w):
    batch, seq, _ = x.shape
    x2d = x.reshape(batch * seq, D_MODEL)
    rope = _rope_tables(jnp.arange(seq, dtype=F32))
    mix_m, c_new, n_new, m_new, mix_r, r_new = _prompt_front(x2d, w, rope, batch=batch, seq=seq)
    (y,) = _tail(x2d, mix_m, mix_r, w, tf=512)
    return (y.reshape(batch, seq, D_MODEL), c_new[None], n_new[None],
            m_new[:, :N_HEADS, 0][None], r_new[None])


def _sample_trunk(x, init, w):
    batch, seq, _ = x.shape
    x2d = x.reshape(batch * seq, D_MODEL)
    proj, gt, w_in16 = _inproj(x2d, w["g1"], w["w_in"], w["w_gatecols"], w["b_gate"],
                               tm=INPROJ_TM, tn=INPROJ_TN // 2)
    gt3 = gt.reshape(N_GATES, batch, seq).transpose(1, 0, 2)
    gt3 = jnp.pad(gt3, ((0, 0), (0, 0), (0, max(0, LANES - seq))))
    c0, n0, m0, r0 = init
    m0 = jnp.pad(m0, ((0, 0), (0, SUBLANES - N_HEADS)))
    m0 = jnp.broadcast_to(m0[:, :, None], (batch, SUBLANES, LANES))
    mix_m, c_new, n_new, m_new = _mlstm_single_chunk(proj, gt3, w["gain_m"], (c0, n0, m0),
                                                     batch=batch, seq=seq)
    cos, sin = _rope_tables(PAST_LEN + jnp.arange(seq, dtype=F32))
    mix_r, r_new = _retention_single_chunk(proj, w["log_gamma"], cos, sin, w["gain_r"], r0,
                                           batch=batch, seq=seq)
    y, w_gate16, w_up16, w_down16 = _tail(x2d, mix_m, mix_r, w, tf=256)
    w16 = dict(w_in=w_in16, w_gate=w_gate16, w_up=w_up16, w_down=w_down16)
    return (y.reshape(batch, seq, D_MODEL), c_new[None], n_new[None],
            m_new[:, :N_HEADS, 0][None], r_new[None]), w16


def kernel(x_prompt, x_sample, state_mlstm_C, state_mlstm_n, state_mlstm_m, state_ret,
           g_norm1, w_in, b_gates, g_mlstm_norm, g_ret_norm, w_out, g_norm2,
           w_gate, w_up, w_down, g_final):
    w_in0 = w_in[0]
    w = dict(
        g1=g_norm1[0].reshape(1, D_MODEL),
        w_in=w_in0,
        w_gatecols=jnp.pad(w_in0[:, D_PROJ:].astype(BF16), ((0, 0), (0, LANES - N_GATES))),
        b_gate=jnp.pad(b_gates[0], (0, LANES - N_GATES)).reshape(1, LANES),
        gain_m=g_mlstm_norm[0].reshape(1, D_GROUP),
        gain_r=g_ret_norm[0].reshape(1, D_GROUP),
        w_out=w_out[0].astype(BF16),
        g2=g_norm2[0].reshape(1, D_MODEL),
        w_gate=w_gate[0],
        w_up=w_up[0],
        w_down=w_down[0],
        g_final=g_final.reshape(1, D_MODEL),
        log_gamma=jnp.log(1.0 - jnp.exp2(-5.0 - jnp.arange(N_HEADS, dtype=F32))),
    )
    init_s = (state_mlstm_C[0], state_mlstm_n[0], state_mlstm_m[0], state_ret[0])
    (y_s, c_s, n_s, m_s, r_s), w16 = _sample_trunk(x_sample, init_s, w)
    y_p, c_p, n_p, m_p, r_p = _prompt_trunk(x_prompt, {**w, **w16})
    return (y_p, y_s, c_p, n_p, m_p, r_p, c_s, n_s, m_s, r_s)
```

```python
import functools

import jax
import jax.numpy as jnp
from jax import lax
from jax.experimental import pallas as pl
from jax.experimental.pallas import tpu as pltpu

F32 = jnp.float32
BF16 = jnp.bfloat16

D_MODEL = 2048
N_HEADS = 4
HEAD_DIM = 256
D_GROUP = N_HEADS * HEAD_DIM
D_PROJ = 8 * D_GROUP
N_GATES = 2 * N_HEADS
D_FF = 5632
ROPE_BASE = 10000.0
PAST_LEN = 4096
EPS = 1e-6
K_SCALE = HEAD_DIM ** -0.5

LANES = 128
SUBLANES = 8
VMEM_LIMIT = 60 * 1024 * 1024

PROMPT_CHUNK = 256
INPROJ_TM = 1024
INPROJ_TN = 2048
SAMPLE_GROUP = 4

_NT = (((1,), (1,)), ((), ()))
_TN = (((0,), (0,)), ((), ()))


def _dot(a, b):
    return jnp.dot(a, b, preferred_element_type=F32)


def _rms_scale(x):
    return lax.rsqrt(jnp.mean(x * x, axis=-1, keepdims=True) + EPS)


def _params(n_axes):
    return pltpu.CompilerParams(
        dimension_semantics=("arbitrary",) * n_axes, vmem_limit_bytes=VMEM_LIMIT)


def _inproj_prologue(x_ref, g1_ref, wgate_ref, bgate_ref, gt_ref, xn_ref):
    x = x_ref[...]
    xn = (x * _rms_scale(x) * g1_ref[...]).astype(BF16)
    xn_ref[...] = xn
    gates = _dot(xn, wgate_ref[...]) + bgate_ref[...]
    gt_ref[...] = gates.T[0:N_GATES, :]


def _inproj_kernel(x_ref, g1_ref, w_ref, wgate_ref, bgate_ref, proj_ref, gt_ref, xn_ref):
    @pl.when(pl.program_id(1) == 0)
    def _():
        _inproj_prologue(x_ref, g1_ref, wgate_ref, bgate_ref, gt_ref, xn_ref)

    proj_ref[...] = _dot(xn_ref[...], w_ref[...]).astype(BF16)


def _inproj(x, g1, w_main, w_gate, b_gate, *, tm, tn):
    n = x.shape[0]
    return pl.pallas_call(
        _inproj_kernel,
        grid=(n // tm, D_PROJ // tn),
        in_specs=[
            pl.BlockSpec((tm, D_MODEL), lambda i, j: (i, 0)),
            pl.BlockSpec((1, D_MODEL), lambda i, j: (0, 0)),
            pl.BlockSpec((D_MODEL, tn), lambda i, j: (0, j)),
            pl.BlockSpec((D_MODEL, LANES), lambda i, j: (0, 0)),
            pl.BlockSpec((1, LANES), lambda i, j: (0, 0)),
        ],
        out_specs=[
            pl.BlockSpec((tm, tn), lambda i, j: (i, j)),
            pl.BlockSpec((N_GATES, tm), lambda i, j: (0, i)),
        ],
        out_shape=[
            jax.ShapeDtypeStruct((n, D_PROJ), BF16),
            jax.ShapeDtypeStruct((N_GATES, n), F32),
        ],
        scratch_shapes=[pltpu.VMEM((tm, D_MODEL), BF16)],
        compiler_params=_params(2),
        name="inproj",
    )(x, g1, w_main, w_gate, b_gate)


def _lane_scan(x, combine, identity):
    width = x.shape[1]
    lane = lax.broadcasted_iota(jnp.int32, x.shape, 1)
    shift = 1
    while shift < width:
        shifted = jnp.where(lane >= shift, pltpu.roll(x, shift, axis=1), identity)
        x = combine(x, shifted)
        shift *= 2
    return x


def _mlstm_phases(q_ref, k_ref, v_ref, o_ref, gates, gain_ref, mix_ref, c_ref, n_ref, m_ref, L):
    b = _lane_scan(jax.nn.log_sigmoid(gates), jnp.add, 0.0)
    b = pltpu.roll(b, N_HEADS, axis=0)
    u = gates - b
    m_prev = m_ref[0][:, 0:1]
    big_m = jnp.maximum(m_prev, _lane_scan(u, jnp.maximum, -jnp.inf))
    a = jnp.exp(m_prev - big_m)
    e = jnp.exp(-(b + big_m))
    m_last = big_m[:, L - 1:L]
    g = jnp.exp(u - m_last)
    dec = a[:, L - 1:L]
    m_new = b[:, L - 1:L] + m_last
    rows = jnp.concatenate(
        [big_m, a, e, g, jnp.zeros((LANES - 4 * SUBLANES, gates.shape[1]), F32)], axis=0)
    cols = rows.T

    t_idx = lax.broadcasted_iota(jnp.int32, (L, L), 0)
    s_idx = lax.broadcasted_iota(jnp.int32, (L, L), 1)
    causal = s_idx <= t_idx

    for h in range(N_HEADS):
        sl = slice(h * HEAD_DIM, (h + 1) * HEAD_DIM)
        qh = q_ref[:, sl]
        kh = k_ref[:, sl] * jnp.asarray(K_SCALE, BF16)
        vh = v_ref[:, sl]
        m_col = cols[:L, h:h + 1]
        a_col = cols[:L, SUBLANES + h:SUBLANES + h + 1]
        e_col = cols[:L, 2 * SUBLANES + h:2 * SUBLANES + h + 1]
        g_col = cols[:L, 3 * SUBLANES + h:3 * SUBLANES + h + 1]
        u_row = u[h:h + 1, :L]

        s = lax.dot_general(qh, kh, _NT, preferred_element_type=F32)
        yield
        sw = s * jnp.exp(jnp.where(causal, u_row - m_col, -jnp.inf))
        c_h = c_ref[0, h]
        n_h = n_ref[0, h:h + 1, :]
        num = a_col * _dot(qh, c_h.astype(BF16)) + _dot(sw.astype(BF16), vh)
        qn = jnp.sum(qh.astype(F32) * n_h, axis=-1, keepdims=True)
        den = a_col * qn + jnp.sum(sw, axis=-1, keepdims=True)
        hh = num * (1.0 / jnp.maximum(jnp.abs(den), e_col))
        y = hh * _rms_scale(hh) * gain_ref[:, sl] * jax.nn.sigmoid(o_ref[:, sl].astype(F32))
        mix_ref[:, sl] = y.astype(BF16)

        kg = kh.astype(F32) * g_col
        dec_h = dec[h:h + 1, :]
        c_ref[0, h] = dec_h * c_h + lax.dot_general(
            kg.astype(BF16), vh, _TN, preferred_element_type=F32)
        n_ref[0, h:h + 1, :] = dec_h * n_h + jnp.sum(kg, axis=0, keepdims=True)
        if h == N_HEADS - 1:
            m_ref[0] = jnp.broadcast_to(m_new, (SUBLANES, LANES))
        yield


def _run(*phase_generators):
    live = list(phase_generators)
    while live:
        live = [g for g in live if next(g, StopIteration) is not StopIteration]


def _seq_views(s, rows, token_refs, state_refs):
    return ([ref.at[pl.ds(s * rows, rows), :] for ref in token_refs],
            [ref.at[pl.ds(s, 1)] for ref in state_refs])


def _mlstm_state_specs(index_map3, index_map4, group=1):
    return [
        pl.BlockSpec((group, N_HEADS, HEAD_DIM, HEAD_DIM), index_map4),
        pl.BlockSpec((group, N_HEADS, HEAD_DIM), index_map3),
        pl.BlockSpec((group, SUBLANES, LANES), index_map3),
    ]


def _mlstm_state_shapes(batch):
    return [
        jax.ShapeDtypeStruct((batch, N_HEADS, HEAD_DIM, HEAD_DIM), F32),
        jax.ShapeDtypeStruct((batch, N_HEADS, HEAD_DIM), F32),
        jax.ShapeDtypeStruct((batch, SUBLANES, LANES), F32),
    ]


def _mlstm_kernel(q_ref, k_ref, v_ref, o_ref, gt_ref, gain_ref, c0_ref, n0_ref, m0_ref,
                  mix_ref, c_ref, n_ref, m_ref, *, chunk, group):
    c_ref[...] = c0_ref[...]
    n_ref[...] = n0_ref[...]
    m_ref[...] = m0_ref[...]
    phases = []
    for s in range(group):
        (q, k, v, o, mix), (c, n, m) = _seq_views(
            s, chunk, (q_ref, k_ref, v_ref, o_ref, mix_ref), (c_ref, n_ref, m_ref))
        phases.append(_mlstm_phases(q, k, v, o, gt_ref[s], gain_ref, mix, c, n, m, chunk))
    _run(*phases)


def _mlstm_single_chunk(proj, gt3, gain, init, *, batch, seq, group):
    tok = lambda col: pl.BlockSpec((group * seq, D_GROUP), lambda b, col=col: (b, col))
    state_specs = _mlstm_state_specs(lambda b: (b, 0, 0), lambda b: (b, 0, 0, 0), group)
    return pl.pallas_call(
        functools.partial(_mlstm_kernel, chunk=seq, group=group),
        grid=(batch // group,),
        in_specs=[tok(0), tok(1), tok(2), tok(3),
                  pl.BlockSpec((group, SUBLANES, gt3.shape[2]), lambda b: (b, 0, 0)),
                  pl.BlockSpec((1, D_GROUP), lambda b: (0, 0))] + state_specs,
        out_specs=[pl.BlockSpec((group * seq, D_GROUP), lambda b: (b, 0))] + state_specs,
        out_shape=[jax.ShapeDtypeStruct((batch * seq, D_GROUP), BF16)] + _mlstm_state_shapes(batch),
        compiler_params=_params(1),
        name="mlstm",
    )(proj, proj, proj, proj, gt3, gain, *init)


def _ret_phases(lg_ref, q_ref, k_ref, v_ref, g_ref, cos_ref, sin_ref, gain_ref, mix_ref, r_ref, L):
    half = HEAD_DIM // 2
    cos = cos_ref[...]
    sin = sin_ref[...]
    cos_k = cos * K_SCALE
    sin_k = sin * K_SCALE
    t_idx = lax.broadcasted_iota(jnp.int32, (L, L), 0)
    s_idx = lax.broadcasted_iota(jnp.int32, (L, L), 1)
    diff = t_idx - s_idx
    diff_pos = jnp.maximum(diff, 0).astype(F32)
    t_col = lax.broadcasted_iota(jnp.int32, (L, 1), 0).astype(F32)

    for h in range(N_HEADS):
        lo = slice(h * HEAD_DIM, h * HEAD_DIM + half)
        hi = slice(h * HEAD_DIM + half, (h + 1) * HEAD_DIM)
        sl = slice(h * HEAD_DIM, (h + 1) * HEAD_DIM)
        lg = lg_ref[h]
        q1 = q_ref[:, lo].astype(F32)
        q2 = q_ref[:, hi].astype(F32)
        k1 = k_ref[:, lo].astype(F32)
        k2 = k_ref[:, hi].astype(F32)
        qr = jnp.concatenate([q1 * cos - q2 * sin, q2 * cos + q1 * sin], axis=-1).astype(BF16)
        kr = jnp.concatenate([k1 * cos_k - k2 * sin_k, k2 * cos_k + k1 * sin_k], axis=-1)
        vh = v_ref[:, sl]

        dmask = jnp.where(diff >= 0, jnp.exp(lg * diff_pos), 0.0)
        inter = jnp.exp(lg * (t_col + 1.0))
        kdec = jnp.exp(lg * (L - 1.0 - t_col))
        cdec = jnp.exp(lg * jnp.full((1, 1), L, F32))

        s = lax.dot_general(qr, kr.astype(BF16), _NT, preferred_element_type=F32)
        yield
        r_h = r_ref[0, h]
        o = _dot((s * dmask).astype(BF16), vh) + inter * _dot(qr, r_h.astype(BF16))
        r_ref[0, h] = cdec * r_h + lax.dot_general(
            (kr * kdec).astype(BF16), vh, _TN, preferred_element_type=F32)

        oc = o - jnp.mean(o, axis=-1, keepdims=True)
        y = oc * _rms_scale(oc) * gain_ref[:, sl] * jax.nn.silu(g_ref[:, sl].astype(F32))
        mix_ref[:, sl] = y.astype(BF16)
        yield


def _ret_kernel(lg_ref, q_ref, k_ref, v_ref, g_ref, cos_ref, sin_ref, gain_ref, r0_ref,
                mix_ref, r_ref, *, chunk, group):
    r_ref[...] = r0_ref[...]
    phases = []
    for s in range(group):
        (q, k, v, g, mix), (r,) = _seq_views(
            s, chunk, (q_ref, k_ref, v_ref, g_ref, mix_ref), (r_ref,))
        phases.append(_ret_phases(lg_ref, q, k, v, g, cos_ref, sin_ref, gain_ref, mix, r, chunk))
    _run(*phases)


def _retention_single_chunk(proj, log_gamma, cos, sin, gain, init, *, batch, seq, group):
    tok = lambda col: pl.BlockSpec((group * seq, D_GROUP), lambda b, col=col: (b, col))
    state_spec = pl.BlockSpec((group, N_HEADS, HEAD_DIM, HEAD_DIM), lambda b: (b, 0, 0, 0))
    rope_spec = pl.BlockSpec((seq, HEAD_DIM // 2), lambda b: (0, 0))
    return pl.pallas_call(
        functools.partial(_ret_kernel, chunk=seq, group=group),
        grid=(batch // group,),
        in_specs=[pl.BlockSpec(memory_space=pltpu.SMEM),
                  tok(4), tok(5), tok(6), tok(7), rope_spec, rope_spec,
                  pl.BlockSpec((1, D_GROUP), lambda b: (0, 0)), state_spec],
        out_specs=[pl.BlockSpec((group * seq, D_GROUP), lambda b: (b, 0)), state_spec],
        out_shape=[
            jax.ShapeDtypeStruct((batch * seq, D_GROUP), BF16),
            jax.ShapeDtypeStruct((batch, N_HEADS, HEAD_DIM, HEAD_DIM), F32),
        ],
        compiler_params=_params(1),
        name="retention",
    )(log_gamma, proj, proj, proj, proj, cos, sin, gain, init)


def _prompt_kernel(lg_ref, x_ref, g1_ref, w_ref, wgate_ref, bgate_ref, gain_m_ref, cos_ref,
                   sin_ref, gain_r_ref, mixm_ref, c_ref, n_ref, m_ref, mixr_ref, r_ref,
                   xn_scr, proj_scr, gt_scr, *, n_chunks, chunks_per_seq, chunk):
    t = pl.program_id(0)
    piece = D_PROJ // N_HEADS

    def inproj_pieces(slot):
        x = x_ref[...]
        xn_scr[...] = (x * _rms_scale(x) * g1_ref[...]).astype(BF16)
        gates = _dot(xn_scr[...], wgate_ref[...]) + bgate_ref[...]
        gt_scr[slot] = gates.T[0:N_GATES, :]
        for p in range(N_HEADS):
            cols = pl.ds(p * piece, piece)
            proj_scr[slot, :, cols] = _dot(xn_scr[...], w_ref[:, cols]).astype(BF16)
            yield

    def mixers(slot):
        group = lambda k: proj_scr.at[slot, :, pl.ds(k * D_GROUP, D_GROUP)]
        return [
            _mlstm_phases(group(0), group(1), group(2), group(3), gt_scr[slot], gain_m_ref,
                          mixm_ref, c_ref, n_ref, m_ref, chunk),
            _ret_phases(lg_ref, group(4), group(5), group(6), group(7), cos_ref, sin_ref,
                        gain_r_ref, mixr_ref, r_ref, chunk),
        ]

    @pl.when((t >= 1) & ((t - 1) % chunks_per_seq == 0))
    def _():
        c_ref[...] = jnp.zeros_like(c_ref)
        n_ref[...] = jnp.zeros_like(n_ref)
        m_ref[...] = jnp.zeros_like(m_ref)
        r_ref[...] = jnp.zeros_like(r_ref)

    @pl.when(t == 0)
    def _():
        _run(inproj_pieces(0))

    def fused_step(slot):
        pieces = inproj_pieces(slot)
        heads = mixers(1 - slot)
        for _ in range(N_HEADS):
            for phases in heads:
                next(phases)
            next(pieces)
            for phases in heads:
                next(phases)

    for slot in range(2):
        pl.when((t >= 1) & (t < n_chunks) & (t % 2 == slot))(functools.partial(fused_step, slot))

    @pl.when(t == n_chunks)
    def _():
        _run(*mixers((n_chunks - 1) % 2))


def _prompt_front(x2d, w, rope, *, batch, seq):
    chunk = PROMPT_CHUNK
    chunks_per_seq = seq // chunk
    n_chunks = batch * chunks_per_seq
    cos, sin = rope
    const = lambda shape, **kw: pl.BlockSpec(shape, lambda t: (0,) * len(shape), **kw)
    mixed = lambda t: jnp.maximum(t - 1, 0)
    rope_spec = pl.BlockSpec((chunk, HEAD_DIM // 2), lambda t: (mixed(t) % chunks_per_seq, 0))
    mix_spec = pl.BlockSpec((chunk, D_GROUP), lambda t: (mixed(t), 0))
    mix_shape = jax.ShapeDtypeStruct((batch * seq, D_GROUP), BF16)
    seq_of = lambda t: mixed(t) // chunks_per_seq
    state4 = pl.BlockSpec((1, N_HEADS, HEAD_DIM, HEAD_DIM), lambda t: (seq_of(t), 0, 0, 0))
    state_specs = _mlstm_state_specs(lambda t: (seq_of(t), 0, 0), lambda t: (seq_of(t), 0, 0, 0))
    return pl.pallas_call(
        functools.partial(_prompt_kernel, n_chunks=n_chunks, chunks_per_seq=chunks_per_seq,
                          chunk=chunk),
        grid=(n_chunks + 1,),
        in_specs=[
            pl.BlockSpec(memory_space=pltpu.SMEM),
            pl.BlockSpec((chunk, D_MODEL), lambda t: (jnp.minimum(t, n_chunks - 1), 0)),
            const((1, D_MODEL)),
            const((D_MODEL, D_PROJ), pipeline_mode=pl.Buffered(1)),
            const((D_MODEL, LANES), pipeline_mode=pl.Buffered(1)),
            const((1, LANES)),
            const((1, D_GROUP)),
            rope_spec, rope_spec,
            const((1, D_GROUP)),
        ],
        out_specs=[mix_spec] + state_specs + [mix_spec, state4],
        out_shape=([mix_shape] + _mlstm_state_shapes(batch)
                   + [mix_shape, jax.ShapeDtypeStruct((batch, N_HEADS, HEAD_DIM, HEAD_DIM), F32)]),
        scratch_shapes=[
            pltpu.VMEM((chunk, D_MODEL), BF16),
            pltpu.VMEM((2, chunk, D_PROJ), BF16),
            pltpu.VMEM((2, N_GATES, chunk), F32),
        ],
        compiler_params=_params(1),
        name="prompt_front",
    )(w["log_gamma"], x2d, w["g1"], w["w_in"], w["w_gatecols"], w["b_gate"], w["gain_m"],
      cos, sin, w["gain_r"])


def _outproj_kernel(x_ref, mm_ref, mr_ref, wm_ref, wr_ref, g2_ref, x1_ref, hn_ref):
    x1 = x_ref[...] + _dot(mm_ref[...], wm_ref[...]) + _dot(mr_ref[...], wr_ref[...])
    x1_ref[...] = x1
    hn_ref[...] = (x1 * _rms_scale(x1) * g2_ref[...]).astype(BF16)


def _outproj(x, mix_m, mix_r, w_out, g2, *, tm):
    n = x.shape[0]
    w_spec = lambda half: pl.BlockSpec((D_GROUP, D_MODEL), lambda i, half=half: (half, 0),
                                       pipeline_mode=pl.Buffered(1))
    return pl.pallas_call(
        _outproj_kernel,
        grid=(n // tm,),
        in_specs=[
            pl.BlockSpec((tm, D_MODEL), lambda i: (i, 0)),
            pl.BlockSpec((tm, D_GROUP), lambda i: (i, 0)),
            pl.BlockSpec((tm, D_GROUP), lambda i: (i, 0)),
            w_spec(0), w_spec(1),
            pl.BlockSpec((1, D_MODEL), lambda i: (0, 0)),
        ],
        out_specs=[
            pl.BlockSpec((tm, D_MODEL), lambda i: (i, 0)),
            pl.BlockSpec((tm, D_MODEL), lambda i: (i, 0)),
        ],
        out_shape=[
            jax.ShapeDtypeStruct((n, D_MODEL), F32),
            jax.ShapeDtypeStruct((n, D_MODEL), BF16),
        ],
        compiler_params=_params(1),
        name="outproj",
    )(x, mix_m, mix_r, w_out, w_out, g2)


def _ffn_kernel(hn_ref, x1_hbm, wg_ref, wu_ref, wd_ref, gf_ref, y_ref, *rest):
    *w16_refs, x1_buf, x1_sem = rest
    i = pl.program_id(0)
    f = pl.program_id(1)
    last = pl.num_programs(1) - 1
    tm = x1_buf.shape[0]

    def x1_copy(tile):
        return pltpu.make_async_copy(x1_hbm.at[pl.ds(tile * tm, tm), :], x1_buf, x1_sem)

    @pl.when((f == 0) & (i == 0))
    def _():
        x1_copy(0).start()

    @pl.when(f == 0)
    def _():
        x1_copy(i).wait()

    @pl.when((f == 1) & (i + 1 < pl.num_programs(0)))
    def _():
        x1_copy(i + 1).start()

    def step(first, final):
        weights = [ref[...].astype(BF16) for ref in (wg_ref, wu_ref, wd_ref)]
        for w16_ref, w16 in zip(w16_refs, weights):
            w16_ref[...] = w16
        wg, wu, wd = weights
        hn = hn_ref[...]
        part = _dot((jax.nn.silu(_dot(hn, wg)) * _dot(hn, wu)).astype(BF16), wd)
        if first:
            y_ref[...] = x1_buf[...] + part
        else:
            y_ref[...] += part
        if final:
            x2 = y_ref[...]
            y_ref[...] = x2 * _rms_scale(x2) * gf_ref[...]

    pl.when(f == 0)(functools.partial(step, True, False))
    pl.when((f > 0) & (f < last))(functools.partial(step, False, False))
    pl.when(f == last)(functools.partial(step, False, True))


def _ffn(hn, x1, w_gate, w_up, w_down, g_final, *, tm, tf):
    n = hn.shape[0]
    emit = w_gate.dtype != BF16
    assert not emit or n == tm
    assert D_FF // tf >= 3
    w_specs = [
        pl.BlockSpec((D_MODEL, tf), lambda i, f: (0, f)),
        pl.BlockSpec((D_MODEL, tf), lambda i, f: (0, f)),
        pl.BlockSpec((tf, D_MODEL), lambda i, f: (f, 0)),
    ]
    w_shapes = [jax.ShapeDtypeStruct(wt.shape, BF16) for wt in (w_gate, w_up, w_down)]
    return pl.pallas_call(
        _ffn_kernel,
        grid=(n // tm, D_FF // tf),
        in_specs=[
            pl.BlockSpec((tm, D_MODEL), lambda i, f: (i, 0)),
            pl.BlockSpec(memory_space=pl.ANY),
            *w_specs,
            pl.BlockSpec((1, D_MODEL), lambda i, f: (0, 0)),
        ],
        out_specs=[pl.BlockSpec((tm, D_MODEL), lambda i, f: (i, 0))] + w_specs * emit,
        out_shape=[jax.ShapeDtypeStruct((n, D_MODEL), F32)] + w_shapes * emit,
        scratch_shapes=[pltpu.VMEM((tm, D_MODEL), F32), pltpu.SemaphoreType.DMA(())],
        compiler_params=_params(2),
        name="ffn",
    )(hn, x1, w_gate, w_up, w_down, g_final)


def _rope_tables(pos):
    half = HEAD_DIM // 2
    freqs = ROPE_BASE ** (-jnp.arange(half, dtype=F32) / half)
    ang = pos[:, None] * freqs[None, :]
    return jnp.cos(ang), jnp.sin(ang)


def _tail(x2d, mix_m, mix_r, w, *, tf):
    x1, hn = _outproj(x2d, mix_m, mix_r, w["w_out"], w["g2"], tm=512)
    return _ffn(hn, x1, w["w_gate"], w["w_up"], w["w_down"], w["g_final"], tm=1024, tf=tf)


def _prompt_trunk(x, w):
    batch, seq, _ = x.shape
    x2d = x.reshape(batch * seq, D_MODEL)
    rope = _rope_tables(jnp.arange(seq, dtype=F32))
    mix_m, c_new, n_new, m_new, mix_r, r_new = _prompt_front(x2d, w, rope, batch=batch, seq=seq)
    (y,) = _tail(x2d, mix_m, mix_r, w, tf=512)
    return (y.reshape(batch, seq, D_MODEL), c_new[None], n_new[None],
            m_new[:, :N_HEADS, 0][None], r_new[None])


def _sample_trunk(x, init, w):
    batch, seq, _ = x.shape
    x2d = x.reshape(batch * seq, D_MODEL)
    proj, gt = _inproj(x2d, w["g1"], w["w_in"], w["w_gatecols"], w["b_gate"],
                       tm=INPROJ_TM, tn=INPROJ_TN)
    gt3 = gt.reshape(N_GATES, batch, seq).transpose(1, 0, 2)
    gt3 = jnp.pad(gt3, ((0, 0), (0, 0), (0, max(0, LANES - seq))))
    c0, n0, m0, r0 = init
    m0 = jnp.pad(m0, ((0, 0), (0, SUBLANES - N_HEADS)))
    m0 = jnp.broadcast_to(m0[:, :, None], (batch, SUBLANES, LANES))
    mix_m, c_new, n_new, m_new = _mlstm_single_chunk(proj, gt3, w["gain_m"], (c0, n0, m0),
                                                     batch=batch, seq=seq, group=SAMPLE_GROUP)
    cos, sin = _rope_tables(PAST_LEN + jnp.arange(seq, dtype=F32))
    mix_r, r_new = _retention_single_chunk(proj, w["log_gamma"], cos, sin, w["gain_r"], r0,
                                           batch=batch, seq=seq, group=SAMPLE_GROUP)
    y, w_gate16, w_up16, w_down16 = _tail(x2d, mix_m, mix_r, w, tf=256)
    w16 = dict(w_gate=w_gate16, w_up=w_up16, w_down=w_down16)
    return (y.reshape(batch, seq, D_MODEL), c_new[None], n_new[None],
            m_new[:, :N_HEADS, 0][None], r_new[None]), w16


def kernel(x_prompt, x_sample, state_mlstm_C, state_mlstm_n, state_mlstm_m, state_ret,
           g_norm1, w_in, b_gates, g_mlstm_norm, g_ret_norm, w_out, g_norm2,
           w_gate, w_up, w_down, g_final):
    w_in0 = w_in[0].astype(BF16)
    w = dict(
        g1=g_norm1[0].reshape(1, D_MODEL),
        w_in=w_in0,
        w_gatecols=jnp.pad(w_in0[:, D_PROJ:], ((0, 0), (0, LANES - N_GATES))),
        b_gate=jnp.pad(b_gates[0], (0, LANES - N_GATES)).reshape(1, LANES),
        gain_m=g_mlstm_norm[0].reshape(1, D_GROUP),
        gain_r=g_ret_norm[0].reshape(1, D_GROUP),
        w_out=w_out[0].astype(BF16),
        g2=g_norm2[0].reshape(1, D_MODEL),
        w_gate=w_gate[0],
        w_up=w_up[0],
        w_down=w_down[0],
        g_final=g_final.reshape(1, D_MODEL),
        log_gamma=jnp.log(1.0 - jnp.exp2(-5.0 - jnp.arange(N_HEADS, dtype=F32))),
    )
    init_s = (state_mlstm_C[0], state_mlstm_n[0], state_mlstm_m[0], state_ret[0])
    (y_s, c_s, n_s, m_s, r_s), w16 = _sample_trunk(x_sample, init_s, w)
    y_p, c_p, n_p, m_p, r_p = _prompt_trunk(x_prompt, {**w, **w16})
    return (y_p, y_s, c_p, n_p, m_p, r_p, c_s, n_s, m_s, r_s)
```

```python
import functools

import jax
import jax.numpy as jnp
from jax import lax
from jax.experimental import pallas as pl
from jax.experimental.pallas import tpu as pltpu

F32 = jnp.float32
BF16 = jnp.bfloat16

D_MODEL = 2048
N_HEADS = 4
HEAD_DIM = 256
D_GROUP = N_HEADS * HEAD_DIM
D_PROJ = 8 * D_GROUP
N_GATES = 2 * N_HEADS
D_FF = 5632
ROPE_BASE = 10000.0
PAST_LEN = 4096
EPS = 1e-6
K_SCALE = HEAD_DIM ** -0.5

LANES = 128
SUBLANES = 8
VMEM_LIMIT = 60 * 1024 * 1024

PROMPT_CHUNK = 256
INPROJ_TM = 1024
INPROJ_TN = 2048
SAMPLE_GROUP = 4
FFN_GRANULE = 256

_NT = (((1,), (1,)), ((), ()))
_TN = (((0,), (0,)), ((), ()))


def _dot(a, b):
    return jnp.dot(a, b, preferred_element_type=F32)


def _rms_scale(x):
    return lax.rsqrt(jnp.mean(x * x, axis=-1, keepdims=True) + EPS)


def _params(n_axes):
    return pltpu.CompilerParams(
        dimension_semantics=("arbitrary",) * n_axes, vmem_limit_bytes=VMEM_LIMIT)


def _inproj_prologue(x_ref, g1_ref, wgate_ref, bgate_ref, gt_ref, xn_ref):
    x = x_ref[...]
    xn = (x * _rms_scale(x) * g1_ref[...]).astype(BF16)
    xn_ref[...] = xn
    gates = _dot(xn, wgate_ref[...]) + bgate_ref[...]
    gt_ref[...] = gates.T[0:N_GATES, :]


def _inproj_kernel(x_ref, g1_ref, w_ref, wgate_ref, bgate_ref, proj_ref, gt_ref, xn_ref):
    @pl.when(pl.program_id(1) == 0)
    def _():
        _inproj_prologue(x_ref, g1_ref, wgate_ref, bgate_ref, gt_ref, xn_ref)

    proj_ref[...] = _dot(xn_ref[...], w_ref[...]).astype(BF16)


def _inproj(x, g1, w_main, w_gate, b_gate, *, tm, tn):
    n = x.shape[0]
    return pl.pallas_call(
        _inproj_kernel,
        grid=(n // tm, D_PROJ // tn),
        in_specs=[
            pl.BlockSpec((tm, D_MODEL), lambda i, j: (i, 0)),
            pl.BlockSpec((1, D_MODEL), lambda i, j: (0, 0)),
            pl.BlockSpec((D_MODEL, tn), lambda i, j: (0, j)),
            pl.BlockSpec((D_MODEL, LANES), lambda i, j: (0, 0)),
            pl.BlockSpec((1, LANES), lambda i, j: (0, 0)),
        ],
        out_specs=[
            pl.BlockSpec((tm, tn), lambda i, j: (i, j)),
            pl.BlockSpec((N_GATES, tm), lambda i, j: (0, i)),
        ],
        out_shape=[
            jax.ShapeDtypeStruct((n, D_PROJ), BF16),
            jax.ShapeDtypeStruct((N_GATES, n), F32),
        ],
        scratch_shapes=[pltpu.VMEM((tm, D_MODEL), BF16)],
        compiler_params=_params(2),
        name="inproj",
    )(x, g1, w_main, w_gate, b_gate)


def _lane_scan(x, combine, identity):
    width = x.shape[1]
    lane = lax.broadcasted_iota(jnp.int32, x.shape, 1)
    shift = 1
    while shift < width:
        shifted = jnp.where(lane >= shift, pltpu.roll(x, shift, axis=1), identity)
        x = combine(x, shifted)
        shift *= 2
    return x


def _mlstm_phases(q_ref, k_ref, v_ref, o_ref, gates, gain_ref, mix_ref, c_ref, n_ref, m_ref, L):
    b = _lane_scan(jax.nn.log_sigmoid(gates), jnp.add, 0.0)
    b = pltpu.roll(b, N_HEADS, axis=0)
    u = gates - b
    m_prev = m_ref[0][:, 0:1]
    big_m = jnp.maximum(m_prev, _lane_scan(u, jnp.maximum, -jnp.inf))
    a = jnp.exp(m_prev - big_m)
    e = jnp.exp(-(b + big_m))
    m_last = big_m[:, L - 1:L]
    g = jnp.exp(u - m_last)
    dec = a[:, L - 1:L]
    m_new = b[:, L - 1:L] + m_last
    rows = jnp.concatenate(
        [big_m, a, e, g, jnp.zeros((LANES - 4 * SUBLANES, gates.shape[1]), F32)], axis=0)
    cols = rows.T

    t_idx = lax.broadcasted_iota(jnp.int32, (L, L), 0)
    s_idx = lax.broadcasted_iota(jnp.int32, (L, L), 1)
    causal = s_idx <= t_idx

    for h in range(N_HEADS):
        sl = slice(h * HEAD_DIM, (h + 1) * HEAD_DIM)
        qh = q_ref[:, sl]
        kh = k_ref[:, sl] * jnp.asarray(K_SCALE, BF16)
        vh = v_ref[:, sl]
        m_col = cols[:L, h:h + 1]
        a_col = cols[:L, SUBLANES + h:SUBLANES + h + 1]
        e_col = cols[:L, 2 * SUBLANES + h:2 * SUBLANES + h + 1]
        g_col = cols[:L, 3 * SUBLANES + h:3 * SUBLANES + h + 1]
        u_row = u[h:h + 1, :L]

        s = lax.dot_general(qh, kh, _NT, preferred_element_type=F32)
        yield
        sw = s * jnp.exp(jnp.where(causal, u_row - m_col, -jnp.inf))
        c_h = c_ref[0, h]
        n_h = n_ref[0, h:h + 1, :]
        num = a_col * _dot(qh, c_h.astype(BF16)) + _dot(sw.astype(BF16), vh)
        qn = jnp.sum(qh.astype(F32) * n_h, axis=-1, keepdims=True)
        den = a_col * qn + jnp.sum(sw, axis=-1, keepdims=True)
        hh = num * (1.0 / jnp.maximum(jnp.abs(den), e_col))
        y = hh * _rms_scale(hh) * gain_ref[:, sl] * jax.nn.sigmoid(o_ref[:, sl].astype(F32))
        mix_ref[:, sl] = y.astype(BF16)

        kg = kh.astype(F32) * g_col
        dec_h = dec[h:h + 1, :]
        c_ref[0, h] = dec_h * c_h + lax.dot_general(
            kg.astype(BF16), vh, _TN, preferred_element_type=F32)
        n_ref[0, h:h + 1, :] = dec_h * n_h + jnp.sum(kg, axis=0, keepdims=True)
        if h == N_HEADS - 1:
            m_ref[0] = jnp.broadcast_to(m_new, (SUBLANES, LANES))
        yield


def _run(*phase_generators):
    live = list(phase_generators)
    while live:
        live = [g for g in live if next(g, StopIteration) is not StopIteration]


def _seq_views(s, rows, token_refs, state_refs):
    return ([ref.at[pl.ds(s * rows, rows), :] for ref in token_refs],
            [ref.at[pl.ds(s, 1)] for ref in state_refs])


def _mlstm_state_specs(index_map3, index_map4, group=1):
    return [
        pl.BlockSpec((group, N_HEADS, HEAD_DIM, HEAD_DIM), index_map4),
        pl.BlockSpec((group, N_HEADS, HEAD_DIM), index_map3),
        pl.BlockSpec((group, SUBLANES, LANES), index_map3),
    ]


def _mlstm_state_shapes(batch):
    return [
        jax.ShapeDtypeStruct((batch, N_HEADS, HEAD_DIM, HEAD_DIM), F32),
        jax.ShapeDtypeStruct((batch, N_HEADS, HEAD_DIM), F32),
        jax.ShapeDtypeStruct((batch, SUBLANES, LANES), F32),
    ]


def _mlstm_kernel(q_ref, k_ref, v_ref, o_ref, gt_ref, gain_ref, c0_ref, n0_ref, m0_ref,
                  mix_ref, c_ref, n_ref, m_ref, *, chunk, group):
    c_ref[...] = c0_ref[...]
    n_ref[...] = n0_ref[...]
    m_ref[...] = m0_ref[...]
    phases = []
    for s in range(group):
        (q, k, v, o, mix), (c, n, m) = _seq_views(
            s, chunk, (q_ref, k_ref, v_ref, o_ref, mix_ref), (c_ref, n_ref, m_ref))
        phases.append(_mlstm_phases(q, k, v, o, gt_ref[s], gain_ref, mix, c, n, m, chunk))
    _run(*phases)


def _mlstm_single_chunk(proj, gt3, gain, init, *, batch, seq, group):
    tok = lambda col: pl.BlockSpec((group * seq, D_GROUP), lambda b, col=col: (b, col))
    state_specs = _mlstm_state_specs(lambda b: (b, 0, 0), lambda b: (b, 0, 0, 0), group)
    return pl.pallas_call(
        functools.partial(_mlstm_kernel, chunk=seq, group=group),
        grid=(batch // group,),
        in_specs=[tok(0), tok(1), tok(2), tok(3),
                  pl.BlockSpec((group, SUBLANES, gt3.shape[2]), lambda b: (b, 0, 0)),
                  pl.BlockSpec((1, D_GROUP), lambda b: (0, 0))] + state_specs,
        out_specs=[pl.BlockSpec((group * seq, D_GROUP), lambda b: (b, 0))] + state_specs,
        out_shape=[jax.ShapeDtypeStruct((batch * seq, D_GROUP), BF16)] + _mlstm_state_shapes(batch),
        compiler_params=_params(1),
        name="mlstm",
    )(proj, proj, proj, proj, gt3, gain, *init)


def _ret_phases(lg_ref, q_ref, k_ref, v_ref, g_ref, cos_ref, sin_ref, gain_ref, mix_ref, r_ref, L):
    half = HEAD_DIM // 2
    cos = cos_ref[...]
    sin = sin_ref[...]
    cos_k = cos * K_SCALE
    sin_k = sin * K_SCALE
    t_idx = lax.broadcasted_iota(jnp.int32, (L, L), 0)
    s_idx = lax.broadcasted_iota(jnp.int32, (L, L), 1)
    diff = t_idx - s_idx
    diff_pos = jnp.maximum(diff, 0).astype(F32)
    t_col = lax.broadcasted_iota(jnp.int32, (L, 1), 0).astype(F32)

    for h in range(N_HEADS):
        lo = slice(h * HEAD_DIM, h * HEAD_DIM + half)
        hi = slice(h * HEAD_DIM + half, (h + 1) * HEAD_DIM)
        sl = slice(h * HEAD_DIM, (h + 1) * HEAD_DIM)
        lg = lg_ref[h]
        q1 = q_ref[:, lo].astype(F32)
        q2 = q_ref[:, hi].astype(F32)
        k1 = k_ref[:, lo].astype(F32)
        k2 = k_ref[:, hi].astype(F32)
        qr = jnp.concatenate([q1 * cos - q2 * sin, q2 * cos + q1 * sin], axis=-1).astype(BF16)
        kr = jnp.concatenate([k1 * cos_k - k2 * sin_k, k2 * cos_k + k1 * sin_k], axis=-1)
        vh = v_ref[:, sl]

        dmask = jnp.where(diff >= 0, jnp.exp(lg * diff_pos), 0.0)
        inter = jnp.exp(lg * (t_col + 1.0))
        kdec = jnp.exp(lg * (L - 1.0 - t_col))
        cdec = jnp.exp(lg * jnp.full((1, 1), L, F32))

        s = lax.dot_general(qr, kr.astype(BF16), _NT, preferred_element_type=F32)
        yield
        r_h = r_ref[0, h]
        o = _dot((s * dmask).astype(BF16), vh) + inter * _dot(qr, r_h.astype(BF16))
        r_ref[0, h] = cdec * r_h + lax.dot_general(
            (kr * kdec).astype(BF16), vh, _TN, preferred_element_type=F32)

        oc = o - jnp.mean(o, axis=-1, keepdims=True)
        y = oc * _rms_scale(oc) * gain_ref[:, sl] * jax.nn.silu(g_ref[:, sl].astype(F32))
        mix_ref[:, sl] = y.astype(BF16)
        yield


def _ret_kernel(lg_ref, q_ref, k_ref, v_ref, g_ref, cos_ref, sin_ref, gain_ref, r0_ref,
                mix_ref, r_ref, *, chunk, group):
    r_ref[...] = r0_ref[...]
    phases = []
    for s in range(group):
        (q, k, v, g, mix), (r,) = _seq_views(
            s, chunk, (q_ref, k_ref, v_ref, g_ref, mix_ref), (r_ref,))
        phases.append(_ret_phases(lg_ref, q, k, v, g, cos_ref, sin_ref, gain_ref, mix, r, chunk))
    _run(*phases)


def _retention_single_chunk(proj, log_gamma, cos, sin, gain, init, *, batch, seq, group):
    tok = lambda col: pl.BlockSpec((group * seq, D_GROUP), lambda b, col=col: (b, col))
    state_spec = pl.BlockSpec((group, N_HEADS, HEAD_DIM, HEAD_DIM), lambda b: (b, 0, 0, 0))
    rope_spec = pl.BlockSpec((seq, HEAD_DIM // 2), lambda b: (0, 0))
    return pl.pallas_call(
        functools.partial(_ret_kernel, chunk=seq, group=group),
        grid=(batch // group,),
        in_specs=[pl.BlockSpec(memory_space=pltpu.SMEM),
                  tok(4), tok(5), tok(6), tok(7), rope_spec, rope_spec,
                  pl.BlockSpec((1, D_GROUP), lambda b: (0, 0)), state_spec],
        out_specs=[pl.BlockSpec((group * seq, D_GROUP), lambda b: (b, 0)), state_spec],
        out_shape=[
            jax.ShapeDtypeStruct((batch * seq, D_GROUP), BF16),
            jax.ShapeDtypeStruct((batch, N_HEADS, HEAD_DIM, HEAD_DIM), F32),
        ],
        compiler_params=_params(1),
        name="retention",
    )(log_gamma, proj, proj, proj, proj, cos, sin, gain, init)


def _prompt_kernel(lg_ref, x_ref, g1_ref, w_ref, wgate_ref, bgate_ref, gain_m_ref, cos_ref,
                   sin_ref, gain_r_ref, mixm_ref, c_ref, n_ref, m_ref, mixr_ref, r_ref,
                   xn_scr, proj_scr, gt_scr, *, n_chunks, chunks_per_seq, chunk):
    t = pl.program_id(0)
    piece = D_PROJ // N_HEADS

    def inproj_pieces(slot):
        x = x_ref[...]
        xn_scr[...] = (x * _rms_scale(x) * g1_ref[...]).astype(BF16)
        gates = _dot(xn_scr[...], wgate_ref[...]) + bgate_ref[...]
        gt_scr[slot] = gates.T[0:N_GATES, :]
        for p in range(N_HEADS):
            cols = pl.ds(p * piece, piece)
            proj_scr[slot, :, cols] = _dot(xn_scr[...], w_ref[:, cols]).astype(BF16)
            yield

    def mixers(slot):
        group = lambda k: proj_scr.at[slot, :, pl.ds(k * D_GROUP, D_GROUP)]
        return [
            _mlstm_phases(group(0), group(1), group(2), group(3), gt_scr[slot], gain_m_ref,
                          mixm_ref, c_ref, n_ref, m_ref, chunk),
            _ret_phases(lg_ref, group(4), group(5), group(6), group(7), cos_ref, sin_ref,
                        gain_r_ref, mixr_ref, r_ref, chunk),
        ]

    @pl.when((t >= 1) & ((t - 1) % chunks_per_seq == 0))
    def _():
        c_ref[...] = jnp.zeros_like(c_ref)
        n_ref[...] = jnp.zeros_like(n_ref)
        m_ref[...] = jnp.zeros_like(m_ref)
        r_ref[...] = jnp.zeros_like(r_ref)

    @pl.when(t == 0)
    def _():
        _run(inproj_pieces(0))

    @pl.when((t >= 1) & (t < n_chunks))
    def _():
        slot = t % 2
        pieces = inproj_pieces(slot)
        heads = mixers(1 - slot)
        for _ in range(N_HEADS):
            for phases in heads:
                next(phases)
            next(pieces)
            for phases in heads:
                next(phases)

    @pl.when(t == n_chunks)
    def _():
        _run(*mixers((n_chunks - 1) % 2))


def _prompt_front(x2d, w, rope, *, batch, seq):
    chunk = PROMPT_CHUNK
    chunks_per_seq = seq // chunk
    n_chunks = batch * chunks_per_seq
    cos, sin = rope
    const = lambda shape, **kw: pl.BlockSpec(shape, lambda t: (0,) * len(shape), **kw)
    mixed = lambda t: jnp.maximum(t - 1, 0)
    rope_spec = pl.BlockSpec((chunk, HEAD_DIM // 2), lambda t: (mixed(t) % chunks_per_seq, 0))
    mix_spec = pl.BlockSpec((chunk, D_GROUP), lambda t: (mixed(t), 0))
    mix_shape = jax.ShapeDtypeStruct((batch * seq, D_GROUP), BF16)
    seq_of = lambda t: mixed(t) // chunks_per_seq
    state4 = pl.BlockSpec((1, N_HEADS, HEAD_DIM, HEAD_DIM), lambda t: (seq_of(t), 0, 0, 0))
    state_specs = _mlstm_state_specs(lambda t: (seq_of(t), 0, 0), lambda t: (seq_of(t), 0, 0, 0))
    return pl.pallas_call(
        functools.partial(_prompt_kernel, n_chunks=n_chunks, chunks_per_seq=chunks_per_seq,
                          chunk=chunk),
        grid=(n_chunks + 1,),
        in_specs=[
            pl.BlockSpec(memory_space=pltpu.SMEM),
            pl.BlockSpec((chunk, D_MODEL), lambda t: (jnp.minimum(t, n_chunks - 1), 0)),
            const((1, D_MODEL)),
            const((D_MODEL, D_PROJ), pipeline_mode=pl.Buffered(1)),
            const((D_MODEL, LANES), pipeline_mode=pl.Buffered(1)),
            const((1, LANES)),
            const((1, D_GROUP)),
            rope_spec, rope_spec,
            const((1, D_GROUP)),
        ],
        out_specs=[mix_spec] + state_specs + [mix_spec, state4],
        out_shape=([mix_shape] + _mlstm_state_shapes(batch)
                   + [mix_shape, jax.ShapeDtypeStruct((batch, N_HEADS, HEAD_DIM, HEAD_DIM), F32)]),
        scratch_shapes=[
            pltpu.VMEM((chunk, D_MODEL), BF16),
            pltpu.VMEM((2, chunk, D_PROJ), BF16),
            pltpu.VMEM((2, N_GATES, chunk), F32),
        ],
        compiler_params=_params(1),
        name="prompt_front",
    )(w["log_gamma"], x2d, w["g1"], w["w_in"], w["w_gatecols"], w["b_gate"], w["gain_m"],
      cos, sin, w["gain_r"])


def _outproj_kernel(x_ref, mm_ref, mr_ref, wm_ref, wr_ref, g2_ref, x1_ref, hn_ref):
    x1 = x_ref[...] + _dot(mm_ref[...], wm_ref[...]) + _dot(mr_ref[...], wr_ref[...])
    x1_ref[...] = x1
    hn_ref[...] = (x1 * _rms_scale(x1) * g2_ref[...]).astype(BF16)


def _outproj(x, mix_m, mix_r, w_out, g2, *, tm):
    n = x.shape[0]
    w_spec = lambda half: pl.BlockSpec((D_GROUP, D_MODEL), lambda i, half=half: (half, 0),
                                       pipeline_mode=pl.Buffered(1))
    return pl.pallas_call(
        _outproj_kernel,
        grid=(n // tm,),
        in_specs=[
            pl.BlockSpec((tm, D_MODEL), lambda i: (i, 0)),
            pl.BlockSpec((tm, D_GROUP), lambda i: (i, 0)),
            pl.BlockSpec((tm, D_GROUP), lambda i: (i, 0)),
            w_spec(0), w_spec(1),
            pl.BlockSpec((1, D_MODEL), lambda i: (0, 0)),
        ],
        out_specs=[
            pl.BlockSpec((tm, D_MODEL), lambda i: (i, 0)),
            pl.BlockSpec((tm, D_MODEL), lambda i: (i, 0)),
        ],
        out_shape=[
            jax.ShapeDtypeStruct((n, D_MODEL), F32),
            jax.ShapeDtypeStruct((n, D_MODEL), BF16),
        ],
        compiler_params=_params(1),
        name="outproj",
    )(x, mix_m, mix_r, w_out, w_out, g2)


def _swiglu_down(hn, w_gu, w_down):
    gu = _dot(hn, w_gu)
    acts = []
    for base in range(0, w_gu.shape[1], 2 * FFN_GRANULE):
        gate = gu[:, base:base + FFN_GRANULE]
        up = gu[:, base + FFN_GRANULE:base + 2 * FFN_GRANULE]
        acts.append((jax.nn.silu(gate) * up).astype(BF16))
    act = acts[0] if len(acts) == 1 else jnp.concatenate(acts, axis=-1)
    return _dot(act, w_down)


def _ffn_kernel(*refs, pack):
    if pack:
        (hn_ref, x1_hbm, wg32_ref, wu32_ref, wd32_ref, gf_ref,
         y_ref, wgu_ref, wd_ref, x1_buf, x1_sem) = refs
    else:
        hn_ref, x1_hbm, wgu_ref, wd_ref, gf_ref, y_ref, x1_buf, x1_sem = refs
    i = pl.program_id(0)
    f = pl.program_id(1)
    last = pl.num_programs(1) - 1
    tm = x1_buf.shape[0]

    def x1_copy(tile):
        return pltpu.make_async_copy(x1_hbm.at[pl.ds(tile * tm, tm), :], x1_buf, x1_sem)

    @pl.when((f == 0) & (i == 0))
    def _():
        x1_copy(0).start()

    @pl.when(f == 0)
    def _():
        x1_copy(i).wait()

    @pl.when((f == 1) & (i + 1 < pl.num_programs(0)))
    def _():
        x1_copy(i + 1).start()

    def step(first, final):
        if pack:
            wgu_ref[:, :FFN_GRANULE] = wg32_ref[...].astype(BF16)
            wgu_ref[:, FFN_GRANULE:] = wu32_ref[...].astype(BF16)
            wd_ref[...] = wd32_ref[...].astype(BF16)
        part = _swiglu_down(hn_ref[...], wgu_ref[...], wd_ref[...])
        if first:
            y_ref[...] = x1_buf[...] + part
        else:
            y_ref[...] += part
        if final:
            x2 = y_ref[...]
            y_ref[...] = x2 * _rms_scale(x2) * gf_ref[...]

    pl.when(f == 0)(functools.partial(step, True, False))
    pl.when((f > 0) & (f < last))(functools.partial(step, False, False))
    pl.when(f == last)(functools.partial(step, False, True))


def _ffn(hn, x1, w, *, tm, tf):
    n = hn.shape[0]
    pack = "w_gu" not in w
    assert not pack or (n == tm and tf == FFN_GRANULE)
    assert D_FF // tf >= 3
    gu_spec = pl.BlockSpec((D_MODEL, 2 * tf), lambda i, f: (0, f))
    down_spec = pl.BlockSpec((tf, D_MODEL), lambda i, f: (f, 0))
    if pack:
        col_spec = pl.BlockSpec((D_MODEL, tf), lambda i, f: (0, f))
        w_specs, w_args = [col_spec, col_spec, down_spec], [w["w_gate"], w["w_up"], w["w_down"]]
    else:
        w_specs, w_args = [gu_spec, down_spec], [w["w_gu"], w["w_down"]]
    return pl.pallas_call(
        functools.partial(_ffn_kernel, pack=pack),
        grid=(n // tm, D_FF // tf),
        in_specs=[
            pl.BlockSpec((tm, D_MODEL), lambda i, f: (i, 0)),
            pl.BlockSpec(memory_space=pl.ANY),
            *w_specs,
            pl.BlockSpec((1, D_MODEL), lambda i, f: (0, 0)),
        ],
        out_specs=([pl.BlockSpec((tm, D_MODEL), lambda i, f: (i, 0))]
                   + [gu_spec, down_spec] * pack),
        out_shape=([jax.ShapeDtypeStruct((n, D_MODEL), F32)]
                   + [jax.ShapeDtypeStruct((D_MODEL, 2 * D_FF), BF16),
                      jax.ShapeDtypeStruct((D_FF, D_MODEL), BF16)] * pack),
        scratch_shapes=[pltpu.VMEM((tm, D_MODEL), F32), pltpu.SemaphoreType.DMA(())],
        compiler_params=_params(2),
        name="ffn",
    )(hn, x1, *w_args, w["g_final"])


def _rope_tables(pos):
    half = HEAD_DIM // 2
    freqs = ROPE_BASE ** (-jnp.arange(half, dtype=F32) / half)
    ang = pos[:, None] * freqs[None, :]
    return jnp.cos(ang), jnp.sin(ang)


def _tail(x2d, mix_m, mix_r, w, *, tf):
    x1, hn = _outproj(x2d, mix_m, mix_r, w["w_out"], w["g2"], tm=512)
    return _ffn(hn, x1, w, tm=1024, tf=tf)


def _prompt_trunk(x, w):
    batch, seq, _ = x.shape
    x2d = x.reshape(batch * seq, D_MODEL)
    rope = _rope_tables(jnp.arange(seq, dtype=F32))
    mix_m, c_new, n_new, m_new, mix_r, r_new = _prompt_front(x2d, w, rope, batch=batch, seq=seq)
    (y,) = _tail(x2d, mix_m, mix_r, w, tf=512)
    return (y.reshape(batch, seq, D_MODEL), c_new[None], n_new[None],
            m_new[:, :N_HEADS, 0][None], r_new[None])


def _sample_trunk(x, init, w):
    batch, seq, _ = x.shape
    x2d = x.reshape(batch * seq, D_MODEL)
    proj, gt = _inproj(x2d, w["g1"], w["w_in"], w["w_gatecols"], w["b_gate"],
                       tm=INPROJ_TM, tn=INPROJ_TN)
    gt3 = gt.reshape(N_GATES, batch, seq).transpose(1, 0, 2)
    gt3 = jnp.pad(gt3, ((0, 0), (0, 0), (0, max(0, LANES - seq))))
    c0, n0, m0, r0 = init
    m0 = jnp.pad(m0, ((0, 0), (0, SUBLANES - N_HEADS)))
    m0 = jnp.broadcast_to(m0[:, :, None], (batch, SUBLANES, LANES))
    mix_m, c_new, n_new, m_new = _mlstm_single_chunk(proj, gt3, w["gain_m"], (c0, n0, m0),
                                                     batch=batch, seq=seq, group=SAMPLE_GROUP)
    cos, sin = _rope_tables(PAST_LEN + jnp.arange(seq, dtype=F32))
    mix_r, r_new = _retention_single_chunk(proj, w["log_gamma"], cos, sin, w["gain_r"], r0,
                                           batch=batch, seq=seq, group=SAMPLE_GROUP)
    y, w_gu16, w_down16 = _tail(x2d, mix_m, mix_r, w, tf=FFN_GRANULE)
    w16 = dict(w_gu=w_gu16, w_down=w_down16)
    return (y.reshape(batch, seq, D_MODEL), c_new[None], n_new[None],
            m_new[:, :N_HEADS, 0][None], r_new[None]), w16


def kernel(x_prompt, x_sample, state_mlstm_C, state_mlstm_n, state_mlstm_m, state_ret,
           g_norm1, w_in, b_gates, g_mlstm_norm, g_ret_norm, w_out, g_norm2,
           w_gate, w_up, w_down, g_final):
    w_in0 = w_in[0].astype(BF16)
    w = dict(
        g1=g_norm1[0].reshape(1, D_MODEL),
        w_in=w_in0,
        w_gatecols=jnp.pad(w_in0[:, D_PROJ:], ((0, 0), (0, LANES - N_GATES))),
        b_gate=jnp.pad(b_gates[0], (0, LANES - N_GATES)).reshape(1, LANES),
        gain_m=g_mlstm_norm[0].reshape(1, D_GROUP),
        gain_r=g_ret_norm[0].reshape(1, D_GROUP),
        w_out=w_out[0].astype(BF16),
        g2=g_norm2[0].reshape(1, D_MODEL),
        w_gate=w_gate[0],
        w_up=w_up[0],
        w_down=w_down[0],
        g_final=g_final.reshape(1, D_MODEL),
        log_gamma=jnp.log(1.0 - jnp.exp2(-5.0 - jnp.arange(N_HEADS, dtype=F32))),
    )
    init_s = (state_mlstm_C[0], state_mlstm_n[0], state_mlstm_m[0], state_ret[0])
    (y_s, c_s, n_s, m_s, r_s), w16 = _sample_trunk(x_sample, init_s, w)
    y_p, c_p, n_p, m_p, r_p = _prompt_trunk(x_prompt, {**w, **w16})
    return (y_p, y_s, c_p, n_p, m_p, r_p, c_s, n_s, m_s, r_s)
```

```python
import functools

import jax
import jax.numpy as jnp
from jax import lax
from jax.experimental import pallas as pl
from jax.experimental.pallas import tpu as pltpu

F32 = jnp.float32
BF16 = jnp.bfloat16

D_MODEL = 2048
N_HEADS = 4
HEAD_DIM = 256
D_GROUP = N_HEADS * HEAD_DIM
D_PROJ = 8 * D_GROUP
N_GATES = 2 * N_HEADS
D_FF = 5632
ROPE_BASE = 10000.0
PAST_LEN = 4096
EPS = 1e-6
K_SCALE = HEAD_DIM ** -0.5

LANES = 128
SUBLANES = 8
VMEM_LIMIT = 60 * 1024 * 1024

PROMPT_CHUNK = 256
INPROJ_TM = 1024
INPROJ_TN = 2048
SAMPLE_GROUP = 4
OUTPROJ_ROWS = 256
FFN_GRANULE = 256

_NT = (((1,), (1,)), ((), ()))
_TN = (((0,), (0,)), ((), ()))


def _dot(a, b):
    return jnp.dot(a, b, preferred_element_type=F32)


def _rms_scale(x):
    return lax.rsqrt(jnp.mean(x * x, axis=-1, keepdims=True) + EPS)


def _params(n_axes):
    return pltpu.CompilerParams(
        dimension_semantics=("arbitrary",) * n_axes, vmem_limit_bytes=VMEM_LIMIT)


def _inproj_prologue(x_ref, g1_ref, wgate_ref, bgate_ref, gt_ref, xn_ref):
    x = x_ref[...]
    xn = (x * _rms_scale(x) * g1_ref[...]).astype(BF16)
    xn_ref[...] = xn
    gates = _dot(xn, wgate_ref[...]) + bgate_ref[...]
    gt_ref[...] = gates.T[0:N_GATES, :]


def _inproj_kernel(x_ref, g1_ref, w_ref, wgate_ref, bgate_ref, proj_ref, gt_ref, xn_ref):
    @pl.when(pl.program_id(1) == 0)
    def _():
        _inproj_prologue(x_ref, g1_ref, wgate_ref, bgate_ref, gt_ref, xn_ref)

    proj_ref[...] = _dot(xn_ref[...], w_ref[...]).astype(BF16)


def _inproj(x, g1, w_main, w_gate, b_gate, *, tm, tn):
    n = x.shape[0]
    return pl.pallas_call(
        _inproj_kernel,
        grid=(n // tm, D_PROJ // tn),
        in_specs=[
            pl.BlockSpec((tm, D_MODEL), lambda i, j: (i, 0)),
            pl.BlockSpec((1, D_MODEL), lambda i, j: (0, 0)),
            pl.BlockSpec((D_MODEL, tn), lambda i, j: (0, j)),
            pl.BlockSpec((D_MODEL, LANES), lambda i, j: (0, 0)),
            pl.BlockSpec((1, LANES), lambda i, j: (0, 0)),
        ],
        out_specs=[
            pl.BlockSpec((tm, tn), lambda i, j: (i, j)),
            pl.BlockSpec((N_GATES, tm), lambda i, j: (0, i)),
        ],
        out_shape=[
            jax.ShapeDtypeStruct((n, D_PROJ), BF16),
            jax.ShapeDtypeStruct((N_GATES, n), F32),
        ],
        scratch_shapes=[pltpu.VMEM((tm, D_MODEL), BF16)],
        compiler_params=_params(2),
        name="inproj",
    )(x, g1, w_main, w_gate, b_gate)


def _lane_scan(x, combine, identity):
    width = x.shape[1]
    lane = lax.broadcasted_iota(jnp.int32, x.shape, 1)
    shift = 1
    while shift < width:
        shifted = jnp.where(lane >= shift, pltpu.roll(x, shift, axis=1), identity)
        x = combine(x, shifted)
        shift *= 2
    return x


def _mlstm_phases(q_ref, k_ref, v_ref, o_ref, gates, gain_ref, mix_ref, c_ref, n_ref, m_ref, L):
    b = _lane_scan(jax.nn.log_sigmoid(gates), jnp.add, 0.0)
    b = pltpu.roll(b, N_HEADS, axis=0)
    u = gates - b
    m_prev = m_ref[0][:, 0:1]
    big_m = jnp.maximum(m_prev, _lane_scan(u, jnp.maximum, -jnp.inf))
    a = jnp.exp(m_prev - big_m)
    e = jnp.exp(-(b + big_m))
    m_last = big_m[:, L - 1:L]
    g = jnp.exp(u - m_last)
    dec = a[:, L - 1:L]
    m_new = b[:, L - 1:L] + m_last
    rows = jnp.concatenate(
        [big_m, a, e, g, jnp.zeros((LANES - 4 * SUBLANES, gates.shape[1]), F32)], axis=0)
    cols = rows.T

    t_idx = lax.broadcasted_iota(jnp.int32, (L, L), 0)
    s_idx = lax.broadcasted_iota(jnp.int32, (L, L), 1)
    causal = s_idx <= t_idx

    for h in range(N_HEADS):
        sl = slice(h * HEAD_DIM, (h + 1) * HEAD_DIM)
        qh = q_ref[:, sl]
        kh = k_ref[:, sl] * jnp.asarray(K_SCALE, BF16)
        vh = v_ref[:, sl]
        m_col = cols[:L, h:h + 1]
        a_col = cols[:L, SUBLANES + h:SUBLANES + h + 1]
        e_col = cols[:L, 2 * SUBLANES + h:2 * SUBLANES + h + 1]
        g_col = cols[:L, 3 * SUBLANES + h:3 * SUBLANES + h + 1]
        u_row = u[h:h + 1, :L]

        s = lax.dot_general(qh, kh, _NT, preferred_element_type=F32)
        yield
        sw = s * jnp.exp(jnp.where(causal, u_row - m_col, -jnp.inf))
        c_h = c_ref[0, h]
        n_h = n_ref[0, h:h + 1, :]
        num = a_col * _dot(qh, c_h.astype(BF16)) + _dot(sw.astype(BF16), vh)
        qn = jnp.sum(qh.astype(F32) * n_h, axis=-1, keepdims=True)
        den = a_col * qn + jnp.sum(sw, axis=-1, keepdims=True)
        hh = num * (1.0 / jnp.maximum(jnp.abs(den), e_col))
        y = hh * _rms_scale(hh) * gain_ref[:, sl] * jax.nn.sigmoid(o_ref[:, sl].astype(F32))
        mix_ref[:, sl] = y.astype(BF16)

        kg = kh.astype(F32) * g_col
        dec_h = dec[h:h + 1, :]
        c_ref[0, h] = dec_h * c_h + lax.dot_general(
            kg.astype(BF16), vh, _TN, preferred_element_type=F32)
        n_ref[0, h:h + 1, :] = dec_h * n_h + jnp.sum(kg, axis=0, keepdims=True)
        if h == N_HEADS - 1:
            m_ref[0] = jnp.broadcast_to(m_new, (SUBLANES, LANES))
        yield


def _run(*phase_generators):
    live = list(phase_generators)
    while live:
        live = [g for g in live if next(g, StopIteration) is not StopIteration]


def _seq_views(s, rows, token_refs, state_refs):
    return ([ref.at[pl.ds(s * rows, rows), :] for ref in token_refs],
            [ref.at[pl.ds(s, 1)] for ref in state_refs])


def _mlstm_state_specs(index_map3, index_map4, group=1):
    return [
        pl.BlockSpec((group, N_HEADS, HEAD_DIM, HEAD_DIM), index_map4),
        pl.BlockSpec((group, N_HEADS, HEAD_DIM), index_map3),
        pl.BlockSpec((group, SUBLANES, LANES), index_map3),
    ]


def _mlstm_state_shapes(batch):
    return [
        jax.ShapeDtypeStruct((batch, N_HEADS, HEAD_DIM, HEAD_DIM), F32),
        jax.ShapeDtypeStruct((batch, N_HEADS, HEAD_DIM), F32),
        jax.ShapeDtypeStruct((batch, SUBLANES, LANES), F32),
    ]


def _mlstm_kernel(q_ref, k_ref, v_ref, o_ref, gt_ref, gain_ref, c0_ref, n0_ref, m0_ref,
                  mix_ref, c_ref, n_ref, m_ref, *, chunk, group):
    c_ref[...] = c0_ref[...]
    n_ref[...] = n0_ref[...]
    m_ref[...] = m0_ref[...]
    phases = []
    for s in range(group):
        (q, k, v, o, mix), (c, n, m) = _seq_views(
            s, chunk, (q_ref, k_ref, v_ref, o_ref, mix_ref), (c_ref, n_ref, m_ref))
        phases.append(_mlstm_phases(q, k, v, o, gt_ref[s], gain_ref, mix, c, n, m, chunk))
    _run(*phases)


def _mlstm_single_chunk(proj, gt3, gain, init, *, batch, seq, group):
    tok = lambda col: pl.BlockSpec((group * seq, D_GROUP), lambda b, col=col: (b, col))
    state_specs = _mlstm_state_specs(lambda b: (b, 0, 0), lambda b: (b, 0, 0, 0), group)
    return pl.pallas_call(
        functools.partial(_mlstm_kernel, chunk=seq, group=group),
        grid=(batch // group,),
        in_specs=[tok(0), tok(1), tok(2), tok(3),
                  pl.BlockSpec((group, SUBLANES, gt3.shape[2]), lambda b: (b, 0, 0)),
                  pl.BlockSpec((1, D_GROUP), lambda b: (0, 0))] + state_specs,
        out_specs=[pl.BlockSpec((group * seq, D_GROUP), lambda b: (b, 0))] + state_specs,
        out_shape=[jax.ShapeDtypeStruct((batch * seq, D_GROUP), BF16)] + _mlstm_state_shapes(batch),
        compiler_params=_params(1),
        name="mlstm",
    )(proj, proj, proj, proj, gt3, gain, *init)


def _ret_phases(lg_ref, q_ref, k_ref, v_ref, g_ref, cos_ref, sin_ref, gain_ref, mix_ref, r_ref, L):
    half = HEAD_DIM // 2
    cos = cos_ref[...]
    sin = sin_ref[...]
    cos_k = cos * K_SCALE
    sin_k = sin * K_SCALE
    t_idx = lax.broadcasted_iota(jnp.int32, (L, L), 0)
    s_idx = lax.broadcasted_iota(jnp.int32, (L, L), 1)
    diff = t_idx - s_idx
    diff_pos = jnp.maximum(diff, 0).astype(F32)
    t_col = lax.broadcasted_iota(jnp.int32, (L, 1), 0).astype(F32)

    for h in range(N_HEADS):
        lo = slice(h * HEAD_DIM, h * HEAD_DIM + half)
        hi = slice(h * HEAD_DIM + half, (h + 1) * HEAD_DIM)
        sl = slice(h * HEAD_DIM, (h + 1) * HEAD_DIM)
        lg = lg_ref[h]
        q1 = q_ref[:, lo].astype(F32)
        q2 = q_ref[:, hi].astype(F32)
        k1 = k_ref[:, lo].astype(F32)
        k2 = k_ref[:, hi].astype(F32)
        qr = jnp.concatenate([q1 * cos - q2 * sin, q2 * cos + q1 * sin], axis=-1).astype(BF16)
        kr = jnp.concatenate([k1 * cos_k - k2 * sin_k, k2 * cos_k + k1 * sin_k], axis=-1)
        vh = v_ref[:, sl]

        dmask = jnp.where(diff >= 0, jnp.exp(lg * diff_pos), 0.0)
        inter = jnp.exp(lg * (t_col + 1.0))
        kdec = jnp.exp(lg * (L - 1.0 - t_col))
        cdec = jnp.exp(lg * jnp.full((1, 1), L, F32))

        s = lax.dot_general(qr, kr.astype(BF16), _NT, preferred_element_type=F32)
        yield
        r_h = r_ref[0, h]
        o = _dot((s * dmask).astype(BF16), vh) + inter * _dot(qr, r_h.astype(BF16))
        r_ref[0, h] = cdec * r_h + lax.dot_general(
            (kr * kdec).astype(BF16), vh, _TN, preferred_element_type=F32)

        oc = o - jnp.mean(o, axis=-1, keepdims=True)
        y = oc * _rms_scale(oc) * gain_ref[:, sl] * jax.nn.silu(g_ref[:, sl].astype(F32))
        mix_ref[:, sl] = y.astype(BF16)
        yield


def _ret_kernel(lg_ref, q_ref, k_ref, v_ref, g_ref, cos_ref, sin_ref, gain_ref, r0_ref,
                mix_ref, r_ref, *, chunk, group):
    r_ref[...] = r0_ref[...]
    phases = []
    for s in range(group):
        (q, k, v, g, mix), (r,) = _seq_views(
            s, chunk, (q_ref, k_ref, v_ref, g_ref, mix_ref), (r_ref,))
        phases.append(_ret_phases(lg_ref, q, k, v, g, cos_ref, sin_ref, gain_ref, mix, r, chunk))
    _run(*phases)


def _retention_single_chunk(proj, log_gamma, cos, sin, gain, init, *, batch, seq, group):
    tok = lambda col: pl.BlockSpec((group * seq, D_GROUP), lambda b, col=col: (b, col))
    state_spec = pl.BlockSpec((group, N_HEADS, HEAD_DIM, HEAD_DIM), lambda b: (b, 0, 0, 0))
    rope_spec = pl.BlockSpec((seq, HEAD_DIM // 2), lambda b: (0, 0))
    return pl.pallas_call(
        functools.partial(_ret_kernel, chunk=seq, group=group),
        grid=(batch // group,),
        in_specs=[pl.BlockSpec(memory_space=pltpu.SMEM),
                  tok(4), tok(5), tok(6), tok(7), rope_spec, rope_spec,
                  pl.BlockSpec((1, D_GROUP), lambda b: (0, 0)), state_spec],
        out_specs=[pl.BlockSpec((group * seq, D_GROUP), lambda b: (b, 0)), state_spec],
        out_shape=[
            jax.ShapeDtypeStruct((batch * seq, D_GROUP), BF16),
            jax.ShapeDtypeStruct((batch, N_HEADS, HEAD_DIM, HEAD_DIM), F32),
        ],
        compiler_params=_params(1),
        name="retention",
    )(log_gamma, proj, proj, proj, proj, cos, sin, gain, init)


def _prompt_kernel(lg_ref, x_ref, g1_ref, w_hbm, wgate_ref, bgate_ref, gain_m_ref, cos_ref,
                   sin_ref, gain_r_ref, mixm_ref, c_ref, n_ref, m_ref, mixr_ref, r_ref,
                   w_scr, w_sems, xn_scr, proj_scr, gt_scr, *, n_chunks, chunks_per_seq, chunk):
    t = pl.program_id(0)
    piece = D_PROJ // N_HEADS

    def w_copy(p):
        cols = pl.ds(p * piece, piece)
        return pltpu.make_async_copy(w_hbm.at[:, cols], w_scr.at[:, cols], w_sems.at[p])

    def inproj_pieces(slot, first_step=False):
        if first_step:
            for p in range(N_HEADS):
                w_copy(p).start()
        x = x_ref[...]
        xn_scr[...] = (x * _rms_scale(x) * g1_ref[...]).astype(BF16)
        gates = _dot(xn_scr[...], wgate_ref[...]) + bgate_ref[...]
        gt_scr[slot] = gates.T[0:N_GATES, :]
        for p in range(N_HEADS):
            cols = pl.ds(p * piece, piece)
            if first_step:
                w_copy(p).wait()
            proj_scr[slot, :, cols] = _dot(xn_scr[...], w_scr[:, cols]).astype(BF16)
            yield

    def mixers(slot):
        group = lambda k: proj_scr.at[slot, :, pl.ds(k * D_GROUP, D_GROUP)]
        return [
            _mlstm_phases(group(0), group(1), group(2), group(3), gt_scr[slot], gain_m_ref,
                          mixm_ref, c_ref, n_ref, m_ref, chunk),
            _ret_phases(lg_ref, group(4), group(5), group(6), group(7), cos_ref, sin_ref,
                        gain_r_ref, mixr_ref, r_ref, chunk),
        ]

    @pl.when((t >= 1) & ((t - 1) % chunks_per_seq == 0))
    def _():
        c_ref[...] = jnp.zeros_like(c_ref)
        n_ref[...] = jnp.zeros_like(n_ref)
        m_ref[...] = jnp.zeros_like(m_ref)
        r_ref[...] = jnp.zeros_like(r_ref)

    @pl.when(t == 0)
    def _():
        _run(inproj_pieces(0, first_step=True))

    @pl.when((t >= 1) & (t < n_chunks))
    def _():
        slot = t % 2
        pieces = inproj_pieces(slot)
        heads = mixers(1 - slot)
        for _ in range(N_HEADS):
            for phases in heads:
                next(phases)
            next(pieces)
            for phases in heads:
                next(phases)

    @pl.when(t == n_chunks)
    def _():
        _run(*mixers((n_chunks - 1) % 2))


def _prompt_front(x2d, w, rope, *, batch, seq):
    chunk = PROMPT_CHUNK
    chunks_per_seq = seq // chunk
    n_chunks = batch * chunks_per_seq
    cos, sin = rope
    const = lambda shape, **kw: pl.BlockSpec(shape, lambda t: (0,) * len(shape), **kw)
    mixed = lambda t: jnp.maximum(t - 1, 0)
    rope_spec = pl.BlockSpec((chunk, HEAD_DIM // 2), lambda t: (mixed(t) % chunks_per_seq, 0))
    mix_spec = pl.BlockSpec((chunk, D_GROUP), lambda t: (mixed(t), 0))
    mix_shape = jax.ShapeDtypeStruct((batch * seq, D_GROUP), BF16)
    seq_of = lambda t: mixed(t) // chunks_per_seq
    state4 = pl.BlockSpec((1, N_HEADS, HEAD_DIM, HEAD_DIM), lambda t: (seq_of(t), 0, 0, 0))
    state_specs = _mlstm_state_specs(lambda t: (seq_of(t), 0, 0), lambda t: (seq_of(t), 0, 0, 0))
    return pl.pallas_call(
        functools.partial(_prompt_kernel, n_chunks=n_chunks, chunks_per_seq=chunks_per_seq,
                          chunk=chunk),
        grid=(n_chunks + 1,),
        in_specs=[
            pl.BlockSpec(memory_space=pltpu.SMEM),
            pl.BlockSpec((chunk, D_MODEL), lambda t: (jnp.minimum(t, n_chunks - 1), 0)),
            const((1, D_MODEL)),
            pl.BlockSpec(memory_space=pl.ANY),
            const((D_MODEL, LANES), pipeline_mode=pl.Buffered(1)),
            const((1, LANES)),
            const((1, D_GROUP)),
            rope_spec, rope_spec,
            const((1, D_GROUP)),
        ],
        out_specs=[mix_spec] + state_specs + [mix_spec, state4],
        out_shape=([mix_shape] + _mlstm_state_shapes(batch)
                   + [mix_shape, jax.ShapeDtypeStruct((batch, N_HEADS, HEAD_DIM, HEAD_DIM), F32)]),
        scratch_shapes=[
            pltpu.VMEM((D_MODEL, D_PROJ), BF16),
            pltpu.SemaphoreType.DMA((N_HEADS,)),
            pltpu.VMEM((chunk, D_MODEL), BF16),
            pltpu.VMEM((2, chunk, D_PROJ), BF16),
            pltpu.VMEM((2, N_GATES, chunk), F32),
        ],
        compiler_params=_params(1),
        name="prompt_front",
    )(w["log_gamma"], x2d, w["g1"], w["w_in"], w["w_gatecols"], w["b_gate"], w["gain_m"],
      cos, sin, w["gain_r"])


def _outproj_kernel(x_ref, mm_ref, mr_ref, wm_ref, wr_ref, g2_ref, x1_ref, hn_ref):
    for r in range(0, x_ref.shape[0], OUTPROJ_ROWS):
        rows = pl.ds(r, OUTPROJ_ROWS)
        x1 = (x_ref[rows, :] + _dot(mm_ref[rows, :], wm_ref[...])
              + _dot(mr_ref[rows, :], wr_ref[...]))
        x1_ref[rows, :] = x1
        hn_ref[rows, :] = (x1 * _rms_scale(x1) * g2_ref[...]).astype(BF16)


def _outproj(x, mix_m, mix_r, w_out, g2, *, tm):
    n = x.shape[0]
    w_spec = lambda half: pl.BlockSpec((D_GROUP, D_MODEL), lambda i, half=half: (half, 0),
                                       pipeline_mode=pl.Buffered(1))
    return pl.pallas_call(
        _outproj_kernel,
        grid=(n // tm,),
        in_specs=[
            pl.BlockSpec((tm, D_MODEL), lambda i: (i, 0)),
            pl.BlockSpec((tm, D_GROUP), lambda i: (i, 0)),
            pl.BlockSpec((tm, D_GROUP), lambda i: (i, 0)),
            w_spec(0), w_spec(1),
            pl.BlockSpec((1, D_MODEL), lambda i: (0, 0)),
        ],
        out_specs=[
            pl.BlockSpec((tm, D_MODEL), lambda i: (i, 0)),
            pl.BlockSpec((tm, D_MODEL), lambda i: (i, 0)),
        ],
        out_shape=[
            jax.ShapeDtypeStruct((n, D_MODEL), F32),
            jax.ShapeDtypeStruct((n, D_MODEL), BF16),
        ],
        compiler_params=_params(1),
        name="outproj",
    )(x, mix_m, mix_r, w_out, w_out, g2)


def _swiglu_down(hn, w_gu, w_down):
    gu = _dot(hn, w_gu)
    acts = []
    for base in range(0, w_gu.shape[1], 2 * FFN_GRANULE):
        gate = gu[:, base:base + FFN_GRANULE]
        up = gu[:, base + FFN_GRANULE:base + 2 * FFN_GRANULE]
        acts.append((jax.nn.silu(gate) * up).astype(BF16))
    act = acts[0] if len(acts) == 1 else jnp.concatenate(acts, axis=-1)
    return _dot(act, w_down)


def _ffn_kernel(*refs, pack):
    if pack:
        (hn_ref, x1_hbm, wg32_ref, wu32_ref, wd32_ref, gf_ref,
         y_ref, wgu_ref, wd_ref, x1_buf, x1_sem) = refs
    else:
        hn_ref, x1_hbm, wgu_ref, wd_ref, gf_ref, y_ref, x1_buf, x1_sem = refs
    i = pl.program_id(0)
    f = pl.program_id(1)
    last = pl.num_programs(1) - 1
    tm = x1_buf.shape[0]

    def x1_copy(tile):
        return pltpu.make_async_copy(x1_hbm.at[pl.ds(tile * tm, tm), :], x1_buf, x1_sem)

    @pl.when((f == 0) & (i == 0))
    def _():
        x1_copy(0).start()

    @pl.when(f == 0)
    def _():
        x1_copy(i).wait()

    @pl.when((f == 1) & (i + 1 < pl.num_programs(0)))
    def _():
        x1_copy(i + 1).start()

    def step(first, final):
        if pack:
            wgu_ref[:, :FFN_GRANULE] = wg32_ref[...].astype(BF16)
            wgu_ref[:, FFN_GRANULE:] = wu32_ref[...].astype(BF16)
            wd_ref[...] = wd32_ref[...].astype(BF16)
        part = _swiglu_down(hn_ref[...], wgu_ref[...], wd_ref[...])
        if first:
            y_ref[...] = x1_buf[...] + part
        else:
            y_ref[...] += part
        if final:
            x2 = y_ref[...]
            y_ref[...] = x2 * _rms_scale(x2) * gf_ref[...]

    pl.when(f == 0)(functools.partial(step, True, False))
    pl.when((f > 0) & (f < last))(functools.partial(step, False, False))
    pl.when(f == last)(functools.partial(step, False, True))


def _ffn(hn, x1, w, *, tm, tf):
    n = hn.shape[0]
    pack = "w_gu" not in w
    assert not pack or (n == tm and tf == FFN_GRANULE)
    assert D_FF // tf >= 3
    gu_spec = pl.BlockSpec((D_MODEL, 2 * tf), lambda i, f: (0, f))
    down_spec = pl.BlockSpec((tf, D_MODEL), lambda i, f: (f, 0))
    if pack:
        col_spec = pl.BlockSpec((D_MODEL, tf), lambda i, f: (0, f))
        w_specs, w_args = [col_spec, col_spec, down_spec], [w["w_gate"], w["w_up"], w["w_down"]]
    else:
        w_specs, w_args = [gu_spec, down_spec], [w["w_gu"], w["w_down"]]
    return pl.pallas_call(
        functools.partial(_ffn_kernel, pack=pack),
        grid=(n // tm, D_FF // tf),
        in_specs=[
            pl.BlockSpec((tm, D_MODEL), lambda i, f: (i, 0)),
            pl.BlockSpec(memory_space=pl.ANY),
            *w_specs,
            pl.BlockSpec((1, D_MODEL), lambda i, f: (0, 0)),
        ],
        out_specs=([pl.BlockSpec((tm, D_MODEL), lambda i, f: (i, 0))]
                   + [gu_spec, down_spec] * pack),
        out_shape=([jax.ShapeDtypeStruct((n, D_MODEL), F32)]
                   + [jax.ShapeDtypeStruct((D_MODEL, 2 * D_FF), BF16),
                      jax.ShapeDtypeStruct((D_FF, D_MODEL), BF16)] * pack),
        scratch_shapes=[pltpu.VMEM((tm, D_MODEL), F32), pltpu.SemaphoreType.DMA(())],
        compiler_params=_params(2),
        name="ffn",
    )(hn, x1, *w_args, w["g_final"])


def _rope_tables(pos):
    half = HEAD_DIM // 2
    freqs = ROPE_BASE ** (-jnp.arange(half, dtype=F32) / half)
    ang = pos[:, None] * freqs[None, :]
    return jnp.cos(ang), jnp.sin(ang)


def _tail(x2d, mix_m, mix_r, w, *, tf):
    x1, hn = _outproj(x2d, mix_m, mix_r, w["w_out"], w["g2"], tm=1024)
    return _ffn(hn, x1, w, tm=1024, tf=tf)


def _prompt_trunk(x, w):
    batch, seq, _ = x.shape
    x2d = x.reshape(batch * seq, D_MODEL)
    rope = _rope_tables(jnp.arange(seq, dtype=F32))
    mix_m, c_new, n_new, m_new, mix_r, r_new = _prompt_front(x2d, w, rope, batch=batch, seq=seq)
    (y,) = _tail(x2d, mix_m, mix_r, w, tf=512)
    return (y.reshape(batch, seq, D_MODEL), c_new[None], n_new[None],
            m_new[:, :N_HEADS, 0][None], r_new[None])


def _sample_trunk(x, init, w):
    batch, seq, _ = x.shape
    x2d = x.reshape(batch * seq, D_MODEL)
    proj, gt = _inproj(x2d, w["g1"], w["w_in"], w["w_gatecols"], w["b_gate"],
                       tm=INPROJ_TM, tn=INPROJ_TN)
    gt3 = gt.reshape(N_GATES, batch, seq).transpose(1, 0, 2)
    gt3 = jnp.pad(gt3, ((0, 0), (0, 0), (0, max(0, LANES - seq))))
    c0, n0, m0, r0 = init
    m0 = jnp.pad(m0, ((0, 0), (0, SUBLANES - N_HEADS)))
    m0 = jnp.broadcast_to(m0[:, :, None], (batch, SUBLANES, LANES))
    mix_m, c_new, n_new, m_new = _mlstm_single_chunk(proj, gt3, w["gain_m"], (c0, n0, m0),
                                                     batch=batch, seq=seq, group=SAMPLE_GROUP)
    cos, sin = _rope_tables(PAST_LEN + jnp.arange(seq, dtype=F32))
    mix_r, r_new = _retention_single_chunk(proj, w["log_gamma"], cos, sin, w["gain_r"], r0,
                                           batch=batch, seq=seq, group=SAMPLE_GROUP)
    y, w_gu16, w_down16 = _tail(x2d, mix_m, mix_r, w, tf=FFN_GRANULE)
    w16 = dict(w_gu=w_gu16, w_down=w_down16)
    return (y.reshape(batch, seq, D_MODEL), c_new[None], n_new[None],
            m_new[:, :N_HEADS, 0][None], r_new[None]), w16


def kernel(x_prompt, x_sample, state_mlstm_C, state_mlstm_n, state_mlstm_m, state_ret,
           g_norm1, w_in, b_gates, g_mlstm_norm, g_ret_norm, w_out, g_norm2,
           w_gate, w_up, w_down, g_final):
    w_in0 = w_in[0].astype(BF16)
    w = dict(
        g1=g_norm1[0].reshape(1, D_MODEL),
        w_in=w_in0,
        w_gatecols=jnp.pad(w_in0[:, D_PROJ:], ((0, 0), (0, LANES - N_GATES))),
        b_gate=jnp.pad(b_gates[0], (0, LANES - N_GATES)).reshape(1, LANES),
        gain_m=g_mlstm_norm[0].reshape(1, D_GROUP),
        gain_r=g_ret_norm[0].reshape(1, D_GROUP),
        w_out=w_out[0].astype(BF16),
        g2=g_norm2[0].reshape(1, D_MODEL),
        w_gate=w_gate[0],
        w_up=w_up[0],
        w_down=w_down[0],
        g_final=g_final.reshape(1, D_MODEL),
        log_gamma=jnp.log(1.0 - jnp.exp2(-5.0 - jnp.arange(N_HEADS, dtype=F32))),
    )
    init_s = (state_mlstm_C[0], state_mlstm_n[0], state_mlstm_m[0], state_ret[0])
    (y_s, c_s, n_s, m_s, r_s), w16 = _sample_trunk(x_sample, init_s, w)
    y_p, c_p, n_p, m_p, r_p = _prompt_trunk(x_prompt, {**w, **w16})
    return (y_p, y_s, c_p, n_p, m_p, r_p, c_s, n_s, m_s, r_s)
```

```python
import functools

import jax
import jax.numpy as jnp
from jax import lax
from jax.experimental import pallas as pl
from jax.experimental.pallas import tpu as pltpu

F32 = jnp.float32
BF16 = jnp.bfloat16

D_MODEL = 2048
N_HEADS = 4
HEAD_DIM = 256
D_GROUP = N_HEADS * HEAD_DIM
D_PROJ = 8 * D_GROUP
N_GATES = 2 * N_HEADS
D_FF = 5632
ROPE_BASE = 10000.0
PAST_LEN = 4096
EPS = 1e-6
K_SCALE = HEAD_DIM ** -0.5

LANES = 128
SUBLANES = 8
VMEM_LIMIT = 60 * 1024 * 1024

PROMPT_CHUNK = 256
INPROJ_TM = 1024
INPROJ_TN = 2048
SAMPLE_GROUP = 4
FFN_GRANULE = 256

_NT = (((1,), (1,)), ((), ()))
_TN = (((0,), (0,)), ((), ()))


def _dot(a, b):
    return jnp.dot(a, b, preferred_element_type=F32)


def _rms_scale(x):
    return lax.rsqrt(jnp.mean(x * x, axis=-1, keepdims=True) + EPS)


def _params(n_axes):
    return pltpu.CompilerParams(
        dimension_semantics=("arbitrary",) * n_axes, vmem_limit_bytes=VMEM_LIMIT)


def _inproj_prologue(x_ref, g1_ref, wgate_ref, bgate_ref, gt_ref, xn_ref):
    x = x_ref[...]
    xn = (x * _rms_scale(x) * g1_ref[...]).astype(BF16)
    xn_ref[...] = xn
    gates = _dot(xn, wgate_ref[...]) + bgate_ref[...]
    gt_ref[...] = gates.T[0:N_GATES, :]


def _inproj_kernel(x_ref, g1_ref, w_ref, wgate_ref, bgate_ref, proj_ref, gt_ref, xn_ref):
    @pl.when(pl.program_id(1) == 0)
    def _():
        _inproj_prologue(x_ref, g1_ref, wgate_ref, bgate_ref, gt_ref, xn_ref)

    proj_ref[...] = _dot(xn_ref[...], w_ref[...]).astype(BF16)


def _inproj(x, g1, w_main, w_gate, b_gate, *, tm, tn):
    n = x.shape[0]
    return pl.pallas_call(
        _inproj_kernel,
        grid=(n // tm, D_PROJ // tn),
        in_specs=[
            pl.BlockSpec((tm, D_MODEL), lambda i, j: (i, 0)),
            pl.BlockSpec((1, D_MODEL), lambda i, j: (0, 0)),
            pl.BlockSpec((D_MODEL, tn), lambda i, j: (0, j)),
            pl.BlockSpec((D_MODEL, LANES), lambda i, j: (0, 0)),
            pl.BlockSpec((1, LANES), lambda i, j: (0, 0)),
        ],
        out_specs=[
            pl.BlockSpec((tm, tn), lambda i, j: (i, j)),
            pl.BlockSpec((N_GATES, tm), lambda i, j: (0, i)),
        ],
        out_shape=[
            jax.ShapeDtypeStruct((n, D_PROJ), BF16),
            jax.ShapeDtypeStruct((N_GATES, n), F32),
        ],
        scratch_shapes=[pltpu.VMEM((tm, D_MODEL), BF16)],
        compiler_params=_params(2),
        name="inproj",
    )(x, g1, w_main, w_gate, b_gate)


def _lane_scan(x, combine, identity):
    width = x.shape[1]
    lane = lax.broadcasted_iota(jnp.int32, x.shape, 1)
    shift = 1
    while shift < width:
        shifted = jnp.where(lane >= shift, pltpu.roll(x, shift, axis=1), identity)
        x = combine(x, shifted)
        shift *= 2
    return x


def _mlstm_phases(q_ref, k_ref, v_ref, o_ref, gates, gain_ref, mix_ref, c_ref, n_ref, m_ref, L):
    b = _lane_scan(jax.nn.log_sigmoid(gates), jnp.add, 0.0)
    b = pltpu.roll(b, N_HEADS, axis=0)
    u = gates - b
    m_prev = m_ref[0][:, 0:1]
    big_m = jnp.maximum(m_prev, _lane_scan(u, jnp.maximum, -jnp.inf))
    a = jnp.exp(m_prev - big_m)
    e = jnp.exp(-(b + big_m))
    m_last = big_m[:, L - 1:L]
    g = jnp.exp(u - m_last)
    dec = a[:, L - 1:L]
    m_new = b[:, L - 1:L] + m_last
    rows = jnp.concatenate(
        [big_m, a, e, g, jnp.zeros((LANES - 4 * SUBLANES, gates.shape[1]), F32)], axis=0)
    cols = rows.T

    t_idx = lax.broadcasted_iota(jnp.int32, (L, L), 0)
    s_idx = lax.broadcasted_iota(jnp.int32, (L, L), 1)
    causal = s_idx <= t_idx

    for h in range(N_HEADS):
        sl = slice(h * HEAD_DIM, (h + 1) * HEAD_DIM)
        qh = q_ref[:, sl]
        kh = k_ref[:, sl] * jnp.asarray(K_SCALE, BF16)
        vh = v_ref[:, sl]
        m_col = cols[:L, h:h + 1]
        a_col = cols[:L, SUBLANES + h:SUBLANES + h + 1]
        e_col = cols[:L, 2 * SUBLANES + h:2 * SUBLANES + h + 1]
        g_col = cols[:L, 3 * SUBLANES + h:3 * SUBLANES + h + 1]
        u_row = u[h:h + 1, :L]

        s = lax.dot_general(qh, kh, _NT, preferred_element_type=F32)
        yield
        sw = s * jnp.exp(jnp.where(causal, u_row - m_col, -jnp.inf))
        c_h = c_ref[0, h]
        n_h = n_ref[0, h:h + 1, :]
        num = a_col * _dot(qh, c_h.astype(BF16)) + _dot(sw.astype(BF16), vh)
        qn = jnp.sum(qh.astype(F32) * n_h, axis=-1, keepdims=True)
        den = a_col * qn + jnp.sum(sw, axis=-1, keepdims=True)
        hh = num * (1.0 / jnp.maximum(jnp.abs(den), e_col))
        y = hh * _rms_scale(hh) * gain_ref[:, sl] * jax.nn.sigmoid(o_ref[:, sl].astype(F32))
        mix_ref[:, sl] = y.astype(BF16)

        kg = kh.astype(F32) * g_col
        dec_h = dec[h:h + 1, :]
        c_ref[0, h] = dec_h * c_h + lax.dot_general(
            kg.astype(BF16), vh, _TN, preferred_element_type=F32)
        n_ref[0, h:h + 1, :] = dec_h * n_h + jnp.sum(kg, axis=0, keepdims=True)
        if h == N_HEADS - 1:
            m_ref[0] = jnp.broadcast_to(m_new, (SUBLANES, LANES))
        yield


def _run(*phase_generators):
    live = list(phase_generators)
    while live:
        live = [g for g in live if next(g, StopIteration) is not StopIteration]


def _seq_views(s, rows, token_refs, state_refs):
    return ([ref.at[pl.ds(s * rows, rows), :] for ref in token_refs],
            [ref.at[pl.ds(s, 1)] for ref in state_refs])


def _mlstm_state_specs(index_map3, index_map4, group=1):
    return [
        pl.BlockSpec((group, N_HEADS, HEAD_DIM, HEAD_DIM), index_map4),
        pl.BlockSpec((group, N_HEADS, HEAD_DIM), index_map3),
        pl.BlockSpec((group, SUBLANES, LANES), index_map3),
    ]


def _mlstm_state_shapes(batch):
    return [
        jax.ShapeDtypeStruct((batch, N_HEADS, HEAD_DIM, HEAD_DIM), F32),
        jax.ShapeDtypeStruct((batch, N_HEADS, HEAD_DIM), F32),
        jax.ShapeDtypeStruct((batch, SUBLANES, LANES), F32),
    ]


def _mlstm_kernel(q_ref, k_ref, v_ref, o_ref, gt_ref, gain_ref, c0_ref, n0_ref, m0_ref,
                  mix_ref, c_ref, n_ref, m_ref, *, chunk, group):
    c_ref[...] = c0_ref[...]
    n_ref[...] = n0_ref[...]
    m_ref[...] = m0_ref[...]
    phases = []
    for s in range(group):
        (q, k, v, o, mix), (c, n, m) = _seq_views(
            s, chunk, (q_ref, k_ref, v_ref, o_ref, mix_ref), (c_ref, n_ref, m_ref))
        phases.append(_mlstm_phases(q, k, v, o, gt_ref[s], gain_ref, mix, c, n, m, chunk))
    _run(*phases)


def _mlstm_single_chunk(proj, gt3, gain, init, *, batch, seq, group):
    tok = lambda col: pl.BlockSpec((group * seq, D_GROUP), lambda b, col=col: (b, col))
    state_specs = _mlstm_state_specs(lambda b: (b, 0, 0), lambda b: (b, 0, 0, 0), group)
    return pl.pallas_call(
        functools.partial(_mlstm_kernel, chunk=seq, group=group),
        grid=(batch // group,),
        in_specs=[tok(0), tok(1), tok(2), tok(3),
                  pl.BlockSpec((group, SUBLANES, gt3.shape[2]), lambda b: (b, 0, 0)),
                  pl.BlockSpec((1, D_GROUP), lambda b: (0, 0))] + state_specs,
        out_specs=[pl.BlockSpec((group * seq, D_GROUP), lambda b: (b, 0))] + state_specs,
        out_shape=[jax.ShapeDtypeStruct((batch * seq, D_GROUP), BF16)] + _mlstm_state_shapes(batch),
        compiler_params=_params(1),
        name="mlstm",
    )(proj, proj, proj, proj, gt3, gain, *init)


def _ret_phases(lg_ref, q_ref, k_ref, v_ref, g_ref, cos_ref, sin_ref, gain_ref, mix_ref, r_ref, L):
    half = HEAD_DIM // 2
    cos = cos_ref[...]
    sin = sin_ref[...]
    cos_k = cos * K_SCALE
    sin_k = sin * K_SCALE
    t_idx = lax.broadcasted_iota(jnp.int32, (L, L), 0)
    s_idx = lax.broadcasted_iota(jnp.int32, (L, L), 1)
    diff = t_idx - s_idx
    diff_pos = jnp.maximum(diff, 0).astype(F32)
    t_col = lax.broadcasted_iota(jnp.int32, (L, 1), 0).astype(F32)

    for h in range(N_HEADS):
        lo = slice(h * HEAD_DIM, h * HEAD_DIM + half)
        hi = slice(h * HEAD_DIM + half, (h + 1) * HEAD_DIM)
        sl = slice(h * HEAD_DIM, (h + 1) * HEAD_DIM)
        lg = lg_ref[h]
        q1 = q_ref[:, lo].astype(F32)
        q2 = q_ref[:, hi].astype(F32)
        k1 = k_ref[:, lo].astype(F32)
        k2 = k_ref[:, hi].astype(F32)
        qr = jnp.concatenate([q1 * cos - q2 * sin, q2 * cos + q1 * sin], axis=-1).astype(BF16)
        kr = jnp.concatenate([k1 * cos_k - k2 * sin_k, k2 * cos_k + k1 * sin_k], axis=-1)
        vh = v_ref[:, sl]

        dmask = jnp.where(diff >= 0, jnp.exp(lg * diff_pos), 0.0)
        inter = jnp.exp(lg * (t_col + 1.0))
        kdec = jnp.exp(lg * (L - 1.0 - t_col))
        cdec = jnp.exp(lg * jnp.full((1, 1), L, F32))

        s = lax.dot_general(qr, kr.astype(BF16), _NT, preferred_element_type=F32)
        yield
        r_h = r_ref[0, h]
        o = _dot((s * dmask).astype(BF16), vh) + inter * _dot(qr, r_h.astype(BF16))
        r_ref[0, h] = cdec * r_h + lax.dot_general(
            (kr * kdec).astype(BF16), vh, _TN, preferred_element_type=F32)

        oc = o - jnp.mean(o, axis=-1, keepdims=True)
        y = oc * _rms_scale(oc) * gain_ref[:, sl] * jax.nn.silu(g_ref[:, sl].astype(F32))
        mix_ref[:, sl] = y.astype(BF16)
        yield


def _ret_kernel(lg_ref, q_ref, k_ref, v_ref, g_ref, cos_ref, sin_ref, gain_ref, r0_ref,
                mix_ref, r_ref, *, chunk, group):
    r_ref[...] = r0_ref[...]
    phases = []
    for s in range(group):
        (q, k, v, g, mix), (r,) = _seq_views(
            s, chunk, (q_ref, k_ref, v_ref, g_ref, mix_ref), (r_ref,))
        phases.append(_ret_phases(lg_ref, q, k, v, g, cos_ref, sin_ref, gain_ref, mix, r, chunk))
    _run(*phases)


def _retention_single_chunk(proj, log_gamma, cos, sin, gain, init, *, batch, seq, group):
    tok = lambda col: pl.BlockSpec((group * seq, D_GROUP), lambda b, col=col: (b, col))
    state_spec = pl.BlockSpec((group, N_HEADS, HEAD_DIM, HEAD_DIM), lambda b: (b, 0, 0, 0))
    rope_spec = pl.BlockSpec((seq, HEAD_DIM // 2), lambda b: (0, 0))
    return pl.pallas_call(
        functools.partial(_ret_kernel, chunk=seq, group=group),
        grid=(batch // group,),
        in_specs=[pl.BlockSpec(memory_space=pltpu.SMEM),
                  tok(4), tok(5), tok(6), tok(7), rope_spec, rope_spec,
                  pl.BlockSpec((1, D_GROUP), lambda b: (0, 0)), state_spec],
        out_specs=[pl.BlockSpec((group * seq, D_GROUP), lambda b: (b, 0)), state_spec],
        out_shape=[
            jax.ShapeDtypeStruct((batch * seq, D_GROUP), BF16),
            jax.ShapeDtypeStruct((batch, N_HEADS, HEAD_DIM, HEAD_DIM), F32),
        ],
        compiler_params=_params(1),
        name="retention",
    )(log_gamma, proj, proj, proj, proj, cos, sin, gain, init)


def _prompt_kernel(lg_ref, x_ref, g1_ref, w_ref, wgate_ref, bgate_ref, gain_m_ref, cos_ref,
                   sin_ref, gain_r_ref, mixm_ref, c_ref, n_ref, m_ref, mixr_ref, r_ref,
                   xn_scr, proj_scr, gt_scr, *, n_chunks, chunks_per_seq, chunk):
    t = pl.program_id(0)
    piece = D_PROJ // (2 * N_HEADS)

    def inproj_pieces(slot):
        x = x_ref[...]
        xn_scr[...] = (x * _rms_scale(x) * g1_ref[...]).astype(BF16)
        gates = _dot(xn_scr[...], wgate_ref[...]) + bgate_ref[...]
        gt_scr[slot] = gates.T[0:N_GATES, :]
        for p in range(D_PROJ // piece):
            cols = pl.ds(p * piece, piece)
            proj_scr[slot, :, cols] = _dot(xn_scr[...], w_ref[:, cols]).astype(BF16)
            yield

    def mixers(slot):
        group = lambda k: proj_scr.at[slot, :, pl.ds(k * D_GROUP, D_GROUP)]
        return [
            _mlstm_phases(group(0), group(1), group(2), group(3), gt_scr[slot], gain_m_ref,
                          mixm_ref, c_ref, n_ref, m_ref, chunk),
            _ret_phases(lg_ref, group(4), group(5), group(6), group(7), cos_ref, sin_ref,
                        gain_r_ref, mixr_ref, r_ref, chunk),
        ]

    @pl.when((t >= 1) & ((t - 1) % chunks_per_seq == 0))
    def _():
        c_ref[...] = jnp.zeros_like(c_ref)
        n_ref[...] = jnp.zeros_like(n_ref)
        m_ref[...] = jnp.zeros_like(m_ref)
        r_ref[...] = jnp.zeros_like(r_ref)

    @pl.when(t == 0)
    def _():
        _run(inproj_pieces(0))

    @pl.when((t >= 1) & (t < n_chunks))
    def _():
        slot = t % 2
        pieces = inproj_pieces(slot)
        heads = mixers(1 - slot)
        for _ in range(N_HEADS):
            for phases in heads:
                next(phases)
                next(pieces)
                next(phases)

    @pl.when(t == n_chunks)
    def _():
        _run(*mixers((n_chunks - 1) % 2))


def _prompt_front(x2d, w, rope, *, batch, seq):
    chunk = PROMPT_CHUNK
    chunks_per_seq = seq // chunk
    n_chunks = batch * chunks_per_seq
    cos, sin = rope
    const = lambda shape, **kw: pl.BlockSpec(shape, lambda t: (0,) * len(shape), **kw)
    mixed = lambda t: jnp.maximum(t - 1, 0)
    rope_spec = pl.BlockSpec((chunk, HEAD_DIM // 2), lambda t: (mixed(t) % chunks_per_seq, 0))
    mix_spec = pl.BlockSpec((chunk, D_GROUP), lambda t: (mixed(t), 0))
    mix_shape = jax.ShapeDtypeStruct((batch * seq, D_GROUP), BF16)
    seq_of = lambda t: mixed(t) // chunks_per_seq
    state4 = pl.BlockSpec((1, N_HEADS, HEAD_DIM, HEAD_DIM), lambda t: (seq_of(t), 0, 0, 0))
    state_specs = _mlstm_state_specs(lambda t: (seq_of(t), 0, 0), lambda t: (seq_of(t), 0, 0, 0))
    return pl.pallas_call(
        functools.partial(_prompt_kernel, n_chunks=n_chunks, chunks_per_seq=chunks_per_seq,
                          chunk=chunk),
        grid=(n_chunks + 1,),
        in_specs=[
            pl.BlockSpec(memory_space=pltpu.SMEM),
            pl.BlockSpec((chunk, D_MODEL), lambda t: (jnp.minimum(t, n_chunks - 1), 0)),
            const((1, D_MODEL)),
            const((D_MODEL, D_PROJ), pipeline_mode=pl.Buffered(1)),
            const((D_MODEL, LANES), pipeline_mode=pl.Buffered(1)),
            const((1, LANES)),
            const((1, D_GROUP)),
            rope_spec, rope_spec,
            const((1, D_GROUP)),
        ],
        out_specs=[mix_spec] + state_specs + [mix_spec, state4],
        out_shape=([mix_shape] + _mlstm_state_shapes(batch)
                   + [mix_shape, jax.ShapeDtypeStruct((batch, N_HEADS, HEAD_DIM, HEAD_DIM), F32)]),
        scratch_shapes=[
            pltpu.VMEM((chunk, D_MODEL), BF16),
            pltpu.VMEM((2, chunk, D_PROJ), BF16),
            pltpu.VMEM((2, N_GATES, chunk), F32),
        ],
        compiler_params=_params(1),
        name="prompt_front",
    )(w["log_gamma"], x2d, w["g1"], w["w_in"], w["w_gatecols"], w["b_gate"], w["gain_m"],
      cos, sin, w["gain_r"])


def _outproj_kernel(x_ref, mm_ref, mr_ref, wm_ref, wr_ref, g2_ref, x1_ref, hn_ref):
    x1 = x_ref[...] + _dot(mm_ref[...], wm_ref[...]) + _dot(mr_ref[...], wr_ref[...])
    x1_ref[...] = x1
    hn_ref[...] = (x1 * _rms_scale(x1) * g2_ref[...]).astype(BF16)


def _outproj(x, mix_m, mix_r, w_out, g2, *, tm):
    n = x.shape[0]
    w_spec = lambda half: pl.BlockSpec((D_GROUP, D_MODEL), lambda i, half=half: (half, 0),
                                       pipeline_mode=pl.Buffered(1))
    return pl.pallas_call(
        _outproj_kernel,
        grid=(n // tm,),
        in_specs=[
            pl.BlockSpec((tm, D_MODEL), lambda i: (i, 0)),
            pl.BlockSpec((tm, D_GROUP), lambda i: (i, 0)),
            pl.BlockSpec((tm, D_GROUP), lambda i: (i, 0)),
            w_spec(0), w_spec(1),
            pl.BlockSpec((1, D_MODEL), lambda i: (0, 0)),
        ],
        out_specs=[
            pl.BlockSpec((tm, D_MODEL), lambda i: (i, 0)),
            pl.BlockSpec((tm, D_MODEL), lambda i: (i, 0)),
        ],
        out_shape=[
            jax.ShapeDtypeStruct((n, D_MODEL), F32),
            jax.ShapeDtypeStruct((n, D_MODEL), BF16),
        ],
        compiler_params=_params(1),
        name="outproj",
    )(x, mix_m, mix_r, w_out, w_out, g2)


def _swiglu_down(hn, w_gu, w_down):
    gu = _dot(hn, w_gu)
    acts = []
    for base in range(0, w_gu.shape[1], 2 * FFN_GRANULE):
        gate = gu[:, base:base + FFN_GRANULE]
        up = gu[:, base + FFN_GRANULE:base + 2 * FFN_GRANULE]
        acts.append((jax.nn.silu(gate) * up).astype(BF16))
    act = acts[0] if len(acts) == 1 else jnp.concatenate(acts, axis=-1)
    return _dot(act, w_down)


def _ffn_kernel(*refs, pack):
    if pack:
        (hn_ref, x1_hbm, wg32_ref, wu32_ref, wd32_ref, gf_ref,
         y_ref, wgu_ref, wd_ref, x1_buf, x1_sem) = refs
    else:
        hn_ref, x1_hbm, wgu_ref, wd_ref, gf_ref, y_ref, x1_buf, x1_sem = refs
    i = pl.program_id(0)
    f = pl.program_id(1)
    last = pl.num_programs(1) - 1
    tm = x1_buf.shape[0]

    def x1_copy(tile):
        return pltpu.make_async_copy(x1_hbm.at[pl.ds(tile * tm, tm), :], x1_buf, x1_sem)

    @pl.when((f == 0) & (i == 0))
    def _():
        x1_copy(0).start()

    @pl.when(f == 0)
    def _():
        x1_copy(i).wait()

    @pl.when((f == 1) & (i + 1 < pl.num_programs(0)))
    def _():
        x1_copy(i + 1).start()

    def step(first, final):
        if pack:
            wgu_ref[:, :FFN_GRANULE] = wg32_ref[...].astype(BF16)
            wgu_ref[:, FFN_GRANULE:] = wu32_ref[...].astype(BF16)
            wd_ref[...] = wd32_ref[...].astype(BF16)
        part = _swiglu_down(hn_ref[...], wgu_ref[...], wd_ref[...])
        if first:
            y_ref[...] = x1_buf[...] + part
        else:
            y_ref[...] += part
        if final:
            x2 = y_ref[...]
            y_ref[...] = x2 * _rms_scale(x2) * gf_ref[...]

    pl.when(f == 0)(functools.partial(step, True, False))
    pl.when((f > 0) & (f < last))(functools.partial(step, False, False))
    pl.when(f == last)(functools.partial(step, False, True))


def _ffn(hn, x1, w, *, tm, tf):
    n = hn.shape[0]
    pack = "w_gu" not in w
    assert not pack or (n == tm and tf == FFN_GRANULE)
    assert D_FF // tf >= 3
    gu_spec = pl.BlockSpec((D_MODEL, 2 * tf), lambda i, f: (0, f))
    down_spec = pl.BlockSpec((tf, D_MODEL), lambda i, f: (f, 0))
    if pack:
        col_spec = pl.BlockSpec((D_MODEL, tf), lambda i, f: (0, f))
        w_specs, w_args = [col_spec, col_spec, down_spec], [w["w_gate"], w["w_up"], w["w_down"]]
    else:
        w_specs, w_args = [gu_spec, down_spec], [w["w_gu"], w["w_down"]]
    return pl.pallas_call(
        functools.partial(_ffn_kernel, pack=pack),
        grid=(n // tm, D_FF // tf),
        in_specs=[
            pl.BlockSpec((tm, D_MODEL), lambda i, f: (i, 0)),
            pl.BlockSpec(memory_space=pl.ANY),
            *w_specs,
            pl.BlockSpec((1, D_MODEL), lambda i, f: (0, 0)),
        ],
        out_specs=([pl.BlockSpec((tm, D_MODEL), lambda i, f: (i, 0))]
                   + [gu_spec, down_spec] * pack),
        out_shape=([jax.ShapeDtypeStruct((n, D_MODEL), F32)]
                   + [jax.ShapeDtypeStruct((D_MODEL, 2 * D_FF), BF16),
                      jax.ShapeDtypeStruct((D_FF, D_MODEL), BF16)] * pack),
        scratch_shapes=[pltpu.VMEM((tm, D_MODEL), F32), pltpu.SemaphoreType.DMA(())],
        compiler_params=_params(2),
        name="ffn",
    )(hn, x1, *w_args, w["g_final"])


def _rope_tables(pos):
    half = HEAD_DIM // 2
    freqs = ROPE_BASE ** (-jnp.arange(half, dtype=F32) / half)
    ang = pos[:, None] * freqs[None, :]
    return jnp.cos(ang), jnp.sin(ang)


def _tail(x2d, mix_m, mix_r, w, *, tf):
    x1, hn = _outproj(x2d, mix_m, mix_r, w["w_out"], w["g2"], tm=512)
    return _ffn(hn, x1, w, tm=1024, tf=tf)


def _prompt_trunk(x, w):
    batch, seq, _ = x.shape
    x2d = x.reshape(batch * seq, D_MODEL)
    rope = _rope_tables(jnp.arange(seq, dtype=F32))
    mix_m, c_new, n_new, m_new, mix_r, r_new = _prompt_front(x2d, w, rope, batch=batch, seq=seq)
    (y,) = _tail(x2d, mix_m, mix_r, w, tf=512)
    return (y.reshape(batch, seq, D_MODEL), c_new[None], n_new[None],
            m_new[:, :N_HEADS, 0][None], r_new[None])


def _sample_trunk(x, init, w):
    batch, seq, _ = x.shape
    x2d = x.reshape(batch * seq, D_MODEL)
    proj, gt = _inproj(x2d, w["g1"], w["w_in"], w["w_gatecols"], w["b_gate"],
                       tm=INPROJ_TM, tn=INPROJ_TN)
    gt3 = gt.reshape(N_GATES, batch, seq).transpose(1, 0, 2)
    gt3 = jnp.pad(gt3, ((0, 0), (0, 0), (0, max(0, LANES - seq))))
    c0, n0, m0, r0 = init
    m0 = jnp.pad(m0, ((0, 0), (0, SUBLANES - N_HEADS)))
    m0 = jnp.broadcast_to(m0[:, :, None], (batch, SUBLANES, LANES))
    mix_m, c_new, n_new, m_new = _mlstm_single_chunk(proj, gt3, w["gain_m"], (c0, n0, m0),
                                                     batch=batch, seq=seq, group=SAMPLE_GROUP)
    cos, sin = _rope_tables(PAST_LEN + jnp.arange(seq, dtype=F32))
    mix_r, r_new = _retention_single_chunk(proj, w["log_gamma"], cos, sin, w["gain_r"], r0,
                                           batch=batch, seq=seq, group=SAMPLE_GROUP)
    y, w_gu16, w_down16 = _tail(x2d, mix_m, mix_r, w, tf=FFN_GRANULE)
    w16 = dict(w_gu=w_gu16, w_down=w_down16)
    return (y.reshape(batch, seq, D_MODEL), c_new[None], n_new[None],
            m_new[:, :N_HEADS, 0][None], r_new[None]), w16


def kernel(x_prompt, x_sample, state_mlstm_C, state_mlstm_n, state_mlstm_m, state_ret,
           g_norm1, w_in, b_gates, g_mlstm_norm, g_ret_norm, w_out, g_norm2,
           w_gate, w_up, w_down, g_final):
    w_in0 = w_in[0].astype(BF16)
    w = dict(
        g1=g_norm1[0].reshape(1, D_MODEL),
        w_in=w_in0,
        w_gatecols=jnp.pad(w_in0[:, D_PROJ:], ((0, 0), (0, LANES - N_GATES))),
        b_gate=jnp.pad(b_gates[0], (0, LANES - N_GATES)).reshape(1, LANES),
        gain_m=g_mlstm_norm[0].reshape(1, D_GROUP),
        gain_r=g_ret_norm[0].reshape(1, D_GROUP),
        w_out=w_out[0].astype(BF16),
        g2=g_norm2[0].reshape(1, D_MODEL),
        w_gate=w_gate[0],
        w_up=w_up[0],
        w_down=w_down[0],
        g_final=g_final.reshape(1, D_MODEL),
        log_gamma=jnp.log(1.0 - jnp.exp2(-5.0 - jnp.arange(N_HEADS, dtype=F32))),
    )
    init_s = (state_mlstm_C[0], state_mlstm_n[0], state_mlstm_m[0], state_ret[0])
    (y_s, c_s, n_s, m_s, r_s), w16 = _sample_trunk(x_sample, init_s, w)
    y_p, c_p, n_p, m_p, r_p = _prompt_trunk(x_prompt, {**w, **w16})
    return (y_p, y_s, c_p, n_p, m_p, r_p, c_s, n_s, m_s, r_s)
```

```python
import functools

import jax
import jax.numpy as jnp
from jax import lax
from jax.experimental import pallas as pl
from jax.experimental.pallas import tpu as pltpu

F32 = jnp.float32
BF16 = jnp.bfloat16

D_MODEL = 2048
N_HEADS = 4
HEAD_DIM = 256
D_GROUP = N_HEADS * HEAD_DIM
D_PROJ = 8 * D_GROUP
N_GATES = 2 * N_HEADS
D_FF = 5632
ROPE_BASE = 10000.0
PAST_LEN = 4096
EPS = 1e-6
K_SCALE = HEAD_DIM ** -0.5

LANES = 128
SUBLANES = 8
VMEM_LIMIT = 60 * 1024 * 1024

PROMPT_CHUNK = 256
INPROJ_TM = 1024
INPROJ_TN = 2048
SAMPLE_GROUP = 4
OUTPROJ_TM = 512
FFN_TM = 1024
FFN_TF = 512
FFN_GRANULE = 256

_NT = (((1,), (1,)), ((), ()))
_TN = (((0,), (0,)), ((), ()))


def _dot(a, b):
    return jnp.dot(a, b, preferred_element_type=F32)


def _rms_scale(x):
    return lax.rsqrt(jnp.mean(x * x, axis=-1, keepdims=True) + EPS)


def _params(n_axes):
    return pltpu.CompilerParams(
        dimension_semantics=("arbitrary",) * n_axes, vmem_limit_bytes=VMEM_LIMIT)


def _inproj_prologue(x_ref, g1_ref, wgate_ref, bgate_ref, gt_ref, xn_ref):
    x = x_ref[...]
    xn = (x * _rms_scale(x) * g1_ref[...]).astype(BF16)
    xn_ref[...] = xn
    gates = _dot(xn, wgate_ref[...]) + bgate_ref[...]
    gt_ref[...] = gates.T[0:N_GATES, :]


def _inproj_kernel(x_ref, g1_ref, w_ref, wgate_ref, bgate_ref, proj_ref, gt_ref, xn_ref):
    @pl.when(pl.program_id(1) == 0)
    def _():
        _inproj_prologue(x_ref, g1_ref, wgate_ref, bgate_ref, gt_ref, xn_ref)

    proj_ref[...] = _dot(xn_ref[...], w_ref[...]).astype(BF16)


def _inproj(x, g1, w_main, w_gate, b_gate, *, tm, tn):
    n = x.shape[0]
    return pl.pallas_call(
        _inproj_kernel,
        grid=(n // tm, D_PROJ // tn),
        in_specs=[
            pl.BlockSpec((tm, D_MODEL), lambda i, j: (i, 0)),
            pl.BlockSpec((1, D_MODEL), lambda i, j: (0, 0)),
            pl.BlockSpec((D_MODEL, tn), lambda i, j: (0, j)),
            pl.BlockSpec((D_MODEL, LANES), lambda i, j: (0, 0)),
            pl.BlockSpec((1, LANES), lambda i, j: (0, 0)),
        ],
        out_specs=[
            pl.BlockSpec((tm, tn), lambda i, j: (i, j)),
            pl.BlockSpec((N_GATES, tm), lambda i, j: (0, i)),
        ],
        out_shape=[
            jax.ShapeDtypeStruct((n, D_PROJ), BF16),
            jax.ShapeDtypeStruct((N_GATES, n), F32),
        ],
        scratch_shapes=[pltpu.VMEM((tm, D_MODEL), BF16)],
        compiler_params=_params(2),
        name="inproj",
    )(x, g1, w_main, w_gate, b_gate)


def _lane_scan(x, combine, identity):
    width = x.shape[1]
    lane = lax.broadcasted_iota(jnp.int32, x.shape, 1)
    shift = 1
    while shift < width:
        shifted = jnp.where(lane >= shift, pltpu.roll(x, shift, axis=1), identity)
        x = combine(x, shifted)
        shift *= 2
    return x


def _mlstm_phases(q_ref, k_ref, v_ref, o_ref, gates, gain_ref, mix_ref, c_ref, n_ref, m_ref, L):
    b = _lane_scan(jax.nn.log_sigmoid(gates), jnp.add, 0.0)
    b = pltpu.roll(b, N_HEADS, axis=0)
    u = gates - b
    m_prev = m_ref[0][:, 0:1]
    big_m = jnp.maximum(m_prev, _lane_scan(u, jnp.maximum, -jnp.inf))
    a = jnp.exp(m_prev - big_m)
    e = jnp.exp(-(b + big_m))
    m_last = big_m[:, L - 1:L]
    g = jnp.exp(u - m_last)
    dec = a[:, L - 1:L]
    m_new = b[:, L - 1:L] + m_last
    rows = jnp.concatenate(
        [big_m, a, e, g, jnp.zeros((LANES - 4 * SUBLANES, gates.shape[1]), F32)], axis=0)
    cols = rows.T

    t_idx = lax.broadcasted_iota(jnp.int32, (L, L), 0)
    s_idx = lax.broadcasted_iota(jnp.int32, (L, L), 1)
    causal = s_idx <= t_idx

    for h in range(N_HEADS):
        sl = slice(h * HEAD_DIM, (h + 1) * HEAD_DIM)
        qh = q_ref[:, sl]
        kh = k_ref[:, sl] * jnp.asarray(K_SCALE, BF16)
        vh = v_ref[:, sl]
        m_col = cols[:L, h:h + 1]
        a_col = cols[:L, SUBLANES + h:SUBLANES + h + 1]
        e_col = cols[:L, 2 * SUBLANES + h:2 * SUBLANES + h + 1]
        g_col = cols[:L, 3 * SUBLANES + h:3 * SUBLANES + h + 1]
        u_row = u[h:h + 1, :L]

        s = lax.dot_general(qh, kh, _NT, preferred_element_type=F32)
        yield
        sw = s * jnp.exp(jnp.where(causal, u_row - m_col, -jnp.inf))
        c_h = c_ref[0, h]
        n_h = n_ref[0, h:h + 1, :]
        num = a_col * _dot(qh, c_h.astype(BF16)) + _dot(sw.astype(BF16), vh)
        qn = jnp.sum(qh.astype(F32) * n_h, axis=-1, keepdims=True)
        den = a_col * qn + jnp.sum(sw, axis=-1, keepdims=True)
        hh = num * (1.0 / jnp.maximum(jnp.abs(den), e_col))
        y = hh * _rms_scale(hh) * gain_ref[:, sl] * jax.nn.sigmoid(o_ref[:, sl].astype(F32))
        mix_ref[:, sl] = y.astype(BF16)

        kg = kh.astype(F32) * g_col
        dec_h = dec[h:h + 1, :]
        c_ref[0, h] = dec_h * c_h + lax.dot_general(
            kg.astype(BF16), vh, _TN, preferred_element_type=F32)
        n_ref[0, h:h + 1, :] = dec_h * n_h + jnp.sum(kg, axis=0, keepdims=True)
        if h == N_HEADS - 1:
            m_ref[0] = jnp.broadcast_to(m_new, (SUBLANES, LANES))
        yield


def _run(*phase_generators):
    live = list(phase_generators)
    while live:
        live = [g for g in live if next(g, StopIteration) is not StopIteration]


def _seq_views(s, rows, token_refs, state_refs):
    return ([ref.at[pl.ds(s * rows, rows), :] for ref in token_refs],
            [ref.at[pl.ds(s, 1)] for ref in state_refs])


def _mlstm_state_specs(index_map3, index_map4, group=1):
    return [
        pl.BlockSpec((group, N_HEADS, HEAD_DIM, HEAD_DIM), index_map4),
        pl.BlockSpec((group, N_HEADS, HEAD_DIM), index_map3),
        pl.BlockSpec((group, SUBLANES, LANES), index_map3),
    ]


def _mlstm_state_shapes(batch):
    return [
        jax.ShapeDtypeStruct((batch, N_HEADS, HEAD_DIM, HEAD_DIM), F32),
        jax.ShapeDtypeStruct((batch, N_HEADS, HEAD_DIM), F32),
        jax.ShapeDtypeStruct((batch, SUBLANES, LANES), F32),
    ]


def _mlstm_kernel(q_ref, k_ref, v_ref, o_ref, gt_ref, gain_ref, c0_ref, n0_ref, m0_ref,
                  mix_ref, c_ref, n_ref, m_ref, *, chunk, group):
    c_ref[...] = c0_ref[...]
    n_ref[...] = n0_ref[...]
    m_ref[...] = m0_ref[...]
    phases = []
    for s in range(group):
        (q, k, v, o, mix), (c, n, m) = _seq_views(
            s, chunk, (q_ref, k_ref, v_ref, o_ref, mix_ref), (c_ref, n_ref, m_ref))
        phases.append(_mlstm_phases(q, k, v, o, gt_ref[s], gain_ref, mix, c, n, m, chunk))
    _run(*phases)


def _mlstm_single_chunk(proj, gt3, gain, init, *, batch, seq, group):
    tok = lambda col: pl.BlockSpec((group * seq, D_GROUP), lambda b, col=col: (b, col))
    state_specs = _mlstm_state_specs(lambda b: (b, 0, 0), lambda b: (b, 0, 0, 0), group)
    return pl.pallas_call(
        functools.partial(_mlstm_kernel, chunk=seq, group=group),
        grid=(batch // group,),
        in_specs=[tok(0), tok(1), tok(2), tok(3),
                  pl.BlockSpec((group, SUBLANES, gt3.shape[2]), lambda b: (b, 0, 0)),
                  pl.BlockSpec((1, D_GROUP), lambda b: (0, 0))] + state_specs,
        out_specs=[pl.BlockSpec((group * seq, D_GROUP), lambda b: (b, 0))] + state_specs,
        out_shape=[jax.ShapeDtypeStruct((batch * seq, D_GROUP), BF16)] + _mlstm_state_shapes(batch),
        compiler_params=_params(1),
        name="mlstm",
    )(proj, proj, proj, proj, gt3, gain, *init)


def _ret_phases(lg_ref, q_ref, k_ref, v_ref, g_ref, cos_ref, sin_ref, gain_ref, mix_ref, r_ref, L):
    half = HEAD_DIM // 2
    cos = cos_ref[...]
    sin = sin_ref[...]
    cos_k = cos * K_SCALE
    sin_k = sin * K_SCALE
    t_idx = lax.broadcasted_iota(jnp.int32, (L, L), 0)
    s_idx = lax.broadcasted_iota(jnp.int32, (L, L), 1)
    diff = t_idx - s_idx
    diff_pos = jnp.maximum(diff, 0).astype(F32)
    t_col = lax.broadcasted_iota(jnp.int32, (L, 1), 0).astype(F32)

    for h in range(N_HEADS):
        lo = slice(h * HEAD_DIM, h * HEAD_DIM + half)
        hi = slice(h * HEAD_DIM + half, (h + 1) * HEAD_DIM)
        sl = slice(h * HEAD_DIM, (h + 1) * HEAD_DIM)
        lg = lg_ref[h]
        q1 = q_ref[:, lo].astype(F32)
        q2 = q_ref[:, hi].astype(F32)
        k1 = k_ref[:, lo].astype(F32)
        k2 = k_ref[:, hi].astype(F32)
        qr = jnp.concatenate([q1 * cos - q2 * sin, q2 * cos + q1 * sin], axis=-1).astype(BF16)
        kr = jnp.concatenate([k1 * cos_k - k2 * sin_k, k2 * cos_k + k1 * sin_k], axis=-1)
        vh = v_ref[:, sl]

        dmask = jnp.where(diff >= 0, jnp.exp(lg * diff_pos), 0.0)
        inter = jnp.exp(lg * (t_col + 1.0))
        kdec = jnp.exp(lg * (L - 1.0 - t_col))
        cdec = jnp.exp(lg * jnp.full((1, 1), L, F32))

        s = lax.dot_general(qr, kr.astype(BF16), _NT, preferred_element_type=F32)
        yield
        r_h = r_ref[0, h]
        o = _dot((s * dmask).astype(BF16), vh) + inter * _dot(qr, r_h.astype(BF16))
        r_ref[0, h] = cdec * r_h + lax.dot_general(
            (kr * kdec).astype(BF16), vh, _TN, preferred_element_type=F32)

        oc = o - jnp.mean(o, axis=-1, keepdims=True)
        y = oc * _rms_scale(oc) * gain_ref[:, sl] * jax.nn.silu(g_ref[:, sl].astype(F32))
        mix_ref[:, sl] = y.astype(BF16)
        yield


def _ret_kernel(lg_ref, q_ref, k_ref, v_ref, g_ref, cos_ref, sin_ref, gain_ref, r0_ref,
                mix_ref, r_ref, *, chunk, group):
    r_ref[...] = r0_ref[...]
    phases = []
    for s in range(group):
        (q, k, v, g, mix), (r,) = _seq_views(
            s, chunk, (q_ref, k_ref, v_ref, g_ref, mix_ref), (r_ref,))
        phases.append(_ret_phases(lg_ref, q, k, v, g, cos_ref, sin_ref, gain_ref, mix, r, chunk))
    _run(*phases)


def _retention_single_chunk(proj, log_gamma, cos, sin, gain, init, *, batch, seq, group):
    tok = lambda col: pl.BlockSpec((group * seq, D_GROUP), lambda b, col=col: (b, col))
    state_spec = pl.BlockSpec((group, N_HEADS, HEAD_DIM, HEAD_DIM), lambda b: (b, 0, 0, 0))
    rope_spec = pl.BlockSpec((seq, HEAD_DIM // 2), lambda b: (0, 0))
    return pl.pallas_call(
        functools.partial(_ret_kernel, chunk=seq, group=group),
        grid=(batch // group,),
        in_specs=[pl.BlockSpec(memory_space=pltpu.SMEM),
                  tok(4), tok(5), tok(6), tok(7), rope_spec, rope_spec,
                  pl.BlockSpec((1, D_GROUP), lambda b: (0, 0)), state_spec],
        out_specs=[pl.BlockSpec((group * seq, D_GROUP), lambda b: (b, 0)), state_spec],
        out_shape=[
            jax.ShapeDtypeStruct((batch * seq, D_GROUP), BF16),
            jax.ShapeDtypeStruct((batch, N_HEADS, HEAD_DIM, HEAD_DIM), F32),
        ],
        compiler_params=_params(1),
        name="retention",
    )(log_gamma, proj, proj, proj, proj, cos, sin, gain, init)


def _prompt_kernel(lg_ref, x_ref, g1_ref, w_ref, wgate_ref, bgate_ref, gain_m_ref, cos_ref,
                   sin_ref, gain_r_ref, mixm_ref, c_ref, n_ref, m_ref, mixr_ref, r_ref,
                   xn_scr, proj_scr, gt_scr, *, n_chunks, chunks_per_seq, chunk):
    t = pl.program_id(0)
    piece = D_PROJ // (2 * N_HEADS)

    def inproj_pieces(slot):
        x = x_ref[...]
        xn_scr[...] = (x * _rms_scale(x) * g1_ref[...]).astype(BF16)
        gates = _dot(xn_scr[...], wgate_ref[...]) + bgate_ref[...]
        gt_scr[slot] = gates.T[0:N_GATES, :]
        for p in range(D_PROJ // piece):
            cols = pl.ds(p * piece, piece)
            proj_scr[slot, :, cols] = _dot(xn_scr[...], w_ref[:, cols]).astype(BF16)
            yield

    def mixers(slot):
        group = lambda k: proj_scr.at[slot, :, pl.ds(k * D_GROUP, D_GROUP)]
        return [
            _mlstm_phases(group(0), group(1), group(2), group(3), gt_scr[slot], gain_m_ref,
                          mixm_ref, c_ref, n_ref, m_ref, chunk),
            _ret_phases(lg_ref, group(4), group(5), group(6), group(7), cos_ref, sin_ref,
                        gain_r_ref, mixr_ref, r_ref, chunk),
        ]

    @pl.when((t >= 1) & ((t - 1) % chunks_per_seq == 0))
    def _():
        c_ref[...] = jnp.zeros_like(c_ref)
        n_ref[...] = jnp.zeros_like(n_ref)
        m_ref[...] = jnp.zeros_like(m_ref)
        r_ref[...] = jnp.zeros_like(r_ref)

    @pl.when(t == 0)
    def _():
        _run(inproj_pieces(0))

    @pl.when((t >= 1) & (t < n_chunks))
    def _():
        slot = t % 2
        pieces = inproj_pieces(slot)
        heads = mixers(1 - slot)
        for _ in range(N_HEADS):
            for phases in heads:
                next(phases)
                next(pieces)
                next(phases)

    @pl.when(t == n_chunks)
    def _():
        _run(*mixers((n_chunks - 1) % 2))


def _prompt_front(x2d, w, rope, *, batch, seq):
    chunk = PROMPT_CHUNK
    chunks_per_seq = seq // chunk
    n_chunks = batch * chunks_per_seq
    cos, sin = rope
    const = lambda shape, **kw: pl.BlockSpec(shape, lambda t: (0,) * len(shape), **kw)
    mixed = lambda t: jnp.maximum(t - 1, 0)
    rope_spec = pl.BlockSpec((chunk, HEAD_DIM // 2), lambda t: (mixed(t) % chunks_per_seq, 0))
    mix_spec = pl.BlockSpec((chunk, D_GROUP), lambda t: (mixed(t), 0))
    mix_shape = jax.ShapeDtypeStruct((batch * seq, D_GROUP), BF16)
    seq_of = lambda t: mixed(t) // chunks_per_seq
    state4 = pl.BlockSpec((1, N_HEADS, HEAD_DIM, HEAD_DIM), lambda t: (seq_of(t), 0, 0, 0))
    state_specs = _mlstm_state_specs(lambda t: (seq_of(t), 0, 0), lambda t: (seq_of(t), 0, 0, 0))
    return pl.pallas_call(
        functools.partial(_prompt_kernel, n_chunks=n_chunks, chunks_per_seq=chunks_per_seq,
                          chunk=chunk),
        grid=(n_chunks + 1,),
        in_specs=[
            pl.BlockSpec(memory_space=pltpu.SMEM),
            pl.BlockSpec((chunk, D_MODEL), lambda t: (jnp.minimum(t, n_chunks - 1), 0)),
            const((1, D_MODEL)),
            const((D_MODEL, D_PROJ), pipeline_mode=pl.Buffered(1)),
            const((D_MODEL, LANES), pipeline_mode=pl.Buffered(1)),
            const((1, LANES)),
            const((1, D_GROUP)),
            rope_spec, rope_spec,
            const((1, D_GROUP)),
        ],
        out_specs=[mix_spec] + state_specs + [mix_spec, state4],
        out_shape=([mix_shape] + _mlstm_state_shapes(batch)
                   + [mix_shape, jax.ShapeDtypeStruct((batch, N_HEADS, HEAD_DIM, HEAD_DIM), F32)]),
        scratch_shapes=[
            pltpu.VMEM((chunk, D_MODEL), BF16),
            pltpu.VMEM((2, chunk, D_PROJ), BF16),
            pltpu.VMEM((2, N_GATES, chunk), F32),
        ],
        compiler_params=_params(1),
        name="prompt_front",
    )(w["log_gamma"], x2d, w["g1"], w["w_in"], w["w_gatecols"], w["b_gate"], w["gain_m"],
      cos, sin, w["gain_r"])


def _outproj_kernel(x_ref, mm_ref, mr_ref, wm_ref, wr_ref, g2_ref, x1_ref, hn_ref):
    x1 = x_ref[...] + _dot(mm_ref[...], wm_ref[...]) + _dot(mr_ref[...], wr_ref[...])
    x1_ref[...] = x1
    hn_ref[...] = (x1 * _rms_scale(x1) * g2_ref[...]).astype(BF16)


def _outproj(x, mix_m, mix_r, w_out, g2, *, tm):
    n = x.shape[0]
    w_spec = lambda half: pl.BlockSpec((D_GROUP, D_MODEL), lambda i, half=half: (half, 0),
                                       pipeline_mode=pl.Buffered(1))
    return pl.pallas_call(
        _outproj_kernel,
        grid=(n // tm,),
        in_specs=[
            pl.BlockSpec((tm, D_MODEL), lambda i: (i, 0)),
            pl.BlockSpec((tm, D_GROUP), lambda i: (i, 0)),
            pl.BlockSpec((tm, D_GROUP), lambda i: (i, 0)),
            w_spec(0), w_spec(1),
            pl.BlockSpec((1, D_MODEL), lambda i: (0, 0)),
        ],
        out_specs=[
            pl.BlockSpec((tm, D_MODEL), lambda i: (i, 0)),
            pl.BlockSpec((tm, D_MODEL), lambda i: (i, 0)),
        ],
        out_shape=[
            jax.ShapeDtypeStruct((n, D_MODEL), F32),
            jax.ShapeDtypeStruct((n, D_MODEL), BF16),
        ],
        compiler_params=_params(1),
        name="outproj",
    )(x, mix_m, mix_r, w_out, w_out, g2)


def _swiglu_down(hn, w_gu, w_down):
    gu = _dot(hn, w_gu)
    acts = []
    for base in range(0, w_gu.shape[1], 2 * FFN_GRANULE):
        gate = gu[:, base:base + FFN_GRANULE]
        up = gu[:, base + FFN_GRANULE:base + 2 * FFN_GRANULE]
        acts.append((jax.nn.silu(gate) * up).astype(BF16))
    act = acts[0] if len(acts) == 1 else jnp.concatenate(acts, axis=-1)
    return _dot(act, w_down)


def _ffn_kernel(*refs, pack):
    if pack:
        (hn_ref, x1_hbm, wg32_ref, wu32_ref, wd32_ref, gf_ref,
         y_ref, wgu_ref, wd_ref, x1_buf, x1_sem) = refs
    else:
        hn_ref, x1_hbm, wgu_ref, wd_ref, gf_ref, y_ref, x1_buf, x1_sem = refs
    i = pl.program_id(0)
    f = pl.program_id(1)
    last = pl.num_programs(1) - 1
    tm = x1_buf.shape[0]

    def x1_copy(tile):
        return pltpu.make_async_copy(x1_hbm.at[pl.ds(tile * tm, tm), :], x1_buf, x1_sem)

    @pl.when((f == 0) & (i == 0))
    def _():
        x1_copy(0).start()

    @pl.when(f == 0)
    def _():
        x1_copy(i).wait()

    @pl.when((f == 1) & (i + 1 < pl.num_programs(0)))
    def _():
        x1_copy(i + 1).start()

    def step(first, final):
        if pack:
            wgu_ref[:, :FFN_GRANULE] = wg32_ref[...].astype(BF16)
            wgu_ref[:, FFN_GRANULE:] = wu32_ref[...].astype(BF16)
            wd_ref[...] = wd32_ref[...].astype(BF16)
        part = _swiglu_down(hn_ref[...], wgu_ref[...], wd_ref[...])
        if first:
            y_ref[...] = x1_buf[...] + part
        else:
            y_ref[...] += part
        if final:
            x2 = y_ref[...]
            y_ref[...] = x2 * _rms_scale(x2) * gf_ref[...]

    pl.when(f == 0)(functools.partial(step, True, False))
    pl.when((f > 0) & (f < last))(functools.partial(step, False, False))
    pl.when(f == last)(functools.partial(step, False, True))


def _ffn(hn, x1, w, *, tm, tf):
    n = hn.shape[0]
    pack = "w_gu" not in w
    assert not pack or (n == tm and tf == FFN_GRANULE)
    assert D_FF // tf >= 3
    gu_spec = pl.BlockSpec((D_MODEL, 2 * tf), lambda i, f: (0, f))
    down_spec = pl.BlockSpec((tf, D_MODEL), lambda i, f: (f, 0))
    if pack:
        col_spec = pl.BlockSpec((D_MODEL, tf), lambda i, f: (0, f))
        w_specs, w_args = [col_spec, col_spec, down_spec], [w["w_gate"], w["w_up"], w["w_down"]]
    else:
        w_specs, w_args = [gu_spec, down_spec], [w["w_gu"], w["w_down"]]
    return pl.pallas_call(
        functools.partial(_ffn_kernel, pack=pack),
        grid=(n // tm, D_FF // tf),
        in_specs=[
            pl.BlockSpec((tm, D_MODEL), lambda i, f: (i, 0)),
            pl.BlockSpec(memory_space=pl.ANY),
            *w_specs,
            pl.BlockSpec((1, D_MODEL), lambda i, f: (0, 0)),
        ],
        out_specs=([pl.BlockSpec((tm, D_MODEL), lambda i, f: (i, 0))]
                   + [gu_spec, down_spec] * pack),
        out_shape=([jax.ShapeDtypeStruct((n, D_MODEL), F32)]
                   + [jax.ShapeDtypeStruct((D_MODEL, 2 * D_FF), BF16),
                      jax.ShapeDtypeStruct((D_FF, D_MODEL), BF16)] * pack),
        scratch_shapes=[pltpu.VMEM((tm, D_MODEL), F32), pltpu.SemaphoreType.DMA(())],
        compiler_params=_params(2),
        name="ffn",
    )(hn, x1, *w_args, w["g_final"])


def _rope_tables(pos):
    half = HEAD_DIM // 2
    freqs = ROPE_BASE ** (-jnp.arange(half, dtype=F32) / half)
    ang = pos[:, None] * freqs[None, :]
    return jnp.cos(ang), jnp.sin(ang)


def _tail(x2d, mix_m, mix_r, w, *, tf):
    x1, hn = _outproj(x2d, mix_m, mix_r, w["w_out"], w["g2"], tm=OUTPROJ_TM)
    return _ffn(hn, x1, w, tm=FFN_TM, tf=tf)


def _prompt_trunk(x, w):
    batch, seq, _ = x.shape
    x2d = x.reshape(batch * seq, D_MODEL)
    rope = _rope_tables(jnp.arange(seq, dtype=F32))
    mix_m, c_new, n_new, m_new, mix_r, r_new = _prompt_front(x2d, w, rope, batch=batch, seq=seq)
    (y,) = _tail(x2d, mix_m, mix_r, w, tf=FFN_TF)
    return (y.reshape(batch, seq, D_MODEL), c_new[None], n_new[None],
            m_new[:, :N_HEADS, 0][None], r_new[None])


def _sample_trunk(x, init, w):
    batch, seq, _ = x.shape
    x2d = x.reshape(batch * seq, D_MODEL)
    proj, gt = _inproj(x2d, w["g1"], w["w_in"], w["w_gatecols"], w["b_gate"],
                       tm=INPROJ_TM, tn=INPROJ_TN)
    gt3 = gt.reshape(N_GATES, batch, seq).transpose(1, 0, 2)
    gt3 = jnp.pad(gt3, ((0, 0), (0, 0), (0, max(0, LANES - seq))))
    c0, n0, m0, r0 = init
    m0 = jnp.pad(m0, ((0, 0), (0, SUBLANES - N_HEADS)))
    m0 = jnp.broadcast_to(m0[:, :, None], (batch, SUBLANES, LANES))
    mix_m, c_new, n_new, m_new = _mlstm_single_chunk(proj, gt3, w["gain_m"], (c0, n0, m0),
                                                     batch=batch, seq=seq, group=SAMPLE_GROUP)
    cos, sin = _rope_tables(PAST_LEN + jnp.arange(seq, dtype=F32))
    mix_r, r_new = _retention_single_chunk(proj, w["log_gamma"], cos, sin, w["gain_r"], r0,
                                           batch=batch, seq=seq, group=SAMPLE_GROUP)
    y, w_gu16, w_down16 = _tail(x2d, mix_m, mix_r, w, tf=FFN_GRANULE)
    w16 = dict(w_gu=w_gu16, w_down=w_down16)
    return (y.reshape(batch, seq, D_MODEL), c_new[None], n_new[None],
            m_new[:, :N_HEADS, 0][None], r_new[None]), w16


def kernel(x_prompt, x_sample, state_mlstm_C, state_mlstm_n, state_mlstm_m, state_ret,
           g_norm1, w_in, b_gates, g_mlstm_norm, g_ret_norm, w_out, g_norm2,
           w_gate, w_up, w_down, g_final):
    w_in0 = w_in[0].astype(BF16)
    w = dict(
        g1=g_norm1[0].reshape(1, D_MODEL),
        w_in=w_in0,
        w_gatecols=jnp.pad(w_in0[:, D_PROJ:], ((0, 0), (0, LANES - N_GATES))),
        b_gate=jnp.pad(b_gates[0], (0, LANES - N_GATES)).reshape(1, LANES),
        gain_m=g_mlstm_norm[0].reshape(1, D_GROUP),
        gain_r=g_ret_norm[0].reshape(1, D_GROUP),
        w_out=w_out[0].astype(BF16),
        g2=g_norm2[0].reshape(1, D_MODEL),
        w_gate=w_gate[0],
        w_up=w_up[0],
        w_down=w_down[0],
        g_final=g_final.reshape(1, D_MODEL),
        log_gamma=jnp.log(1.0 - jnp.exp2(-5.0 - jnp.arange(N_HEADS, dtype=F32))),
    )
    init_s = (state_mlstm_C[0], state_mlstm_n[0], state_mlstm_m[0], state_ret[0])
    (y_s, c_s, n_s, m_s, r_s), w16 = _sample_trunk(x_sample, init_s, w)
    y_p, c_p, n_p, m_p, r_p = _prompt_trunk(x_prompt, {**w, **w16})
    return (y_p, y_s, c_p, n_p, m_p, r_p, c_s, n_s, m_s, r_s)
```

```python
import functools

import jax
import jax.numpy as jnp
from jax import lax
from jax.experimental import pallas as pl
from jax.experimental.pallas import tpu as pltpu

F32 = jnp.float32
BF16 = jnp.bfloat16

D_MODEL = 2048
N_HEADS = 4
HEAD_DIM = 256
D_GROUP = N_HEADS * HEAD_DIM
D_PROJ = 8 * D_GROUP
N_GATES = 2 * N_HEADS
D_FF = 5632
ROPE_BASE = 10000.0
PAST_LEN = 4096
EPS = 1e-6
K_SCALE = HEAD_DIM ** -0.5

LANES = 128
SUBLANES = 8
VMEM_LIMIT = 60 * 1024 * 1024

PROMPT_CHUNK = 256
INPROJ_TM = 1024
INPROJ_TN = 2048
SAMPLE_GROUP = 4
OUTPROJ_TM = 512
FFN_TM = 1024
FFN_TF = 512
FFN_GRANULE = 256

_NT = (((1,), (1,)), ((), ()))
_TN = (((0,), (0,)), ((), ()))


def _dot(a, b):
    return jnp.dot(a, b, preferred_element_type=F32)


def _rms_scale(x):
    return lax.rsqrt(jnp.mean(x * x, axis=-1, keepdims=True) + EPS)


def _params(n_axes):
    return pltpu.CompilerParams(
        dimension_semantics=("arbitrary",) * n_axes, vmem_limit_bytes=VMEM_LIMIT)


def _inproj_prologue(x_ref, g1_ref, wgate_ref, bgate_ref, gt_ref, xn_ref):
    x = x_ref[...]
    xn = (x * _rms_scale(x) * g1_ref[...]).astype(BF16)
    xn_ref[...] = xn
    gates = _dot(xn, wgate_ref[...]) + bgate_ref[...]
    gt_ref[...] = gates.T[0:N_GATES, :]


def _inproj_kernel(x_ref, g1_ref, w_ref, wgate_ref, bgate_ref, proj_ref, gt_ref, xn_ref):
    @pl.when(pl.program_id(1) == 0)
    def _():
        _inproj_prologue(x_ref, g1_ref, wgate_ref, bgate_ref, gt_ref, xn_ref)

    proj_ref[...] = _dot(xn_ref[...], w_ref[...]).astype(BF16)


def _inproj(x, g1, w_main, w_gate, b_gate, *, tm, tn):
    n = x.shape[0]
    return pl.pallas_call(
        _inproj_kernel,
        grid=(n // tm, D_PROJ // tn),
        in_specs=[
            pl.BlockSpec((tm, D_MODEL), lambda i, j: (i, 0)),
            pl.BlockSpec((1, D_MODEL), lambda i, j: (0, 0)),
            pl.BlockSpec((D_MODEL, tn), lambda i, j: (0, j)),
            pl.BlockSpec((D_MODEL, LANES), lambda i, j: (0, 0)),
            pl.BlockSpec((1, LANES), lambda i, j: (0, 0)),
        ],
        out_specs=[
            pl.BlockSpec((tm, tn), lambda i, j: (i, j)),
            pl.BlockSpec((N_GATES, tm), lambda i, j: (0, i)),
        ],
        out_shape=[
            jax.ShapeDtypeStruct((n, D_PROJ), BF16),
            jax.ShapeDtypeStruct((N_GATES, n), F32),
        ],
        scratch_shapes=[pltpu.VMEM((tm, D_MODEL), BF16)],
        compiler_params=_params(2),
        name="inproj",
    )(x, g1, w_main, w_gate, b_gate)


def _lane_scan(x, combine, identity):
    width = x.shape[1]
    lane = lax.broadcasted_iota(jnp.int32, x.shape, 1)
    shift = 1
    while shift < width:
        shifted = jnp.where(lane >= shift, pltpu.roll(x, shift, axis=1), identity)
        x = combine(x, shifted)
        shift *= 2
    return x


def _mlstm_phases(q_ref, k_ref, v_ref, o_ref, gates, gain_ref, mix_ref, c_ref, n_ref, m_ref, L):
    b = _lane_scan(jax.nn.log_sigmoid(gates), jnp.add, 0.0)
    b = pltpu.roll(b, N_HEADS, axis=0)
    u = gates - b
    m_prev = m_ref[0][:, 0:1]
    big_m = jnp.maximum(m_prev, _lane_scan(u, jnp.maximum, -jnp.inf))
    a = jnp.exp(m_prev - big_m)
    e = jnp.exp(-(b + big_m))
    m_last = big_m[:, L - 1:L]
    g = jnp.exp(u - m_last)
    dec = a[:, L - 1:L]
    m_new = b[:, L - 1:L] + m_last
    rows = jnp.concatenate(
        [big_m, a, e, g, jnp.zeros((LANES - 4 * SUBLANES, gates.shape[1]), F32)], axis=0)
    cols = rows.T

    t_idx = lax.broadcasted_iota(jnp.int32, (L, L), 0)
    s_idx = lax.broadcasted_iota(jnp.int32, (L, L), 1)
    causal = s_idx <= t_idx

    for h in range(N_HEADS):
        sl = slice(h * HEAD_DIM, (h + 1) * HEAD_DIM)
        qh = q_ref[:, sl]
        kh = k_ref[:, sl] * jnp.asarray(K_SCALE, BF16)
        vh = v_ref[:, sl]
        m_col = cols[:L, h:h + 1]
        a_col = cols[:L, SUBLANES + h:SUBLANES + h + 1]
        e_col = cols[:L, 2 * SUBLANES + h:2 * SUBLANES + h + 1]
        g_col = cols[:L, 3 * SUBLANES + h:3 * SUBLANES + h + 1]
        u_row = u[h:h + 1, :L]

        s = lax.dot_general(qh, kh, _NT, preferred_element_type=F32)
        yield
        sw = s * jnp.exp(jnp.where(causal, u_row - m_col, -jnp.inf))
        c_h = c_ref[0, h]
        n_h = n_ref[0, h:h + 1, :]
        num = a_col * _dot(qh, c_h.astype(BF16)) + _dot(sw.astype(BF16), vh)
        qn = jnp.sum(qh.astype(F32) * n_h, axis=-1, keepdims=True)
        den = a_col * qn + jnp.sum(sw, axis=-1, keepdims=True)
        hh = num * (1.0 / jnp.maximum(jnp.abs(den), e_col))
        y = hh * _rms_scale(hh) * gain_ref[:, sl] * jax.nn.sigmoid(o_ref[:, sl].astype(F32))
        mix_ref[:, sl] = y.astype(BF16)

        kg = kh.astype(F32) * g_col
        dec_h = dec[h:h + 1, :]
        c_ref[0, h] = dec_h * c_h + lax.dot_general(
            kg.astype(BF16), vh, _TN, preferred_element_type=F32)
        n_ref[0, h:h + 1, :] = dec_h * n_h + jnp.sum(kg, axis=0, keepdims=True)
        if h == N_HEADS - 1:
            m_ref[0] = jnp.broadcast_to(m_new, (SUBLANES, LANES))
        yield


def _run(*phase_generators):
    live = list(phase_generators)
    while live:
        live = [g for g in live if next(g, StopIteration) is not StopIteration]


def _seq_views(s, rows, token_refs, state_refs):
    return ([ref.at[pl.ds(s * rows, rows), :] for ref in token_refs],
            [ref.at[pl.ds(s, 1)] for ref in state_refs])


def _mlstm_state_specs(index_map3, index_map4, group=1):
    return [
        pl.BlockSpec((group, N_HEADS, HEAD_DIM, HEAD_DIM), index_map4),
        pl.BlockSpec((group, N_HEADS, HEAD_DIM), index_map3),
        pl.BlockSpec((group, SUBLANES, LANES), index_map3),
    ]


def _mlstm_state_shapes(batch):
    return [
        jax.ShapeDtypeStruct((batch, N_HEADS, HEAD_DIM, HEAD_DIM), F32),
        jax.ShapeDtypeStruct((batch, N_HEADS, HEAD_DIM), F32),
        jax.ShapeDtypeStruct((batch, SUBLANES, LANES), F32),
    ]


def _mlstm_kernel(q_ref, k_ref, v_ref, o_ref, gt_ref, gain_ref, c0_ref, n0_ref, m0_ref,
                  mix_ref, c_ref, n_ref, m_ref, *, chunk, group):
    c_ref[...] = c0_ref[...]
    n_ref[...] = n0_ref[...]
    m_ref[...] = m0_ref[...]
    phases = []
    for s in range(group):
        (q, k, v, o, mix), (c, n, m) = _seq_views(
            s, chunk, (q_ref, k_ref, v_ref, o_ref, mix_ref), (c_ref, n_ref, m_ref))
        phases.append(_mlstm_phases(q, k, v, o, gt_ref[s], gain_ref, mix, c, n, m, chunk))
    _run(*phases)


def _mlstm_single_chunk(proj, gt3, gain, init, *, batch, seq, group):
    tok = lambda col: pl.BlockSpec((group * seq, D_GROUP), lambda b, col=col: (b, col))
    state_specs = _mlstm_state_specs(lambda b: (b, 0, 0), lambda b: (b, 0, 0, 0), group)
    return pl.pallas_call(
        functools.partial(_mlstm_kernel, chunk=seq, group=group),
        grid=(batch // group,),
        in_specs=[tok(0), tok(1), tok(2), tok(3),
                  pl.BlockSpec((group, SUBLANES, gt3.shape[2]), lambda b: (b, 0, 0)),
                  pl.BlockSpec((1, D_GROUP), lambda b: (0, 0))] + state_specs,
        out_specs=[pl.BlockSpec((group * seq, D_GROUP), lambda b: (b, 0))] + state_specs,
        out_shape=[jax.ShapeDtypeStruct((batch * seq, D_GROUP), BF16)] + _mlstm_state_shapes(batch),
        compiler_params=_params(1),
        name="mlstm",
    )(proj, proj, proj, proj, gt3, gain, *init)


def _ret_phases(lg_ref, q_ref, k_ref, v_ref, g_ref, cos_ref, sin_ref, gain_ref, mix_ref, r_ref, L):
    half = HEAD_DIM // 2
    cos = cos_ref[...]
    sin = sin_ref[...]
    cos_k = cos * K_SCALE
    sin_k = sin * K_SCALE
    t_idx = lax.broadcasted_iota(jnp.int32, (L, L), 0)
    s_idx = lax.broadcasted_iota(jnp.int32, (L, L), 1)
    diff = t_idx - s_idx
    diff_pos = jnp.maximum(diff, 0).astype(F32)
    t_col = lax.broadcasted_iota(jnp.int32, (L, 1), 0).astype(F32)

    for h in range(N_HEADS):
        lo = slice(h * HEAD_DIM, h * HEAD_DIM + half)
        hi = slice(h * HEAD_DIM + half, (h + 1) * HEAD_DIM)
        sl = slice(h * HEAD_DIM, (h + 1) * HEAD_DIM)
        lg = lg_ref[h]
        q1 = q_ref[:, lo].astype(F32)
        q2 = q_ref[:, hi].astype(F32)
        k1 = k_ref[:, lo].astype(F32)
        k2 = k_ref[:, hi].astype(F32)
        qr = jnp.concatenate([q1 * cos - q2 * sin, q2 * cos + q1 * sin], axis=-1).astype(BF16)
        kr = jnp.concatenate([k1 * cos_k - k2 * sin_k, k2 * cos_k + k1 * sin_k], axis=-1)
        vh = v_ref[:, sl]

        dmask = jnp.where(diff >= 0, jnp.exp(lg * diff_pos), 0.0)
        inter = jnp.exp(lg * (t_col + 1.0))
        kdec = jnp.exp(lg * (L - 1.0 - t_col))
        cdec = jnp.exp(lg * jnp.full((1, 1), L, F32))

        s = lax.dot_general(qr, kr.astype(BF16), _NT, preferred_element_type=F32)
        yield
        r_h = r_ref[0, h]
        o = _dot((s * dmask).astype(BF16), vh) + inter * _dot(qr, r_h.astype(BF16))
        r_ref[0, h] = cdec * r_h + lax.dot_general(
            (kr * kdec).astype(BF16), vh, _TN, preferred_element_type=F32)

        oc = o - jnp.mean(o, axis=-1, keepdims=True)
        y = oc * _rms_scale(oc) * gain_ref[:, sl] * jax.nn.silu(g_ref[:, sl].astype(F32))
        mix_ref[:, sl] = y.astype(BF16)
        yield


def _ret_kernel(lg_ref, q_ref, k_ref, v_ref, g_ref, cos_ref, sin_ref, gain_ref, r0_ref,
                mix_ref, r_ref, *, chunk, group):
    r_ref[...] = r0_ref[...]
    phases = []
    for s in range(group):
        (q, k, v, g, mix), (r,) = _seq_views(
            s, chunk, (q_ref, k_ref, v_ref, g_ref, mix_ref), (r_ref,))
        phases.append(_ret_phases(lg_ref, q, k, v, g, cos_ref, sin_ref, gain_ref, mix, r, chunk))
    _run(*phases)


def _retention_single_chunk(proj, log_gamma, cos, sin, gain, init, *, batch, seq, group):
    tok = lambda col: pl.BlockSpec((group * seq, D_GROUP), lambda b, col=col: (b, col))
    state_spec = pl.BlockSpec((group, N_HEADS, HEAD_DIM, HEAD_DIM), lambda b: (b, 0, 0, 0))
    rope_spec = pl.BlockSpec((seq, HEAD_DIM // 2), lambda b: (0, 0))
    return pl.pallas_call(
        functools.partial(_ret_kernel, chunk=seq, group=group),
        grid=(batch // group,),
        in_specs=[pl.BlockSpec(memory_space=pltpu.SMEM),
                  tok(4), tok(5), tok(6), tok(7), rope_spec, rope_spec,
                  pl.BlockSpec((1, D_GROUP), lambda b: (0, 0)), state_spec],
        out_specs=[pl.BlockSpec((group * seq, D_GROUP), lambda b: (b, 0)), state_spec],
        out_shape=[
            jax.ShapeDtypeStruct((batch * seq, D_GROUP), BF16),
            jax.ShapeDtypeStruct((batch, N_HEADS, HEAD_DIM, HEAD_DIM), F32),
        ],
        compiler_params=_params(1),
        name="retention",
    )(log_gamma, proj, proj, proj, proj, cos, sin, gain, init)


def _prompt_kernel(lg_ref, x_ref, g1_ref, w_ref, wgate_ref, bgate_ref, gain_m_ref, cos_ref,
                   sin_ref, gain_r_ref, mixm_ref, c_ref, n_ref, m_ref, mixr_ref, r_ref,
                   xn_scr, proj_scr, gt_scr, *, n_chunks, chunks_per_seq, chunk):
    t = pl.program_id(0)
    piece = D_PROJ // (2 * N_HEADS)

    def inproj_pieces(slot):
        x = x_ref[...]
        xn_scr[...] = (x * _rms_scale(x) * g1_ref[...]).astype(BF16)
        gates = _dot(xn_scr[...], wgate_ref[...]) + bgate_ref[...]
        gt_scr[slot] = gates.T[0:N_GATES, :]
        for p in range(D_PROJ // piece):
            cols = pl.ds(p * piece, piece)
            proj_scr[slot, :, cols] = _dot(xn_scr[...], w_ref[:, cols]).astype(BF16)
            yield

    def mixers(slot):
        group = lambda k: proj_scr.at[slot, :, pl.ds(k * D_GROUP, D_GROUP)]
        return [
            _mlstm_phases(group(0), group(1), group(2), group(3), gt_scr[slot], gain_m_ref,
                          mixm_ref, c_ref, n_ref, m_ref, chunk),
            _ret_phases(lg_ref, group(4), group(5), group(6), group(7), cos_ref, sin_ref,
                        gain_r_ref, mixr_ref, r_ref, chunk),
        ]

    @pl.when((t >= 1) & ((t - 1) % chunks_per_seq == 0))
    def _():
        c_ref[...] = jnp.zeros_like(c_ref)
        n_ref[...] = jnp.zeros_like(n_ref)
        m_ref[...] = jnp.zeros_like(m_ref)
        r_ref[...] = jnp.zeros_like(r_ref)

    @pl.when(t == 0)
    def _():
        _run(inproj_pieces(0))

    @pl.when((t >= 1) & (t < n_chunks))
    def _():
        slot = t % 2
        pieces = inproj_pieces(slot)
        heads = mixers(1 - slot)
        for _ in range(N_HEADS):
            for phases in heads:
                next(phases)
                next(pieces)
                next(phases)

    @pl.when(t == n_chunks)
    def _():
        _run(*mixers((n_chunks - 1) % 2))


def _prompt_front(x2d, w, rope, *, batch, seq):
    chunk = PROMPT_CHUNK
    chunks_per_seq = seq // chunk
    n_chunks = batch * chunks_per_seq
    cos, sin = rope
    const = lambda shape, **kw: pl.BlockSpec(shape, lambda t: (0,) * len(shape), **kw)
    mixed = lambda t: jnp.maximum(t - 1, 0)
    rope_spec = pl.BlockSpec((chunk, HEAD_DIM // 2), lambda t: (mixed(t) % chunks_per_seq, 0))
    mix_spec = pl.BlockSpec((chunk, D_GROUP), lambda t: (mixed(t), 0))
    mix_shape = jax.ShapeDtypeStruct((batch * seq, D_GROUP), BF16)
    seq_of = lambda t: mixed(t) // chunks_per_seq
    state4 = pl.BlockSpec((1, N_HEADS, HEAD_DIM, HEAD_DIM), lambda t: (seq_of(t), 0, 0, 0))
    state_specs = _mlstm_state_specs(lambda t: (seq_of(t), 0, 0), lambda t: (seq_of(t), 0, 0, 0))
    return pl.pallas_call(
        functools.partial(_prompt_kernel, n_chunks=n_chunks, chunks_per_seq=chunks_per_seq,
                          chunk=chunk),
        grid=(n_chunks + 1,),
        in_specs=[
            pl.BlockSpec(memory_space=pltpu.SMEM),
            pl.BlockSpec((chunk, D_MODEL), lambda t: (jnp.minimum(t, n_chunks - 1), 0)),
            const((1, D_MODEL)),
            const((D_MODEL, D_PROJ), pipeline_mode=pl.Buffered(1)),
            const((D_MODEL, LANES), pipeline_mode=pl.Buffered(1)),
            const((1, LANES)),
            const((1, D_GROUP)),
            rope_spec, rope_spec,
            const((1, D_GROUP)),
        ],
        out_specs=[mix_spec] + state_specs + [mix_spec, state4],
        out_shape=([mix_shape] + _mlstm_state_shapes(batch)
                   + [mix_shape, jax.ShapeDtypeStruct((batch, N_HEADS, HEAD_DIM, HEAD_DIM), F32)]),
        scratch_shapes=[
            pltpu.VMEM((chunk, D_MODEL), BF16),
            pltpu.VMEM((2, chunk, D_PROJ), BF16),
            pltpu.VMEM((2, N_GATES, chunk), F32),
        ],
        compiler_params=_params(1),
        name="prompt_front",
    )(w["log_gamma"], x2d, w["g1"], w["w_in"], w["w_gatecols"], w["b_gate"], w["gain_m"],
      cos, sin, w["gain_r"])


def _outproj_kernel(x_ref, mm_ref, mr_ref, wm_ref, wr_ref, g2_ref, x1_ref, hn_ref):
    x1 = x_ref[...] + _dot(mm_ref[...], wm_ref[...]) + _dot(mr_ref[...], wr_ref[...])
    x1_ref[...] = x1
    hn_ref[...] = (x1 * _rms_scale(x1) * g2_ref[...]).astype(BF16)


def _outproj(x, mix_m, mix_r, w_out, g2, *, tm):
    n = x.shape[0]
    w_spec = lambda half: pl.BlockSpec((D_GROUP, D_MODEL), lambda i, half=half: (half, 0),
                                       pipeline_mode=pl.Buffered(1))
    return pl.pallas_call(
        _outproj_kernel,
        grid=(n // tm,),
        in_specs=[
            pl.BlockSpec((tm, D_MODEL), lambda i: (i, 0)),
            pl.BlockSpec((tm, D_GROUP), lambda i: (i, 0)),
            pl.BlockSpec((tm, D_GROUP), lambda i: (i, 0)),
            w_spec(0), w_spec(1),
            pl.BlockSpec((1, D_MODEL), lambda i: (0, 0)),
        ],
        out_specs=[
            pl.BlockSpec((tm, D_MODEL), lambda i: (i, 0)),
            pl.BlockSpec((tm, D_MODEL), lambda i: (i, 0)),
        ],
        out_shape=[
            jax.ShapeDtypeStruct((n, D_MODEL), F32),
            jax.ShapeDtypeStruct((n, D_MODEL), BF16),
        ],
        compiler_params=_params(1),
        name="outproj",
    )(x, mix_m, mix_r, w_out, w_out, g2)


def _swiglu_down(hn, w_gu, w_down):
    gu = _dot(hn, w_gu)
    acts = []
    for base in range(0, w_gu.shape[1], 2 * FFN_GRANULE):
        gate = gu[:, base:base + FFN_GRANULE]
        up = gu[:, base + FFN_GRANULE:base + 2 * FFN_GRANULE]
        acts.append((jax.nn.silu(gate) * up).astype(BF16))
    act = acts[0] if len(acts) == 1 else jnp.concatenate(acts, axis=-1)
    return _dot(act, w_down)


def _ffn_kernel(*refs, pack):
    if pack:
        (hn_ref, x1_hbm, wg32_ref, wu32_ref, wd32_ref, gf_ref,
         y_ref, wgu_ref, wd_ref, x1_buf, x1_sem) = refs
    else:
        hn_ref, x1_hbm, wgu_ref, wd_ref, gf_ref, y_ref, x1_buf, x1_sem = refs
    i = pl.program_id(0)
    f = pl.program_id(1)
    last = pl.num_programs(1) - 1
    tm = x1_buf.shape[0]

    def x1_copy(tile):
        return pltpu.make_async_copy(x1_hbm.at[pl.ds(tile * tm, tm), :], x1_buf, x1_sem)

    @pl.when((f == 0) & (i == 0))
    def _():
        x1_copy(0).start()

    @pl.when(f == 0)
    def _():
        x1_copy(i).wait()

    @pl.when((f == 1) & (i + 1 < pl.num_programs(0)))
    def _():
        x1_copy(i + 1).start()

    def step(first, final):
        if pack:
            wgu_ref[:, :FFN_GRANULE] = wg32_ref[...].astype(BF16)
            wgu_ref[:, FFN_GRANULE:] = wu32_ref[...].astype(BF16)
            wd_ref[...] = wd32_ref[...].astype(BF16)
        part = _swiglu_down(hn_ref[...], wgu_ref[...], wd_ref[...])
        if first:
            y_ref[...] = x1_buf[...] + part
        else:
            y_ref[...] += part
        if final:
            x2 = y_ref[...]
            y_ref[...] = x2 * _rms_scale(x2) * gf_ref[...]

    pl.when(f == 0)(functools.partial(step, True, False))
    pl.when((f > 0) & (f < last))(functools.partial(step, False, False))
    pl.when(f == last)(functools.partial(step, False, True))


def _ffn(hn, x1, w, *, tm, tf):
    n = hn.shape[0]
    pack = "w_gu" not in w
    assert not pack or (n == tm and tf == FFN_GRANULE)
    assert D_FF // tf >= 3
    gu_spec = pl.BlockSpec((D_MODEL, 2 * tf), lambda i, f: (0, f))
    down_spec = pl.BlockSpec((tf, D_MODEL), lambda i, f: (f, 0))
    if pack:
        col_spec = pl.BlockSpec((D_MODEL, tf), lambda i, f: (0, f))
        w_specs, w_args = [col_spec, col_spec, down_spec], [w["w_gate"], w["w_up"], w["w_down"]]
    else:
        w_specs, w_args = [gu_spec, down_spec], [w["w_gu"], w["w_down"]]
    return pl.pallas_call(
        functools.partial(_ffn_kernel, pack=pack),
        grid=(n // tm, D_FF // tf),
        in_specs=[
            pl.BlockSpec((tm, D_MODEL), lambda i, f: (i, 0)),
            pl.BlockSpec(memory_space=pl.ANY),
            *w_specs,
            pl.BlockSpec((1, D_MODEL), lambda i, f: (0, 0)),
        ],
        out_specs=([pl.BlockSpec((tm, D_MODEL), lambda i, f: (i, 0))]
                   + [gu_spec, down_spec] * pack),
        out_shape=([jax.ShapeDtypeStruct((n, D_MODEL), F32)]
                   + [jax.ShapeDtypeStruct((D_MODEL, 2 * D_FF), BF16),
                      jax.ShapeDtypeStruct((D_FF, D_MODEL), BF16)] * pack),
        scratch_shapes=[pltpu.VMEM((tm, D_MODEL), F32), pltpu.SemaphoreType.DMA(())],
        compiler_params=_params(2),
        name="ffn",
    )(hn, x1, *w_args, w["g_final"])


def _ffn_stream_kernel(hn_ref, x1_hbm, wgu_hbm, wd_hbm, gf_ref, y_ref,
                       x1_buf, wgu_buf, wd_buf, x1_sem, w_sems, *, n_f, tf):
    i = pl.program_id(0)
    n_i = pl.num_programs(0)
    tm = x1_buf.shape[0]

    def x1_copy(tile):
        return pltpu.make_async_copy(x1_hbm.at[pl.ds(tile * tm, tm), :], x1_buf, x1_sem)

    def w_copies(f, slot):
        col, row = f * (2 * tf), f * tf
        if not isinstance(f, int):
            col, row = pl.multiple_of(col, 2 * tf), pl.multiple_of(row, tf)
        return (
            pltpu.make_async_copy(wgu_hbm.at[:, pl.ds(col, 2 * tf)], wgu_buf.at[slot],
                                  w_sems.at[0, slot]),
            pltpu.make_async_copy(wd_hbm.at[pl.ds(row, tf), :], wd_buf.at[slot],
                                  w_sems.at[1, slot]),
        )

    @pl.when(i == 0)
    def _():
        x1_copy(0).start()
        for copy in w_copies(0, 0):
            copy.start()

    def iteration(f, first, final):
        slot = (i * n_f + f) % 2
        for copy in w_copies(f, slot):
            copy.wait()
        if final:
            @pl.when(i + 1 < n_i)
            def _():
                for copy in w_copies(0, 1 - slot):
                    copy.start()
        else:
            for copy in w_copies(f + 1, 1 - slot):
                copy.start()
        if first:
            x1_copy(i).wait()
        part = _swiglu_down(hn_ref[...], wgu_buf[slot], wd_buf[slot])
        if first:
            y_ref[...] = x1_buf[...] + part
        else:
            y_ref[...] += part
        if final:
            x2 = y_ref[...]
            y_ref[...] = x2 * _rms_scale(x2) * gf_ref[...]

    def middle(f, carry):
        @pl.when((f == 1) & (i + 1 < n_i))
        def _():
            x1_copy(i + 1).start()

        iteration(f, False, False)
        return carry

    iteration(0, True, False)
    lax.fori_loop(1, n_f - 1, middle, 0)
    iteration(n_f - 1, False, True)


def _ffn_stream(hn, x1, w, *, tm, tf):
    n = hn.shape[0]
    n_f = D_FF // tf
    assert n_f >= 3
    return pl.pallas_call(
        functools.partial(_ffn_stream_kernel, n_f=n_f, tf=tf),
        grid=(n // tm,),
        in_specs=[
            pl.BlockSpec((tm, D_MODEL), lambda i: (i, 0)),
            pl.BlockSpec(memory_space=pl.ANY),
            pl.BlockSpec(memory_space=pl.ANY),
            pl.BlockSpec(memory_space=pl.ANY),
            pl.BlockSpec((1, D_MODEL), lambda i: (0, 0)),
        ],
        out_specs=pl.BlockSpec((tm, D_MODEL), lambda i: (i, 0)),
        out_shape=jax.ShapeDtypeStruct((n, D_MODEL), F32),
        scratch_shapes=[
            pltpu.VMEM((tm, D_MODEL), F32),
            pltpu.VMEM((2, D_MODEL, 2 * tf), BF16),
            pltpu.VMEM((2, tf, D_MODEL), BF16),
            pltpu.SemaphoreType.DMA(()),
            pltpu.SemaphoreType.DMA((2, 2)),
        ],
        compiler_params=_params(1),
        name="ffn_stream",
    )(hn, x1, w["w_gu"], w["w_down"], w["g_final"])


def _rope_tables(pos):
    half = HEAD_DIM // 2
    freqs = ROPE_BASE ** (-jnp.arange(half, dtype=F32) / half)
    ang = pos[:, None] * freqs[None, :]
    return jnp.cos(ang), jnp.sin(ang)


def _tail(x2d, mix_m, mix_r, w, *, tf):
    x1, hn = _outproj(x2d, mix_m, mix_r, w["w_out"], w["g2"], tm=OUTPROJ_TM)
    if "w_gu" in w:
        return [_ffn_stream(hn, x1, w, tm=FFN_TM, tf=tf)]
    return _ffn(hn, x1, w, tm=FFN_TM, tf=tf)


def _prompt_trunk(x, w):
    batch, seq, _ = x.shape
    x2d = x.reshape(batch * seq, D_MODEL)
    rope = _rope_tables(jnp.arange(seq, dtype=F32))
    mix_m, c_new, n_new, m_new, mix_r, r_new = _prompt_front(x2d, w, rope, batch=batch, seq=seq)
    (y,) = _tail(x2d, mix_m, mix_r, w, tf=FFN_TF)
    return (y.reshape(batch, seq, D_MODEL), c_new[None], n_new[None],
            m_new[:, :N_HEADS, 0][None], r_new[None])


def _sample_trunk(x, init, w):
    batch, seq, _ = x.shape
    x2d = x.reshape(batch * seq, D_MODEL)
    proj, gt = _inproj(x2d, w["g1"], w["w_in"], w["w_gatecols"], w["b_gate"],
                       tm=INPROJ_TM, tn=INPROJ_TN)
    gt3 = gt.reshape(N_GATES, batch, seq).transpose(1, 0, 2)
    gt3 = jnp.pad(gt3, ((0, 0), (0, 0), (0, max(0, LANES - seq))))
    c0, n0, m0, r0 = init
    m0 = jnp.pad(m0, ((0, 0), (0, SUBLANES - N_HEADS)))
    m0 = jnp.broadcast_to(m0[:, :, None], (batch, SUBLANES, LANES))
    mix_m, c_new, n_new, m_new = _mlstm_single_chunk(proj, gt3, w["gain_m"], (c0, n0, m0),
                                                     batch=batch, seq=seq, group=SAMPLE_GROUP)
    cos, sin = _rope_tables(PAST_LEN + jnp.arange(seq, dtype=F32))
    mix_r, r_new = _retention_single_chunk(proj, w["log_gamma"], cos, sin, w["gain_r"], r0,
                                           batch=batch, seq=seq, group=SAMPLE_GROUP)
    y, w_gu16, w_down16 = _tail(x2d, mix_m, mix_r, w, tf=FFN_GRANULE)
    w16 = dict(w_gu=w_gu16, w_down=w_down16)
    return (y.reshape(batch, seq, D_MODEL), c_new[None], n_new[None],
            m_new[:, :N_HEADS, 0][None], r_new[None]), w16


def kernel(x_prompt, x_sample, state_mlstm_C, state_mlstm_n, state_mlstm_m, state_ret,
           g_norm1, w_in, b_gates, g_mlstm_norm, g_ret_norm, w_out, g_norm2,
           w_gate, w_up, w_down, g_final):
    w_in0 = w_in[0].astype(BF16)
    w = dict(
        g1=g_norm1[0].reshape(1, D_MODEL),
        w_in=w_in0,
        w_gatecols=jnp.pad(w_in0[:, D_PROJ:], ((0, 0), (0, LANES - N_GATES))),
        b_gate=jnp.pad(b_gates[0], (0, LANES - N_GATES)).reshape(1, LANES),
        gain_m=g_mlstm_norm[0].reshape(1, D_GROUP),
        gain_r=g_ret_norm[0].reshape(1, D_GROUP),
        w_out=w_out[0].astype(BF16),
        g2=g_norm2[0].reshape(1, D_MODEL),
        w_gate=w_gate[0],
        w_up=w_up[0],
        w_down=w_down[0],
        g_final=g_final.reshape(1, D_MODEL),
        log_gamma=jnp.log(1.0 - jnp.exp2(-5.0 - jnp.arange(N_HEADS, dtype=F32))),
    )
    init_s = (state_mlstm_C[0], state_mlstm_n[0], state_mlstm_m[0], state_ret[0])
    (y_s, c_s, n_s, m_s, r_s), w16 = _sample_trunk(x_sample, init_s, w)
    y_p, c_p, n_p, m_p, r_p = _prompt_trunk(x_prompt, {**w, **w16})
    return (y_p, y_s, c_p, n_p, m_p, r_p, c_s, n_s, m_s, r_s)
```

```python
import functools

import jax
import jax.numpy as jnp
from jax import lax
from jax.experimental import pallas as pl
from jax.experimental.pallas import tpu as pltpu

F32 = jnp.float32
BF16 = jnp.bfloat16

D_MODEL = 2048
N_HEADS = 4
HEAD_DIM = 256
D_GROUP = N_HEADS * HEAD_DIM
D_PROJ = 8 * D_GROUP
N_GATES = 2 * N_HEADS
D_FF = 5632
ROPE_BASE = 10000.0
PAST_LEN = 4096
EPS = 1e-6
K_SCALE = HEAD_DIM ** -0.5

LANES = 128
SUBLANES = 8
VMEM_LIMIT = 60 * 1024 * 1024
TAIL_VMEM_LIMIT = 62 * 1024 * 1024

PROMPT_CHUNK = 256
INPROJ_TM = 1024
INPROJ_TN = 2048
SAMPLE_GROUP = 4
OUTPROJ_TM = 512
FFN_TM = 1024
TAIL_ROWS = 256
FFN_TF = 512
FFN_GRANULE = 256

_NT = (((1,), (1,)), ((), ()))
_TN = (((0,), (0,)), ((), ()))


def _dot(a, b):
    return jnp.dot(a, b, preferred_element_type=F32)


def _rms_scale(x):
    return lax.rsqrt(jnp.mean(x * x, axis=-1, keepdims=True) + EPS)


def _params(n_axes, vmem_limit=VMEM_LIMIT):
    return pltpu.CompilerParams(
        dimension_semantics=("arbitrary",) * n_axes, vmem_limit_bytes=vmem_limit)


def _inproj_prologue(x_ref, g1_ref, wgate_ref, bgate_ref, gt_ref, xn_ref):
    x = x_ref[...]
    xn = (x * _rms_scale(x) * g1_ref[...]).astype(BF16)
    xn_ref[...] = xn
    gates = _dot(xn, wgate_ref[...]) + bgate_ref[...]
    gt_ref[...] = gates.T[0:N_GATES, :]


def _inproj_kernel(x_ref, g1_ref, w_ref, wgate_ref, bgate_ref, proj_ref, gt_ref, xn_ref):
    @pl.when(pl.program_id(1) == 0)
    def _():
        _inproj_prologue(x_ref, g1_ref, wgate_ref, bgate_ref, gt_ref, xn_ref)

    proj_ref[...] = _dot(xn_ref[...], w_ref[...]).astype(BF16)


def _inproj(x, g1, w_main, w_gate, b_gate, *, tm, tn):
    n = x.shape[0]
    return pl.pallas_call(
        _inproj_kernel,
        grid=(n // tm, D_PROJ // tn),
        in_specs=[
            pl.BlockSpec((tm, D_MODEL), lambda i, j: (i, 0)),
            pl.BlockSpec((1, D_MODEL), lambda i, j: (0, 0)),
            pl.BlockSpec((D_MODEL, tn), lambda i, j: (0, j)),
            pl.BlockSpec((D_MODEL, LANES), lambda i, j: (0, 0)),
            pl.BlockSpec((1, LANES), lambda i, j: (0, 0)),
        ],
        out_specs=[
            pl.BlockSpec((tm, tn), lambda i, j: (i, j)),
            pl.BlockSpec((N_GATES, tm), lambda i, j: (0, i)),
        ],
        out_shape=[
            jax.ShapeDtypeStruct((n, D_PROJ), BF16),
            jax.ShapeDtypeStruct((N_GATES, n), F32),
        ],
        scratch_shapes=[pltpu.VMEM((tm, D_MODEL), BF16)],
        compiler_params=_params(2),
        name="inproj",
    )(x, g1, w_main, w_gate, b_gate)


def _lane_scan(x, combine, identity):
    width = x.shape[1]
    lane = lax.broadcasted_iota(jnp.int32, x.shape, 1)
    shift = 1
    while shift < width:
        shifted = jnp.where(lane >= shift, pltpu.roll(x, shift, axis=1), identity)
        x = combine(x, shifted)
        shift *= 2
    return x


def _mlstm_phases(q_ref, k_ref, v_ref, o_ref, gates, gain_ref, mix_ref, c_ref, n_ref, m_ref, L):
    b = _lane_scan(jax.nn.log_sigmoid(gates), jnp.add, 0.0)
    b = pltpu.roll(b, N_HEADS, axis=0)
    u = gates - b
    m_prev = m_ref[0][:, 0:1]
    big_m = jnp.maximum(m_prev, _lane_scan(u, jnp.maximum, -jnp.inf))
    a = jnp.exp(m_prev - big_m)
    e = jnp.exp(-(b + big_m))
    m_last = big_m[:, L - 1:L]
    g = jnp.exp(u - m_last)
    dec = a[:, L - 1:L]
    m_new = b[:, L - 1:L] + m_last
    rows = jnp.concatenate(
        [big_m, a, e, g, jnp.zeros((LANES - 4 * SUBLANES, gates.shape[1]), F32)], axis=0)
    cols = rows.T

    t_idx = lax.broadcasted_iota(jnp.int32, (L, L), 0)
    s_idx = lax.broadcasted_iota(jnp.int32, (L, L), 1)
    causal = s_idx <= t_idx

    for h in range(N_HEADS):
        sl = slice(h * HEAD_DIM, (h + 1) * HEAD_DIM)
        qh = q_ref[:, sl]
        kh = k_ref[:, sl] * jnp.asarray(K_SCALE, BF16)
        vh = v_ref[:, sl]
        m_col = cols[:L, h:h + 1]
        a_col = cols[:L, SUBLANES + h:SUBLANES + h + 1]
        e_col = cols[:L, 2 * SUBLANES + h:2 * SUBLANES + h + 1]
        g_col = cols[:L, 3 * SUBLANES + h:3 * SUBLANES + h + 1]
        u_row = u[h:h + 1, :L]

        s = lax.dot_general(qh, kh, _NT, preferred_element_type=F32)
        yield
        sw = s * jnp.exp(jnp.where(causal, u_row - m_col, -jnp.inf))
        c_h = c_ref[0, h]
        n_h = n_ref[0, h:h + 1, :]
        num = a_col * _dot(qh, c_h.astype(BF16)) + _dot(sw.astype(BF16), vh)
        qn = jnp.sum(qh.astype(F32) * n_h, axis=-1, keepdims=True)
        den = a_col * qn + jnp.sum(sw, axis=-1, keepdims=True)
        hh = num * (1.0 / jnp.maximum(jnp.abs(den), e_col))
        y = hh * _rms_scale(hh) * gain_ref[:, sl] * jax.nn.sigmoid(o_ref[:, sl].astype(F32))
        mix_ref[:, sl] = y.astype(BF16)

        kg = kh.astype(F32) * g_col
        dec_h = dec[h:h + 1, :]
        c_ref[0, h] = dec_h * c_h + lax.dot_general(
            kg.astype(BF16), vh, _TN, preferred_element_type=F32)
        n_ref[0, h:h + 1, :] = dec_h * n_h + jnp.sum(kg, axis=0, keepdims=True)
        if h == N_HEADS - 1:
            m_ref[0] = jnp.broadcast_to(m_new, (SUBLANES, LANES))
        yield


def _run(*phase_generators):
    live = list(phase_generators)
    while live:
        live = [g for g in live if next(g, StopIteration) is not StopIteration]


def _seq_views(s, rows, token_refs, state_refs):
    return ([ref.at[pl.ds(s * rows, rows), :] for ref in token_refs],
            [ref.at[pl.ds(s, 1)] for ref in state_refs])


def _mlstm_state_specs(index_map3, index_map4, group=1):
    return [
        pl.BlockSpec((group, N_HEADS, HEAD_DIM, HEAD_DIM), index_map4),
        pl.BlockSpec((group, N_HEADS, HEAD_DIM), index_map3),
        pl.BlockSpec((group, SUBLANES, LANES), index_map3),
    ]


def _mlstm_state_shapes(batch):
    return [
        jax.ShapeDtypeStruct((batch, N_HEADS, HEAD_DIM, HEAD_DIM), F32),
        jax.ShapeDtypeStruct((batch, N_HEADS, HEAD_DIM), F32),
        jax.ShapeDtypeStruct((batch, SUBLANES, LANES), F32),
    ]


def _mlstm_kernel(q_ref, k_ref, v_ref, o_ref, gt_ref, gain_ref, c0_ref, n0_ref, m0_ref,
                  mix_ref, c_ref, n_ref, m_ref, *, chunk, group):
    c_ref[...] = c0_ref[...]
    n_ref[...] = n0_ref[...]
    m_ref[...] = m0_ref[...]
    phases = []
    for s in range(group):
        (q, k, v, o, mix), (c, n, m) = _seq_views(
            s, chunk, (q_ref, k_ref, v_ref, o_ref, mix_ref), (c_ref, n_ref, m_ref))
        phases.append(_mlstm_phases(q, k, v, o, gt_ref[s], gain_ref, mix, c, n, m, chunk))
    _run(*phases)


def _mlstm_single_chunk(proj, gt3, gain, init, *, batch, seq, group):
    tok = lambda col: pl.BlockSpec((group * seq, D_GROUP), lambda b, col=col: (b, col))
    state_specs = _mlstm_state_specs(lambda b: (b, 0, 0), lambda b: (b, 0, 0, 0), group)
    return pl.pallas_call(
        functools.partial(_mlstm_kernel, chunk=seq, group=group),
        grid=(batch // group,),
        in_specs=[tok(0), tok(1), tok(2), tok(3),
                  pl.BlockSpec((group, SUBLANES, gt3.shape[2]), lambda b: (b, 0, 0)),
                  pl.BlockSpec((1, D_GROUP), lambda b: (0, 0))] + state_specs,
        out_specs=[pl.BlockSpec((group * seq, D_GROUP), lambda b: (b, 0))] + state_specs,
        out_shape=[jax.ShapeDtypeStruct((batch * seq, D_GROUP), BF16)] + _mlstm_state_shapes(batch),
        compiler_params=_params(1),
        name="mlstm",
    )(proj, proj, proj, proj, gt3, gain, *init)


def _ret_phases(lg_ref, q_ref, k_ref, v_ref, g_ref, cos_ref, sin_ref, gain_ref, mix_ref, r_ref, L):
    half = HEAD_DIM // 2
    cos = cos_ref[...]
    sin = sin_ref[...]
    cos_k = cos * K_SCALE
    sin_k = sin * K_SCALE
    t_idx = lax.broadcasted_iota(jnp.int32, (L, L), 0)
    s_idx = lax.broadcasted_iota(jnp.int32, (L, L), 1)
    diff = t_idx - s_idx
    diff_pos = jnp.maximum(diff, 0).astype(F32)
    t_col = lax.broadcasted_iota(jnp.int32, (L, 1), 0).astype(F32)

    for h in range(N_HEADS):
        lo = slice(h * HEAD_DIM, h * HEAD_DIM + half)
        hi = slice(h * HEAD_DIM + half, (h + 1) * HEAD_DIM)
        sl = slice(h * HEAD_DIM, (h + 1) * HEAD_DIM)
        lg = lg_ref[h]
        q1 = q_ref[:, lo].astype(F32)
        q2 = q_ref[:, hi].astype(F32)
        k1 = k_ref[:, lo].astype(F32)
        k2 = k_ref[:, hi].astype(F32)
        qr = jnp.concatenate([q1 * cos - q2 * sin, q2 * cos + q1 * sin], axis=-1).astype(BF16)
        kr = jnp.concatenate([k1 * cos_k - k2 * sin_k, k2 * cos_k + k1 * sin_k], axis=-1)
        vh = v_ref[:, sl]

        dmask = jnp.where(diff >= 0, jnp.exp(lg * diff_pos), 0.0)
        inter = jnp.exp(lg * (t_col + 1.0))
        kdec = jnp.exp(lg * (L - 1.0 - t_col))
        cdec = jnp.exp(lg * jnp.full((1, 1), L, F32))

        s = lax.dot_general(qr, kr.astype(BF16), _NT, preferred_element_type=F32)
        yield
        r_h = r_ref[0, h]
        o = _dot((s * dmask).astype(BF16), vh) + inter * _dot(qr, r_h.astype(BF16))
        r_ref[0, h] = cdec * r_h + lax.dot_general(
            (kr * kdec).astype(BF16), vh, _TN, preferred_element_type=F32)

        oc = o - jnp.mean(o, axis=-1, keepdims=True)
        y = oc * _rms_scale(oc) * gain_ref[:, sl] * jax.nn.silu(g_ref[:, sl].astype(F32))
        mix_ref[:, sl] = y.astype(BF16)
        yield


def _ret_kernel(lg_ref, q_ref, k_ref, v_ref, g_ref, cos_ref, sin_ref, gain_ref, r0_ref,
                mix_ref, r_ref, *, chunk, group):
    r_ref[...] = r0_ref[...]
    phases = []
    for s in range(group):
        (q, k, v, g, mix), (r,) = _seq_views(
            s, chunk, (q_ref, k_ref, v_ref, g_ref, mix_ref), (r_ref,))
        phases.append(_ret_phases(lg_ref, q, k, v, g, cos_ref, sin_ref, gain_ref, mix, r, chunk))
    _run(*phases)


def _retention_single_chunk(proj, log_gamma, cos, sin, gain, init, *, batch, seq, group):
    tok = lambda col: pl.BlockSpec((group * seq, D_GROUP), lambda b, col=col: (b, col))
    state_spec = pl.BlockSpec((group, N_HEADS, HEAD_DIM, HEAD_DIM), lambda b: (b, 0, 0, 0))
    rope_spec = pl.BlockSpec((seq, HEAD_DIM // 2), lambda b: (0, 0))
    return pl.pallas_call(
        functools.partial(_ret_kernel, chunk=seq, group=group),
        grid=(batch // group,),
        in_specs=[pl.BlockSpec(memory_space=pltpu.SMEM),
                  tok(4), tok(5), tok(6), tok(7), rope_spec, rope_spec,
                  pl.BlockSpec((1, D_GROUP), lambda b: (0, 0)), state_spec],
        out_specs=[pl.BlockSpec((group * seq, D_GROUP), lambda b: (b, 0)), state_spec],
        out_shape=[
            jax.ShapeDtypeStruct((batch * seq, D_GROUP), BF16),
            jax.ShapeDtypeStruct((batch, N_HEADS, HEAD_DIM, HEAD_DIM), F32),
        ],
        compiler_params=_params(1),
        name="retention",
    )(log_gamma, proj, proj, proj, proj, cos, sin, gain, init)


def _prompt_kernel(lg_ref, x_ref, g1_ref, w_ref, wgate_ref, bgate_ref, gain_m_ref, cos_ref,
                   sin_ref, gain_r_ref, mixm_ref, c_ref, n_ref, m_ref, mixr_ref, r_ref,
                   xn_scr, proj_scr, gt_scr, *, n_chunks, chunks_per_seq, chunk):
    t = pl.program_id(0)
    piece = D_PROJ // (2 * N_HEADS)

    def inproj_pieces(slot):
        x = x_ref[...]
        xn_scr[...] = (x * _rms_scale(x) * g1_ref[...]).astype(BF16)
        gates = _dot(xn_scr[...], wgate_ref[...]) + bgate_ref[...]
        gt_scr[slot] = gates.T[0:N_GATES, :]
        for p in range(D_PROJ // piece):
            cols = pl.ds(p * piece, piece)
            proj_scr[slot, :, cols] = _dot(xn_scr[...], w_ref[:, cols]).astype(BF16)
            yield

    def mixers(slot):
        group = lambda k: proj_scr.at[slot, :, pl.ds(k * D_GROUP, D_GROUP)]
        return [
            _mlstm_phases(group(0), group(1), group(2), group(3), gt_scr[slot], gain_m_ref,
                          mixm_ref, c_ref, n_ref, m_ref, chunk),
            _ret_phases(lg_ref, group(4), group(5), group(6), group(7), cos_ref, sin_ref,
                        gain_r_ref, mixr_ref, r_ref, chunk),
        ]

    @pl.when((t >= 1) & ((t - 1) % chunks_per_seq == 0))
    def _():
        c_ref[...] = jnp.zeros_like(c_ref)
        n_ref[...] = jnp.zeros_like(n_ref)
        m_ref[...] = jnp.zeros_like(m_ref)
        r_ref[...] = jnp.zeros_like(r_ref)

    @pl.when(t == 0)
    def _():
        _run(inproj_pieces(0))

    @pl.when((t >= 1) & (t < n_chunks))
    def _():
        slot = t % 2
        pieces = inproj_pieces(slot)
        heads = mixers(1 - slot)
        for _ in range(N_HEADS):
            for phases in heads:
                next(phases)
                next(pieces)
                next(phases)

    @pl.when(t == n_chunks)
    def _():
        _run(*mixers((n_chunks - 1) % 2))


def _prompt_front(x2d, w, rope, *, batch, seq):
    chunk = PROMPT_CHUNK
    chunks_per_seq = seq // chunk
    n_chunks = batch * chunks_per_seq
    cos, sin = rope
    const = lambda shape, **kw: pl.BlockSpec(shape, lambda t: (0,) * len(shape), **kw)
    mixed = lambda t: jnp.maximum(t - 1, 0)
    rope_spec = pl.BlockSpec((chunk, HEAD_DIM // 2), lambda t: (mixed(t) % chunks_per_seq, 0))
    mix_spec = pl.BlockSpec((chunk, D_GROUP), lambda t: (mixed(t), 0))
    mix_shape = jax.ShapeDtypeStruct((batch * seq, D_GROUP), BF16)
    seq_of = lambda t: mixed(t) // chunks_per_seq
    state4 = pl.BlockSpec((1, N_HEADS, HEAD_DIM, HEAD_DIM), lambda t: (seq_of(t), 0, 0, 0))
    state_specs = _mlstm_state_specs(lambda t: (seq_of(t), 0, 0), lambda t: (seq_of(t), 0, 0, 0))
    return pl.pallas_call(
        functools.partial(_prompt_kernel, n_chunks=n_chunks, chunks_per_seq=chunks_per_seq,
                          chunk=chunk),
        grid=(n_chunks + 1,),
        in_specs=[
            pl.BlockSpec(memory_space=pltpu.SMEM),
            pl.BlockSpec((chunk, D_MODEL), lambda t: (jnp.minimum(t, n_chunks - 1), 0)),
            const((1, D_MODEL)),
            const((D_MODEL, D_PROJ), pipeline_mode=pl.Buffered(1)),
            const((D_MODEL, LANES), pipeline_mode=pl.Buffered(1)),
            const((1, LANES)),
            const((1, D_GROUP)),
            rope_spec, rope_spec,
            const((1, D_GROUP)),
        ],
        out_specs=[mix_spec] + state_specs + [mix_spec, state4],
        out_shape=([mix_shape] + _mlstm_state_shapes(batch)
                   + [mix_shape, jax.ShapeDtypeStruct((batch, N_HEADS, HEAD_DIM, HEAD_DIM), F32)]),
        scratch_shapes=[
            pltpu.VMEM((chunk, D_MODEL), BF16),
            pltpu.VMEM((2, chunk, D_PROJ), BF16),
            pltpu.VMEM((2, N_GATES, chunk), F32),
        ],
        compiler_params=_params(1),
        name="prompt_front",
    )(w["log_gamma"], x2d, w["g1"], w["w_in"], w["w_gatecols"], w["b_gate"], w["gain_m"],
      cos, sin, w["gain_r"])


def _outproj_kernel(x_ref, mm_ref, mr_ref, wm_ref, wr_ref, g2_ref, x1_ref, hn_ref):
    x1 = x_ref[...] + _dot(mm_ref[...], wm_ref[...]) + _dot(mr_ref[...], wr_ref[...])
    x1_ref[...] = x1
    hn_ref[...] = (x1 * _rms_scale(x1) * g2_ref[...]).astype(BF16)


def _outproj(x, mix_m, mix_r, w_out, g2, *, tm):
    n = x.shape[0]
    w_spec = lambda half: pl.BlockSpec((D_GROUP, D_MODEL), lambda i, half=half: (half, 0),
                                       pipeline_mode=pl.Buffered(1))
    return pl.pallas_call(
        _outproj_kernel,
        grid=(n // tm,),
        in_specs=[
            pl.BlockSpec((tm, D_MODEL), lambda i: (i, 0)),
            pl.BlockSpec((tm, D_GROUP), lambda i: (i, 0)),
            pl.BlockSpec((tm, D_GROUP), lambda i: (i, 0)),
            w_spec(0), w_spec(1),
            pl.BlockSpec((1, D_MODEL), lambda i: (0, 0)),
        ],
        out_specs=[
            pl.BlockSpec((tm, D_MODEL), lambda i: (i, 0)),
            pl.BlockSpec((tm, D_MODEL), lambda i: (i, 0)),
        ],
        out_shape=[
            jax.ShapeDtypeStruct((n, D_MODEL), F32),
            jax.ShapeDtypeStruct((n, D_MODEL), BF16),
        ],
        compiler_params=_params(1),
        name="outproj",
    )(x, mix_m, mix_r, w_out, w_out, g2)


def _swiglu_down(hn, w_gu, w_down):
    gu = _dot(hn, w_gu)
    acts = []
    for base in range(0, w_gu.shape[1], 2 * FFN_GRANULE):
        gate = gu[:, base:base + FFN_GRANULE]
        up = gu[:, base + FFN_GRANULE:base + 2 * FFN_GRANULE]
        acts.append((jax.nn.silu(gate) * up).astype(BF16))
    act = acts[0] if len(acts) == 1 else jnp.concatenate(acts, axis=-1)
    return _dot(act, w_down)


def _ffn_kernel(*refs, pack):
    if pack:
        (hn_ref, x1_hbm, wg32_ref, wu32_ref, wd32_ref, gf_ref,
         y_ref, wgu_ref, wd_ref, x1_buf, x1_sem) = refs
    else:
        hn_ref, x1_hbm, wgu_ref, wd_ref, gf_ref, y_ref, x1_buf, x1_sem = refs
    i = pl.program_id(0)
    f = pl.program_id(1)
    last = pl.num_programs(1) - 1
    tm = x1_buf.shape[0]

    def x1_copy(tile):
        return pltpu.make_async_copy(x1_hbm.at[pl.ds(tile * tm, tm), :], x1_buf, x1_sem)

    @pl.when((f == 0) & (i == 0))
    def _():
        x1_copy(0).start()

    @pl.when(f == 0)
    def _():
        x1_copy(i).wait()

    @pl.when((f == 1) & (i + 1 < pl.num_programs(0)))
    def _():
        x1_copy(i + 1).start()

    def step(first, final):
        if pack:
            wgu_ref[:, :FFN_GRANULE] = wg32_ref[...].astype(BF16)
            wgu_ref[:, FFN_GRANULE:] = wu32_ref[...].astype(BF16)
            wd_ref[...] = wd32_ref[...].astype(BF16)
        part = _swiglu_down(hn_ref[...], wgu_ref[...], wd_ref[...])
        if first:
            y_ref[...] = x1_buf[...] + part
        else:
            y_ref[...] += part
        if final:
            x2 = y_ref[...]
            y_ref[...] = x2 * _rms_scale(x2) * gf_ref[...]

    pl.when(f == 0)(functools.partial(step, True, False))
    pl.when((f > 0) & (f < last))(functools.partial(step, False, False))
    pl.when(f == last)(functools.partial(step, False, True))


def _ffn(hn, x1, w, *, tm, tf):
    n = hn.shape[0]
    pack = "w_gu" not in w
    assert not pack or (n == tm and tf == FFN_GRANULE)
    assert D_FF // tf >= 3
    gu_spec = pl.BlockSpec((D_MODEL, 2 * tf), lambda i, f: (0, f))
    down_spec = pl.BlockSpec((tf, D_MODEL), lambda i, f: (f, 0))
    if pack:
        col_spec = pl.BlockSpec((D_MODEL, tf), lambda i, f: (0, f))
        w_specs, w_args = [col_spec, col_spec, down_spec], [w["w_gate"], w["w_up"], w["w_down"]]
    else:
        w_specs, w_args = [gu_spec, down_spec], [w["w_gu"], w["w_down"]]
    return pl.pallas_call(
        functools.partial(_ffn_kernel, pack=pack),
        grid=(n // tm, D_FF // tf),
        in_specs=[
            pl.BlockSpec((tm, D_MODEL), lambda i, f: (i, 0)),
            pl.BlockSpec(memory_space=pl.ANY),
            *w_specs,
            pl.BlockSpec((1, D_MODEL), lambda i, f: (0, 0)),
        ],
        out_specs=([pl.BlockSpec((tm, D_MODEL), lambda i, f: (i, 0))]
                   + [gu_spec, down_spec] * pack),
        out_shape=([jax.ShapeDtypeStruct((n, D_MODEL), F32)]
                   + [jax.ShapeDtypeStruct((D_MODEL, 2 * D_FF), BF16),
                      jax.ShapeDtypeStruct((D_FF, D_MODEL), BF16)] * pack),
        scratch_shapes=[pltpu.VMEM((tm, D_MODEL), F32), pltpu.SemaphoreType.DMA(())],
        compiler_params=_params(2),
        name="ffn",
    )(hn, x1, *w_args, w["g_final"])


def _tail_stream_kernel(x_hbm, mm_hbm, mr_hbm, wom_ref, wor_ref, g2_ref, wgu_hbm, wd_hbm, gf_ref,
                        y_ref, x1_buf, mm_buf, mr_buf, hn_buf, wgu_buf, wd_buf, row_sems, w_sems,
                        *, n_f, tf):
    i = pl.program_id(0)
    n_i = pl.num_programs(0)
    tm = x1_buf.shape[0]

    def row_copies(tile):
        rows = pl.ds(tile * tm, tm)
        return (
            pltpu.make_async_copy(x_hbm.at[rows, :], x1_buf, row_sems.at[0]),
            pltpu.make_async_copy(mm_hbm.at[rows, :], mm_buf, row_sems.at[1]),
            pltpu.make_async_copy(mr_hbm.at[rows, :], mr_buf, row_sems.at[2]),
        )

    def w_copies(f, slot):
        col, row = f * (2 * tf), f * tf
        if not isinstance(f, int):
            col, row = pl.multiple_of(col, 2 * tf), pl.multiple_of(row, tf)
        return (
            pltpu.make_async_copy(wgu_hbm.at[:, pl.ds(col, 2 * tf)], wgu_buf.at[slot],
                                  w_sems.at[0, slot]),
            pltpu.make_async_copy(wd_hbm.at[pl.ds(row, tf), :], wd_buf.at[slot],
                                  w_sems.at[1, slot]),
        )

    @pl.when(i == 0)
    def _():
        for copy in row_copies(0) + w_copies(0, 0):
            copy.start()

    def iteration(f, first, final):
        slot = (i * n_f + f) % 2
        for copy in w_copies(f, slot):
            copy.wait()
        if final:
            @pl.when(i + 1 < n_i)
            def _():
                for copy in w_copies(0, 1 - slot):
                    copy.start()
        else:
            for copy in w_copies(f + 1, 1 - slot):
                copy.start()
        if first:
            for copy in row_copies(i):
                copy.wait()
            for r in range(0, tm, TAIL_ROWS):
                rows = pl.ds(r, TAIL_ROWS)
                x1 = (x1_buf[rows, :] + _dot(mm_buf[rows, :], wom_ref[...])
                      + _dot(mr_buf[rows, :], wor_ref[...]))
                x1_buf[rows, :] = x1
                hn_buf[rows, :] = (x1 * _rms_scale(x1) * g2_ref[...]).astype(BF16)
        part = _swiglu_down(hn_buf[...], wgu_buf[slot], wd_buf[slot])
        if first:
            y_ref[...] = x1_buf[...] + part
        else:
            y_ref[...] += part
        if final:
            x2 = y_ref[...]
            y_ref[...] = x2 * _rms_scale(x2) * gf_ref[...]

    def middle(f, carry):
        @pl.when((f == 1) & (i + 1 < n_i))
        def _():
            for copy in row_copies(i + 1):
                copy.start()

        iteration(f, False, False)
        return carry

    iteration(0, True, False)
    lax.fori_loop(1, n_f - 1, middle, 0)
    iteration(n_f - 1, False, True)


def _tail_stream(x, mix_m, mix_r, w, *, tm, tf):
    n = x.shape[0]
    n_f = D_FF // tf
    assert n_f >= 3
    w_out_spec = lambda half: pl.BlockSpec((D_GROUP, D_MODEL), lambda i, half=half: (half, 0),
                                           pipeline_mode=pl.Buffered(1))
    return pl.pallas_call(
        functools.partial(_tail_stream_kernel, n_f=n_f, tf=tf),
        grid=(n // tm,),
        in_specs=[
            pl.BlockSpec(memory_space=pl.ANY),
            pl.BlockSpec(memory_space=pl.ANY),
            pl.BlockSpec(memory_space=pl.ANY),
            w_out_spec(0), w_out_spec(1),
            pl.BlockSpec((1, D_MODEL), lambda i: (0, 0)),
            pl.BlockSpec(memory_space=pl.ANY),
            pl.BlockSpec(memory_space=pl.ANY),
            pl.BlockSpec((1, D_MODEL), lambda i: (0, 0)),
        ],
        out_specs=pl.BlockSpec((tm, D_MODEL), lambda i: (i, 0)),
        out_shape=jax.ShapeDtypeStruct((n, D_MODEL), F32),
        scratch_shapes=[
            pltpu.VMEM((tm, D_MODEL), F32),
            pltpu.VMEM((tm, D_GROUP), BF16),
            pltpu.VMEM((tm, D_GROUP), BF16),
            pltpu.VMEM((tm, D_MODEL), BF16),
            pltpu.VMEM((2, D_MODEL, 2 * tf), BF16),
            pltpu.VMEM((2, tf, D_MODEL), BF16),
            pltpu.SemaphoreType.DMA((3,)),
            pltpu.SemaphoreType.DMA((2, 2)),
        ],
        compiler_params=_params(1, TAIL_VMEM_LIMIT),
        name="tail_stream",
    )(x, mix_m, mix_r, w["w_out"], w["w_out"], w["g2"], w["w_gu"], w["w_down"], w["g_final"])


def _rope_tables(pos):
    half = HEAD_DIM // 2
    freqs = ROPE_BASE ** (-jnp.arange(half, dtype=F32) / half)
    ang = pos[:, None] * freqs[None, :]
    return jnp.cos(ang), jnp.sin(ang)


def _tail(x2d, mix_m, mix_r, w, *, tf):
    if "w_gu" in w:
        return [_tail_stream(x2d, mix_m, mix_r, w, tm=FFN_TM, tf=tf)]
    x1, hn = _outproj(x2d, mix_m, mix_r, w["w_out"], w["g2"], tm=OUTPROJ_TM)
    return _ffn(hn, x1, w, tm=FFN_TM, tf=tf)


def _prompt_trunk(x, w):
    batch, seq, _ = x.shape
    x2d = x.reshape(batch * seq, D_MODEL)
    rope = _rope_tables(jnp.arange(seq, dtype=F32))
    mix_m, c_new, n_new, m_new, mix_r, r_new = _prompt_front(x2d, w, rope, batch=batch, seq=seq)
    (y,) = _tail(x2d, mix_m, mix_r, w, tf=FFN_TF)
    return (y.reshape(batch, seq, D_MODEL), c_new[None], n_new[None],
            m_new[:, :N_HEADS, 0][None], r_new[None])


def _sample_trunk(x, init, w):
    batch, seq, _ = x.shape
    x2d = x.reshape(batch * seq, D_MODEL)
    proj, gt = _inproj(x2d, w["g1"], w["w_in"], w["w_gatecols"], w["b_gate"],
                       tm=INPROJ_TM, tn=INPROJ_TN)
    gt3 = gt.reshape(N_GATES, batch, seq).transpose(1, 0, 2)
    gt3 = jnp.pad(gt3, ((0, 0), (0, 0), (0, max(0, LANES - seq))))
    c0, n0, m0, r0 = init
    m0 = jnp.pad(m0, ((0, 0), (0, SUBLANES - N_HEADS)))
    m0 = jnp.broadcast_to(m0[:, :, None], (batch, SUBLANES, LANES))
    mix_m, c_new, n_new, m_new = _mlstm_single_chunk(proj, gt3, w["gain_m"], (c0, n0, m0),
                                                     batch=batch, seq=seq, group=SAMPLE_GROUP)
    cos, sin = _rope_tables(PAST_LEN + jnp.arange(seq, dtype=F32))
    mix_r, r_new = _retention_single_chunk(proj, w["log_gamma"], cos, sin, w["gain_r"], r0,
                                           batch=batch, seq=seq, group=SAMPLE_GROUP)
    y, w_gu16, w_down16 = _tail(x2d, mix_m, mix_r, w, tf=FFN_GRANULE)
    w16 = dict(w_gu=w_gu16, w_down=w_down16)
    return (y.reshape(batch, seq, D_MODEL), c_new[None], n_new[None],
            m_new[:, :N_HEADS, 0][None], r_new[None]), w16


def kernel(x_prompt, x_sample, state_mlstm_C, state_mlstm_n, state_mlstm_m, state_ret,
           g_norm1, w_in, b_gates, g_mlstm_norm, g_ret_norm, w_out, g_norm2,
           w_gate, w_up, w_down, g_final):
    w_in0 = w_in[0].astype(BF16)
    w = dict(
        g1=g_norm1[0].reshape(1, D_MODEL),
        w_in=w_in0,
        w_gatecols=jnp.pad(w_in0[:, D_PROJ:], ((0, 0), (0, LANES - N_GATES))),
        b_gate=jnp.pad(b_gates[0], (0, LANES - N_GATES)).reshape(1, LANES),
        gain_m=g_mlstm_norm[0].reshape(1, D_GROUP),
        gain_r=g_ret_norm[0].reshape(1, D_GROUP),
        w_out=w_out[0].astype(BF16),
        g2=g_norm2[0].reshape(1, D_MODEL),
        w_gate=w_gate[0],
        w_up=w_up[0],
        w_down=w_down[0],
        g_final=g_final.reshape(1, D_MODEL),
        log_gamma=jnp.log(1.0 - jnp.exp2(-5.0 - jnp.arange(N_HEADS, dtype=F32))),
    )
    init_s = (state_mlstm_C[0], state_mlstm_n[0], state_mlstm_m[0], state_ret[0])
    (y_s, c_s, n_s, m_s, r_s), w16 = _sample_trunk(x_sample, init_s, w)
    y_p, c_p, n_p, m_p, r_p = _prompt_trunk(x_prompt, {**w, **w16})
    return (y_p, y_s, c_p, n_p, m_p, r_p, c_s, n_s, m_s, r_s)
```

```python
import functools

import jax
import jax.numpy as jnp
from jax import lax
from jax.experimental import pallas as pl
from jax.experimental.pallas import tpu as pltpu

F32 = jnp.float32
BF16 = jnp.bfloat16

D_MODEL = 2048
N_HEADS = 4
HEAD_DIM = 256
D_GROUP = N_HEADS * HEAD_DIM
D_PROJ = 8 * D_GROUP
N_GATES = 2 * N_HEADS
D_FF = 5632
ROPE_BASE = 10000.0
PAST_LEN = 4096
EPS = 1e-6
K_SCALE = HEAD_DIM ** -0.5

LANES = 128
SUBLANES = 8
VMEM_LIMIT = 60 * 1024 * 1024
TAIL_VMEM_LIMIT = 62 * 1024 * 1024

PROMPT_CHUNK = 256
INPROJ_TM = 1024
INPROJ_TN = 2048
SAMPLE_GROUP = 4
OUTPROJ_TM = 512
FFN_TM = 1024
TAIL_ROWS = 256
FFN_TF = 512
FFN_GRANULE = 256

_NT = (((1,), (1,)), ((), ()))
_TN = (((0,), (0,)), ((), ()))


def _dot(a, b):
    return jnp.dot(a, b, preferred_element_type=F32)


def _rms_scale(x):
    return lax.rsqrt(jnp.mean(x * x, axis=-1, keepdims=True) + EPS)


def _params(n_axes, vmem_limit=VMEM_LIMIT):
    return pltpu.CompilerParams(
        dimension_semantics=("arbitrary",) * n_axes, vmem_limit_bytes=vmem_limit)


def _inproj_prologue(x_ref, g1_ref, wgate_ref, bgate_ref, gt_ref, xn_ref):
    x = x_ref[...]
    xn = (x * _rms_scale(x) * g1_ref[...]).astype(BF16)
    xn_ref[...] = xn
    gates = _dot(xn, wgate_ref[...]) + bgate_ref[...]
    gt_ref[...] = gates.T[0:N_GATES, :]


def _inproj_kernel(x_ref, g1_ref, w_ref, wgate_ref, bgate_ref, proj_ref, gt_ref, xn_ref):
    @pl.when(pl.program_id(1) == 0)
    def _():
        _inproj_prologue(x_ref, g1_ref, wgate_ref, bgate_ref, gt_ref, xn_ref)

    proj_ref[...] = _dot(xn_ref[...], w_ref[...]).astype(BF16)


def _inproj(x, g1, w_main, w_gate, b_gate, *, tm, tn):
    n = x.shape[0]
    return pl.pallas_call(
        _inproj_kernel,
        grid=(n // tm, D_PROJ // tn),
        in_specs=[
            pl.BlockSpec((tm, D_MODEL), lambda i, j: (i, 0)),
            pl.BlockSpec((1, D_MODEL), lambda i, j: (0, 0)),
            pl.BlockSpec((D_MODEL, tn), lambda i, j: (0, j)),
            pl.BlockSpec((D_MODEL, LANES), lambda i, j: (0, 0)),
            pl.BlockSpec((1, LANES), lambda i, j: (0, 0)),
        ],
        out_specs=[
            pl.BlockSpec((tm, tn), lambda i, j: (i, j)),
            pl.BlockSpec((N_GATES, tm), lambda i, j: (0, i)),
        ],
        out_shape=[
            jax.ShapeDtypeStruct((n, D_PROJ), BF16),
            jax.ShapeDtypeStruct((N_GATES, n), F32),
        ],
        scratch_shapes=[pltpu.VMEM((tm, D_MODEL), BF16)],
        compiler_params=_params(2),
        name="inproj",
    )(x, g1, w_main, w_gate, b_gate)


def _lane_scan(x, combine, identity):
    width = x.shape[1]
    lane = lax.broadcasted_iota(jnp.int32, x.shape, 1)
    shift = 1
    while shift < width:
        shifted = jnp.where(lane >= shift, pltpu.roll(x, shift, axis=1), identity)
        x = combine(x, shifted)
        shift *= 2
    return x


def _mlstm_phases(q_ref, k_ref, v_ref, o_ref, gates, gain_ref, mix_ref, c_ref, n_ref, m_ref, L):
    b = _lane_scan(jax.nn.log_sigmoid(gates), jnp.add, 0.0)
    b = pltpu.roll(b, N_HEADS, axis=0)
    u = gates - b
    m_prev = m_ref[0][:, 0:1]
    big_m = jnp.maximum(m_prev, _lane_scan(u, jnp.maximum, -jnp.inf))
    a = jnp.exp(m_prev - big_m)
    e = jnp.exp(-(b + big_m))
    m_last = big_m[:, L - 1:L]
    g = jnp.exp(u - m_last)
    dec = a[:, L - 1:L]
    m_new = b[:, L - 1:L] + m_last
    rows = jnp.concatenate(
        [big_m, a, e, g, jnp.zeros((LANES - 4 * SUBLANES, gates.shape[1]), F32)], axis=0)
    cols = rows.T

    t_idx = lax.broadcasted_iota(jnp.int32, (L, L), 0)
    s_idx = lax.broadcasted_iota(jnp.int32, (L, L), 1)
    causal = s_idx <= t_idx

    for h in range(N_HEADS):
        sl = slice(h * HEAD_DIM, (h + 1) * HEAD_DIM)
        qh = q_ref[:, sl]
        kh = k_ref[:, sl] * jnp.asarray(K_SCALE, BF16)
        vh = v_ref[:, sl]
        m_col = cols[:L, h:h + 1]
        a_col = cols[:L, SUBLANES + h:SUBLANES + h + 1]
        e_col = cols[:L, 2 * SUBLANES + h:2 * SUBLANES + h + 1]
        g_col = cols[:L, 3 * SUBLANES + h:3 * SUBLANES + h + 1]
        u_row = u[h:h + 1, :L]

        s = lax.dot_general(qh, kh, _NT, preferred_element_type=F32)
        yield
        sw = s * jnp.exp(jnp.where(causal, u_row - m_col, -jnp.inf))
        c_h = c_ref[0, h]
        n_h = n_ref[0, h:h + 1, :]
        num = a_col * _dot(qh, c_h.astype(BF16)) + _dot(sw.astype(BF16), vh)
        qn = jnp.sum(qh.astype(F32) * n_h, axis=-1, keepdims=True)
        den = a_col * qn + jnp.sum(sw, axis=-1, keepdims=True)
        hh = num * (1.0 / jnp.maximum(jnp.abs(den), e_col))
        y = hh * _rms_scale(hh) * gain_ref[:, sl] * jax.nn.sigmoid(o_ref[:, sl].astype(F32))
        mix_ref[:, sl] = y.astype(BF16)

        kg = kh.astype(F32) * g_col
        dec_h = dec[h:h + 1, :]
        c_ref[0, h] = dec_h * c_h + lax.dot_general(
            kg.astype(BF16), vh, _TN, preferred_element_type=F32)
        n_ref[0, h:h + 1, :] = dec_h * n_h + jnp.sum(kg, axis=0, keepdims=True)
        if h == N_HEADS - 1:
            m_ref[0] = jnp.broadcast_to(m_new, (SUBLANES, LANES))
        yield


def _run(*phase_generators):
    live = list(phase_generators)
    while live:
        live = [g for g in live if next(g, StopIteration) is not StopIteration]


def _seq_views(s, rows, token_refs, state_refs):
    return ([ref.at[pl.ds(s * rows, rows), :] for ref in token_refs],
            [ref.at[pl.ds(s, 1)] for ref in state_refs])


def _mlstm_state_specs(index_map3, index_map4, group=1):
    return [
        pl.BlockSpec((group, N_HEADS, HEAD_DIM, HEAD_DIM), index_map4),
        pl.BlockSpec((group, N_HEADS, HEAD_DIM), index_map3),
        pl.BlockSpec((group, SUBLANES, LANES), index_map3),
    ]


def _mlstm_state_shapes(batch):
    return [
        jax.ShapeDtypeStruct((batch, N_HEADS, HEAD_DIM, HEAD_DIM), F32),
        jax.ShapeDtypeStruct((batch, N_HEADS, HEAD_DIM), F32),
        jax.ShapeDtypeStruct((batch, SUBLANES, LANES), F32),
    ]


def _mlstm_kernel(q_ref, k_ref, v_ref, o_ref, gt_ref, gain_ref, c0_ref, n0_ref, m0_ref,
                  mix_ref, c_ref, n_ref, m_ref, *, chunk, group):
    c_ref[...] = c0_ref[...]
    n_ref[...] = n0_ref[...]
    m_ref[...] = m0_ref[...]
    phases = []
    for s in range(group):
        (q, k, v, o, mix), (c, n, m) = _seq_views(
            s, chunk, (q_ref, k_ref, v_ref, o_ref, mix_ref), (c_ref, n_ref, m_ref))
        phases.append(_mlstm_phases(q, k, v, o, gt_ref[s], gain_ref, mix, c, n, m, chunk))
    _run(*phases)


def _mlstm_single_chunk(proj, gt3, gain, init, *, batch, seq, group):
    tok = lambda col: pl.BlockSpec((group * seq, D_GROUP), lambda b, col=col: (b, col))
    state_specs = _mlstm_state_specs(lambda b: (b, 0, 0), lambda b: (b, 0, 0, 0), group)
    return pl.pallas_call(
        functools.partial(_mlstm_kernel, chunk=seq, group=group),
        grid=(batch // group,),
        in_specs=[tok(0), tok(1), tok(2), tok(3),
                  pl.BlockSpec((group, SUBLANES, gt3.shape[2]), lambda b: (b, 0, 0)),
                  pl.BlockSpec((1, D_GROUP), lambda b: (0, 0))] + state_specs,
        out_specs=[pl.BlockSpec((group * seq, D_GROUP), lambda b: (b, 0))] + state_specs,
        out_shape=[jax.ShapeDtypeStruct((batch * seq, D_GROUP), BF16)] + _mlstm_state_shapes(batch),
        compiler_params=_params(1),
        name="mlstm",
    )(proj, proj, proj, proj, gt3, gain, *init)


def _ret_phases(lg_ref, q_ref, k_ref, v_ref, g_ref, cos_ref, sin_ref, gain_ref, mix_ref, r_ref, L):
    half = HEAD_DIM // 2
    cos = cos_ref[...]
    sin = sin_ref[...]
    cos_k = cos * K_SCALE
    sin_k = sin * K_SCALE
    t_idx = lax.broadcasted_iota(jnp.int32, (L, L), 0)
    s_idx = lax.broadcasted_iota(jnp.int32, (L, L), 1)
    diff = t_idx - s_idx
    diff_pos = jnp.maximum(diff, 0).astype(F32)
    t_col = lax.broadcasted_iota(jnp.int32, (L, 1), 0).astype(F32)

    for h in range(N_HEADS):
        lo = slice(h * HEAD_DIM, h * HEAD_DIM + half)
        hi = slice(h * HEAD_DIM + half, (h + 1) * HEAD_DIM)
        sl = slice(h * HEAD_DIM, (h + 1) * HEAD_DIM)
        lg = lg_ref[h]
        q1 = q_ref[:, lo].astype(F32)
        q2 = q_ref[:, hi].astype(F32)
        k1 = k_ref[:, lo].astype(F32)
        k2 = k_ref[:, hi].astype(F32)
        qr = jnp.concatenate([q1 * cos - q2 * sin, q2 * cos + q1 * sin], axis=-1).astype(BF16)
        kr = jnp.concatenate([k1 * cos_k - k2 * sin_k, k2 * cos_k + k1 * sin_k], axis=-1)
        vh = v_ref[:, sl]

        dmask = jnp.where(diff >= 0, jnp.exp(lg * diff_pos), 0.0)
        inter = jnp.exp(lg * (t_col + 1.0))
        kdec = jnp.exp(lg * (L - 1.0 - t_col))
        cdec = jnp.exp(lg * jnp.full((1, 1), L, F32))

        s = lax.dot_general(qr, kr.astype(BF16), _NT, preferred_element_type=F32)
        yield
        r_h = r_ref[0, h]
        o = _dot((s * dmask).astype(BF16), vh) + inter * _dot(qr, r_h.astype(BF16))
        r_ref[0, h] = cdec * r_h + lax.dot_general(
            (kr * kdec).astype(BF16), vh, _TN, preferred_element_type=F32)

        oc = o - jnp.mean(o, axis=-1, keepdims=True)
        y = oc * _rms_scale(oc) * gain_ref[:, sl] * jax.nn.silu(g_ref[:, sl].astype(F32))
        mix_ref[:, sl] = y.astype(BF16)
        yield


def _ret_kernel(lg_ref, q_ref, k_ref, v_ref, g_ref, cos_ref, sin_ref, gain_ref, r0_ref,
                mix_ref, r_ref, *, chunk, group):
    r_ref[...] = r0_ref[...]
    phases = []
    for s in range(group):
        (q, k, v, g, mix), (r,) = _seq_views(
            s, chunk, (q_ref, k_ref, v_ref, g_ref, mix_ref), (r_ref,))
        phases.append(_ret_phases(lg_ref, q, k, v, g, cos_ref, sin_ref, gain_ref, mix, r, chunk))
    _run(*phases)


def _retention_single_chunk(proj, log_gamma, cos, sin, gain, init, *, batch, seq, group):
    tok = lambda col: pl.BlockSpec((group * seq, D_GROUP), lambda b, col=col: (b, col))
    state_spec = pl.BlockSpec((group, N_HEADS, HEAD_DIM, HEAD_DIM), lambda b: (b, 0, 0, 0))
    rope_spec = pl.BlockSpec((seq, HEAD_DIM // 2), lambda b: (0, 0))
    return pl.pallas_call(
        functools.partial(_ret_kernel, chunk=seq, group=group),
        grid=(batch // group,),
        in_specs=[pl.BlockSpec(memory_space=pltpu.SMEM),
                  tok(4), tok(5), tok(6), tok(7), rope_spec, rope_spec,
                  pl.BlockSpec((1, D_GROUP), lambda b: (0, 0)), state_spec],
        out_specs=[pl.BlockSpec((group * seq, D_GROUP), lambda b: (b, 0)), state_spec],
        out_shape=[
            jax.ShapeDtypeStruct((batch * seq, D_GROUP), BF16),
            jax.ShapeDtypeStruct((batch, N_HEADS, HEAD_DIM, HEAD_DIM), F32),
        ],
        compiler_params=_params(1),
        name="retention",
    )(log_gamma, proj, proj, proj, proj, cos, sin, gain, init)


def _prompt_kernel(lg_ref, x_ref, g1_ref, w_ref, wgate_ref, bgate_ref, gain_m_ref, cos_ref,
                   sin_ref, gain_r_ref, mixm_ref, c_ref, n_ref, m_ref, mixr_ref, r_ref,
                   xn_scr, proj_scr, gt_scr, *, n_chunks, chunks_per_seq, chunk):
    t = pl.program_id(0)
    piece = D_PROJ // (2 * N_HEADS)

    def inproj_pieces(slot):
        x = x_ref[...]
        xn_scr[...] = (x * _rms_scale(x) * g1_ref[...]).astype(BF16)
        gates = _dot(xn_scr[...], wgate_ref[...]) + bgate_ref[...]
        gt_scr[slot] = gates.T[0:N_GATES, :]
        for p in range(D_PROJ // piece):
            cols = pl.ds(p * piece, piece)
            proj_scr[slot, :, cols] = _dot(xn_scr[...], w_ref[:, cols]).astype(BF16)
            yield

    def mixers(slot):
        group = lambda k: proj_scr.at[slot, :, pl.ds(k * D_GROUP, D_GROUP)]
        return [
            _mlstm_phases(group(0), group(1), group(2), group(3), gt_scr[slot], gain_m_ref,
                          mixm_ref, c_ref, n_ref, m_ref, chunk),
            _ret_phases(lg_ref, group(4), group(5), group(6), group(7), cos_ref, sin_ref,
                        gain_r_ref, mixr_ref, r_ref, chunk),
        ]

    @pl.when((t >= 1) & ((t - 1) % chunks_per_seq == 0))
    def _():
        c_ref[...] = jnp.zeros_like(c_ref)
        n_ref[...] = jnp.zeros_like(n_ref)
        m_ref[...] = jnp.zeros_like(m_ref)
        r_ref[...] = jnp.zeros_like(r_ref)

    @pl.when(t == 0)
    def _():
        _run(inproj_pieces(0))

    @pl.when((t >= 1) & (t < n_chunks))
    def _():
        slot = t % 2
        pieces = inproj_pieces(slot)
        heads = mixers(1 - slot)
        for _ in range(N_HEADS):
            for phases in heads:
                next(phases)
                next(pieces)
                next(phases)

    @pl.when(t == n_chunks)
    def _():
        _run(*mixers((n_chunks - 1) % 2))


def _prompt_front(x2d, w, rope, *, batch, seq):
    chunk = PROMPT_CHUNK
    chunks_per_seq = seq // chunk
    n_chunks = batch * chunks_per_seq
    cos, sin = rope
    const = lambda shape, **kw: pl.BlockSpec(shape, lambda t: (0,) * len(shape), **kw)
    mixed = lambda t: jnp.maximum(t - 1, 0)
    rope_spec = pl.BlockSpec((chunk, HEAD_DIM // 2), lambda t: (mixed(t) % chunks_per_seq, 0))
    mix_spec = pl.BlockSpec((chunk, D_GROUP), lambda t: (mixed(t), 0))
    mix_shape = jax.ShapeDtypeStruct((batch * seq, D_GROUP), BF16)
    seq_of = lambda t: mixed(t) // chunks_per_seq
    state4 = pl.BlockSpec((1, N_HEADS, HEAD_DIM, HEAD_DIM), lambda t: (seq_of(t), 0, 0, 0))
    state_specs = _mlstm_state_specs(lambda t: (seq_of(t), 0, 0), lambda t: (seq_of(t), 0, 0, 0))
    return pl.pallas_call(
        functools.partial(_prompt_kernel, n_chunks=n_chunks, chunks_per_seq=chunks_per_seq,
                          chunk=chunk),
        grid=(n_chunks + 1,),
        in_specs=[
            pl.BlockSpec(memory_space=pltpu.SMEM),
            pl.BlockSpec((chunk, D_MODEL), lambda t: (jnp.minimum(t, n_chunks - 1), 0)),
            const((1, D_MODEL)),
            const((D_MODEL, D_PROJ), pipeline_mode=pl.Buffered(1)),
            const((D_MODEL, LANES), pipeline_mode=pl.Buffered(1)),
            const((1, LANES)),
            const((1, D_GROUP)),
            rope_spec, rope_spec,
            const((1, D_GROUP)),
        ],
        out_specs=[mix_spec] + state_specs + [mix_spec, state4],
        out_shape=([mix_shape] + _mlstm_state_shapes(batch)
                   + [mix_shape, jax.ShapeDtypeStruct((batch, N_HEADS, HEAD_DIM, HEAD_DIM), F32)]),
        scratch_shapes=[
            pltpu.VMEM((chunk, D_MODEL), BF16),
            pltpu.VMEM((2, chunk, D_PROJ), BF16),
            pltpu.VMEM((2, N_GATES, chunk), F32),
        ],
        compiler_params=_params(1),
        name="prompt_front",
    )(w["log_gamma"], x2d, w["g1"], w["w_in"], w["w_gatecols"], w["b_gate"], w["gain_m"],
      cos, sin, w["gain_r"])


def _outproj_kernel(x_ref, mm_ref, mr_ref, wm_ref, wr_ref, g2_ref, x1_ref, hn_ref):
    x1 = x_ref[...] + _dot(mm_ref[...], wm_ref[...]) + _dot(mr_ref[...], wr_ref[...])
    x1_ref[...] = x1
    hn_ref[...] = (x1 * _rms_scale(x1) * g2_ref[...]).astype(BF16)


def _outproj(x, mix_m, mix_r, w_out, g2, *, tm):
    n = x.shape[0]
    w_spec = lambda half: pl.BlockSpec((D_GROUP, D_MODEL), lambda i, half=half: (half, 0),
                                       pipeline_mode=pl.Buffered(1))
    return pl.pallas_call(
        _outproj_kernel,
        grid=(n // tm,),
        in_specs=[
            pl.BlockSpec((tm, D_MODEL), lambda i: (i, 0)),
            pl.BlockSpec((tm, D_GROUP), lambda i: (i, 0)),
            pl.BlockSpec((tm, D_GROUP), lambda i: (i, 0)),
            w_spec(0), w_spec(1),
            pl.BlockSpec((1, D_MODEL), lambda i: (0, 0)),
        ],
        out_specs=[
            pl.BlockSpec((tm, D_MODEL), lambda i: (i, 0)),
            pl.BlockSpec((tm, D_MODEL), lambda i: (i, 0)),
        ],
        out_shape=[
            jax.ShapeDtypeStruct((n, D_MODEL), F32),
            jax.ShapeDtypeStruct((n, D_MODEL), BF16),
        ],
        compiler_params=_params(1),
        name="outproj",
    )(x, mix_m, mix_r, w_out, w_out, g2)


def _swiglu_down(hn, w_gu, w_down):
    gu = _dot(hn, w_gu)
    acts = []
    for base in range(0, w_gu.shape[1], 2 * FFN_GRANULE):
        gate = gu[:, base:base + FFN_GRANULE]
        up = gu[:, base + FFN_GRANULE:base + 2 * FFN_GRANULE]
        acts.append((jax.nn.silu(gate) * up).astype(BF16))
    act = acts[0] if len(acts) == 1 else jnp.concatenate(acts, axis=-1)
    return _dot(act, w_down)


def _ffn_pack_kernel(hn_ref, x1_ref, wg32_ref, wu32_ref, wd32_ref, gf_ref, y_ref, wgu_ref, wd_ref):
    f = pl.program_id(0)
    last = pl.num_programs(0) - 1

    def step(first, final):
        wgu_ref[:, :FFN_GRANULE] = wg32_ref[...].astype(BF16)
        wgu_ref[:, FFN_GRANULE:] = wu32_ref[...].astype(BF16)
        wd_ref[...] = wd32_ref[...].astype(BF16)
        part = _swiglu_down(hn_ref[...], wgu_ref[...], wd_ref[...])
        if first:
            y_ref[...] = x1_ref[...] + part
        else:
            y_ref[...] += part
        if final:
            x2 = y_ref[...]
            y_ref[...] = x2 * _rms_scale(x2) * gf_ref[...]

    pl.when(f == 0)(functools.partial(step, True, False))
    pl.when((f > 0) & (f < last))(functools.partial(step, False, False))
    pl.when(f == last)(functools.partial(step, False, True))


def _ffn_pack(hn, x1, w):
    n = hn.shape[0]
    tf = FFN_GRANULE
    assert D_FF // tf >= 3
    rows_spec = lambda: pl.BlockSpec((n, D_MODEL), lambda f: (0, 0), pipeline_mode=pl.Buffered(1))
    col_spec = pl.BlockSpec((D_MODEL, tf), lambda f: (0, f))
    gu_spec = pl.BlockSpec((D_MODEL, 2 * tf), lambda f: (0, f))
    down_spec = pl.BlockSpec((tf, D_MODEL), lambda f: (f, 0))
    return pl.pallas_call(
        _ffn_pack_kernel,
        grid=(D_FF // tf,),
        in_specs=[rows_spec(), rows_spec(), col_spec, col_spec, down_spec,
                  pl.BlockSpec((1, D_MODEL), lambda f: (0, 0))],
        out_specs=[pl.BlockSpec((n, D_MODEL), lambda f: (0, 0)), gu_spec, down_spec],
        out_shape=[jax.ShapeDtypeStruct((n, D_MODEL), F32),
                   jax.ShapeDtypeStruct((D_MODEL, 2 * D_FF), BF16),
                   jax.ShapeDtypeStruct((D_FF, D_MODEL), BF16)],
        compiler_params=_params(1),
        name="ffn_pack",
    )(hn, x1, w["w_gate"], w["w_up"], w["w_down"], w["g_final"])


def _tail_stream_kernel(x_hbm, mm_hbm, mr_hbm, wom_ref, wor_ref, g2_ref, wgu_hbm, wd_hbm, gf_ref,
                        y_ref, x1_buf, mm_buf, mr_buf, hn_buf, wgu_buf, wd_buf, row_sems, w_sems,
                        *, n_f, tf):
    i = pl.program_id(0)
    n_i = pl.num_programs(0)
    tm = x1_buf.shape[0]

    def row_copies(tile):
        rows = pl.ds(tile * tm, tm)
        return (
            pltpu.make_async_copy(x_hbm.at[rows, :], x1_buf, row_sems.at[0]),
            pltpu.make_async_copy(mm_hbm.at[rows, :], mm_buf, row_sems.at[1]),
            pltpu.make_async_copy(mr_hbm.at[rows, :], mr_buf, row_sems.at[2]),
        )

    def w_copies(f, slot):
        col, row = f * (2 * tf), f * tf
        if not isinstance(f, int):
            col, row = pl.multiple_of(col, 2 * tf), pl.multiple_of(row, tf)
        return (
            pltpu.make_async_copy(wgu_hbm.at[:, pl.ds(col, 2 * tf)], wgu_buf.at[slot],
                                  w_sems.at[0, slot]),
            pltpu.make_async_copy(wd_hbm.at[pl.ds(row, tf), :], wd_buf.at[slot],
                                  w_sems.at[1, slot]),
        )

    @pl.when(i == 0)
    def _():
        for copy in row_copies(0) + w_copies(0, 0):
            copy.start()

    def iteration(f, first, final):
        slot = (i * n_f + f) % 2
        for copy in w_copies(f, slot):
            copy.wait()
        if final:
            @pl.when(i + 1 < n_i)
            def _():
                for copy in w_copies(0, 1 - slot):
                    copy.start()
        else:
            for copy in w_copies(f + 1, 1 - slot):
                copy.start()
        if first:
            for copy in row_copies(i):
                copy.wait()
            for r in range(0, tm, TAIL_ROWS):
                rows = pl.ds(r, TAIL_ROWS)
                x1 = (x1_buf[rows, :] + _dot(mm_buf[rows, :], wom_ref[...])
                      + _dot(mr_buf[rows, :], wor_ref[...]))
                x1_buf[rows, :] = x1
                hn_buf[rows, :] = (x1 * _rms_scale(x1) * g2_ref[...]).astype(BF16)
        part = _swiglu_down(hn_buf[...], wgu_buf[slot], wd_buf[slot])
        if first:
            y_ref[...] = x1_buf[...] + part
        else:
            y_ref[...] += part
        if final:
            x2 = y_ref[...]
            y_ref[...] = x2 * _rms_scale(x2) * gf_ref[...]

    def middle(f, carry):
        @pl.when((f == 1) & (i + 1 < n_i))
        def _():
            for copy in row_copies(i + 1):
                copy.start()

        iteration(f, False, False)
        return carry

    iteration(0, True, False)
    lax.fori_loop(1, n_f - 1, middle, 0)
    iteration(n_f - 1, False, True)


def _tail_stream(x, mix_m, mix_r, w, *, tm, tf):
    n = x.shape[0]
    n_f = D_FF // tf
    assert n_f >= 3
    w_out_spec = lambda half: pl.BlockSpec((D_GROUP, D_MODEL), lambda i, half=half: (half, 0),
                                           pipeline_mode=pl.Buffered(1))
    return pl.pallas_call(
        functools.partial(_tail_stream_kernel, n_f=n_f, tf=tf),
        grid=(n // tm,),
        in_specs=[
            pl.BlockSpec(memory_space=pl.ANY),
            pl.BlockSpec(memory_space=pl.ANY),
            pl.BlockSpec(memory_space=pl.ANY),
            w_out_spec(0), w_out_spec(1),
            pl.BlockSpec((1, D_MODEL), lambda i: (0, 0)),
            pl.BlockSpec(memory_space=pl.ANY),
            pl.BlockSpec(memory_space=pl.ANY),
            pl.BlockSpec((1, D_MODEL), lambda i: (0, 0)),
        ],
        out_specs=pl.BlockSpec((tm, D_MODEL), lambda i: (i, 0)),
        out_shape=jax.ShapeDtypeStruct((n, D_MODEL), F32),
        scratch_shapes=[
            pltpu.VMEM((tm, D_MODEL), F32),
            pltpu.VMEM((tm, D_GROUP), BF16),
            pltpu.VMEM((tm, D_GROUP), BF16),
            pltpu.VMEM((tm, D_MODEL), BF16),
            pltpu.VMEM((2, D_MODEL, 2 * tf), BF16),
            pltpu.VMEM((2, tf, D_MODEL), BF16),
            pltpu.SemaphoreType.DMA((3,)),
            pltpu.SemaphoreType.DMA((2, 2)),
        ],
        compiler_params=_params(1, TAIL_VMEM_LIMIT),
        name="tail_stream",
    )(x, mix_m, mix_r, w["w_out"], w["w_out"], w["g2"], w["w_gu"], w["w_down"], w["g_final"])


def _rope_tables(pos):
    half = HEAD_DIM // 2
    freqs = ROPE_BASE ** (-jnp.arange(half, dtype=F32) / half)
    ang = pos[:, None] * freqs[None, :]
    return jnp.cos(ang), jnp.sin(ang)


def _prompt_trunk(x, w):
    batch, seq, _ = x.shape
    x2d = x.reshape(batch * seq, D_MODEL)
    rope = _rope_tables(jnp.arange(seq, dtype=F32))
    mix_m, c_new, n_new, m_new, mix_r, r_new = _prompt_front(x2d, w, rope, batch=batch, seq=seq)
    y = _tail_stream(x2d, mix_m, mix_r, w, tm=FFN_TM, tf=FFN_TF)
    return (y.reshape(batch, seq, D_MODEL), c_new[None], n_new[None],
            m_new[:, :N_HEADS, 0][None], r_new[None])


def _sample_trunk(x, init, w):
    batch, seq, _ = x.shape
    x2d = x.reshape(batch * seq, D_MODEL)
    proj, gt = _inproj(x2d, w["g1"], w["w_in"], w["w_gatecols"], w["b_gate"],
                       tm=INPROJ_TM, tn=INPROJ_TN)
    gt3 = gt.reshape(N_GATES, batch, seq).transpose(1, 0, 2)
    gt3 = jnp.pad(gt3, ((0, 0), (0, 0), (0, max(0, LANES - seq))))
    c0, n0, m0, r0 = init
    m0 = jnp.pad(m0, ((0, 0), (0, SUBLANES - N_HEADS)))
    m0 = jnp.broadcast_to(m0[:, :, None], (batch, SUBLANES, LANES))
    mix_m, c_new, n_new, m_new = _mlstm_single_chunk(proj, gt3, w["gain_m"], (c0, n0, m0),
                                                     batch=batch, seq=seq, group=SAMPLE_GROUP)
    cos, sin = _rope_tables(PAST_LEN + jnp.arange(seq, dtype=F32))
    mix_r, r_new = _retention_single_chunk(proj, w["log_gamma"], cos, sin, w["gain_r"], r0,
                                           batch=batch, seq=seq, group=SAMPLE_GROUP)
    x1, hn = _outproj(x2d, mix_m, mix_r, w["w_out"], w["g2"], tm=OUTPROJ_TM)
    y, w_gu16, w_down16 = _ffn_pack(hn, x1, w)
    w16 = dict(w_gu=w_gu16, w_down=w_down16)
    return (y.reshape(batch, seq, D_MODEL), c_new[None], n_new[None],
            m_new[:, :N_HEADS, 0][None], r_new[None]), w16


def kernel(x_prompt, x_sample, state_mlstm_C, state_mlstm_n, state_mlstm_m, state_ret,
           g_norm1, w_in, b_gates, g_mlstm_norm, g_ret_norm, w_out, g_norm2,
           w_gate, w_up, w_down, g_final):
    w_in0 = w_in[0].astype(BF16)
    w = dict(
        g1=g_norm1[0].reshape(1, D_MODEL),
        w_in=w_in0,
        w_gatecols=jnp.pad(w_in0[:, D_PROJ:], ((0, 0), (0, LANES - N_GATES))),
        b_gate=jnp.pad(b_gates[0], (0, LANES - N_GATES)).reshape(1, LANES),
        gain_m=g_mlstm_norm[0].reshape(1, D_GROUP),
        gain_r=g_ret_norm[0].reshape(1, D_GROUP),
        w_out=w_out[0].astype(BF16),
        g2=g_norm2[0].reshape(1, D_MODEL),
        w_gate=w_gate[0],
        w_up=w_up[0],
        w_down=w_down[0],
        g_final=g_final.reshape(1, D_MODEL),
        log_gamma=jnp.log(1.0 - jnp.exp2(-5.0 - jnp.arange(N_HEADS, dtype=F32))),
    )
    init_s = (state_mlstm_C[0], state_mlstm_n[0], state_mlstm_m[0], state_ret[0])
    (y_s, c_s, n_s, m_s, r_s), w16 = _sample_trunk(x_sample, init_s, w)
    y_p, c_p, n_p, m_p, r_p = _prompt_trunk(x_prompt, {**w, **w16})
    return (y_p, y_s, c_p, n_p, m_p, r_p, c_s, n_s, m_s, r_s)
```

```python
import functools

import jax
import jax.numpy as jnp
from jax import lax
from jax.experimental import pallas as pl
from jax.experimental.pallas import tpu as pltpu

F32 = jnp.float32
BF16 = jnp.bfloat16

D_MODEL = 2048
N_HEADS = 4
HEAD_DIM = 256
D_GROUP = N_HEADS * HEAD_DIM
D_PROJ = 8 * D_GROUP
N_GATES = 2 * N_HEADS
D_FF = 5632
ROPE_BASE = 10000.0
PAST_LEN = 4096
EPS = 1e-6
K_SCALE = HEAD_DIM ** -0.5

LANES = 128
SUBLANES = 8
VMEM_LIMIT = 60 * 1024 * 1024
TAIL_VMEM_LIMIT = 62 * 1024 * 1024

PROMPT_CHUNK = 256
INPROJ_TM = 1024
INPROJ_TN = 2048
SAMPLE_GROUP = 4
OUTPROJ_TM = 512
FFN_TM = 1024
TAIL_ROWS = 256
FFN_TF = 512
FFN_GRANULE = 256

_NT = (((1,), (1,)), ((), ()))
_TN = (((0,), (0,)), ((), ()))


def _dot(a, b):
    return jnp.dot(a, b, preferred_element_type=F32)


def _rms_scale(x):
    return lax.rsqrt(jnp.mean(x * x, axis=-1, keepdims=True) + EPS)


def _params(n_axes, vmem_limit=VMEM_LIMIT):
    return pltpu.CompilerParams(
        dimension_semantics=("arbitrary",) * n_axes, vmem_limit_bytes=vmem_limit)


def _inproj_prologue(x_ref, g1_ref, wgate_ref, bgate_ref, gt_ref, xn_ref):
    x = x_ref[...]
    xn = (x * _rms_scale(x) * g1_ref[...]).astype(BF16)
    xn_ref[...] = xn
    gates = _dot(xn, wgate_ref[...]) + bgate_ref[...]
    gt_ref[...] = gates.T[0:N_GATES, :]


def _inproj_kernel(x_ref, g1_ref, w_ref, wgate_ref, bgate_ref, proj_ref, gt_ref, xn_ref):
    @pl.when(pl.program_id(1) == 0)
    def _():
        _inproj_prologue(x_ref, g1_ref, wgate_ref, bgate_ref, gt_ref, xn_ref)

    proj_ref[...] = _dot(xn_ref[...], w_ref[...]).astype(BF16)


def _inproj(x, g1, w_main, w_gate, b_gate, *, tm, tn):
    n = x.shape[0]
    return pl.pallas_call(
        _inproj_kernel,
        grid=(n // tm, D_PROJ // tn),
        in_specs=[
            pl.BlockSpec((tm, D_MODEL), lambda i, j: (i, 0)),
            pl.BlockSpec((1, D_MODEL), lambda i, j: (0, 0)),
            pl.BlockSpec((D_MODEL, tn), lambda i, j: (0, j)),
            pl.BlockSpec((D_MODEL, LANES), lambda i, j: (0, 0)),
            pl.BlockSpec((1, LANES), lambda i, j: (0, 0)),
        ],
        out_specs=[
            pl.BlockSpec((tm, tn), lambda i, j: (i, j)),
            pl.BlockSpec((N_GATES, tm), lambda i, j: (0, i)),
        ],
        out_shape=[
            jax.ShapeDtypeStruct((n, D_PROJ), BF16),
            jax.ShapeDtypeStruct((N_GATES, n), F32),
        ],
        scratch_shapes=[pltpu.VMEM((tm, D_MODEL), BF16)],
        compiler_params=_params(2),
        name="inproj",
    )(x, g1, w_main, w_gate, b_gate)


def _lane_scan(x, combine, identity):
    width = x.shape[1]
    lane = lax.broadcasted_iota(jnp.int32, x.shape, 1)
    shift = 1
    while shift < width:
        shifted = jnp.where(lane >= shift, pltpu.roll(x, shift, axis=1), identity)
        x = combine(x, shifted)
        shift *= 2
    return x


def _mlstm_phases(q_ref, k_ref, v_ref, o_ref, gates, gain_ref, mix_ref, c_ref, n_ref, m_ref, L):
    b = _lane_scan(jax.nn.log_sigmoid(gates), jnp.add, 0.0)
    b = pltpu.roll(b, N_HEADS, axis=0)
    u = gates - b
    m_prev = m_ref[0][:, 0:1]
    big_m = jnp.maximum(m_prev, _lane_scan(u, jnp.maximum, -jnp.inf))
    a = jnp.exp(m_prev - big_m)
    e = jnp.exp(-(b + big_m))
    m_last = big_m[:, L - 1:L]
    g = jnp.exp(u - m_last)
    dec = a[:, L - 1:L]
    m_new = b[:, L - 1:L] + m_last
    rows = jnp.concatenate(
        [big_m, a, e, g, jnp.zeros((LANES - 4 * SUBLANES, gates.shape[1]), F32)], axis=0)
    cols = rows.T

    t_idx = lax.broadcasted_iota(jnp.int32, (L, L), 0)
    s_idx = lax.broadcasted_iota(jnp.int32, (L, L), 1)
    causal = s_idx <= t_idx

    for h in range(N_HEADS):
        sl = slice(h * HEAD_DIM, (h + 1) * HEAD_DIM)
        qh = q_ref[:, sl]
        kh = k_ref[:, sl] * jnp.asarray(K_SCALE, BF16)
        vh = v_ref[:, sl]
        m_col = cols[:L, h:h + 1]
        a_col = cols[:L, SUBLANES + h:SUBLANES + h + 1]
        e_col = cols[:L, 2 * SUBLANES + h:2 * SUBLANES + h + 1]
        g_col = cols[:L, 3 * SUBLANES + h:3 * SUBLANES + h + 1]
        u_row = u[h:h + 1, :L]

        s = lax.dot_general(qh, kh, _NT, preferred_element_type=F32)
        yield
        sw = s * jnp.exp(jnp.where(causal, u_row - m_col, -jnp.inf))
        c_h = c_ref[0, h]
        n_h = n_ref[0, h:h + 1, :]
        num = a_col * _dot(qh, c_h.astype(BF16)) + _dot(sw.astype(BF16), vh)
        qn = jnp.sum(qh.astype(F32) * n_h, axis=-1, keepdims=True)
        den = a_col * qn + jnp.sum(sw, axis=-1, keepdims=True)
        hh = num * (1.0 / jnp.maximum(jnp.abs(den), e_col))
        y = hh * _rms_scale(hh) * gain_ref[:, sl] * jax.nn.sigmoid(o_ref[:, sl].astype(F32))
        mix_ref[:, sl] = y.astype(BF16)

        kg = kh.astype(F32) * g_col
        dec_h = dec[h:h + 1, :]
        c_ref[0, h] = dec_h * c_h + lax.dot_general(
            kg.astype(BF16), vh, _TN, preferred_element_type=F32)
        n_ref[0, h:h + 1, :] = dec_h * n_h + jnp.sum(kg, axis=0, keepdims=True)
        if h == N_HEADS - 1:
            m_ref[0] = jnp.broadcast_to(m_new, (SUBLANES, LANES))
        yield


def _run(*phase_generators):
    live = list(phase_generators)
    while live:
        live = [g for g in live if next(g, StopIteration) is not StopIteration]


def _seq_views(s, rows, token_refs, state_refs):
    return ([ref.at[pl.ds(s * rows, rows), :] for ref in token_refs],
            [ref.at[pl.ds(s, 1)] for ref in state_refs])


def _mlstm_state_specs(index_map3, index_map4, group=1):
    return [
        pl.BlockSpec((group, N_HEADS, HEAD_DIM, HEAD_DIM), index_map4),
        pl.BlockSpec((group, N_HEADS, HEAD_DIM), index_map3),
        pl.BlockSpec((group, SUBLANES, LANES), index_map3),
    ]


def _mlstm_state_shapes(batch):
    return [
        jax.ShapeDtypeStruct((batch, N_HEADS, HEAD_DIM, HEAD_DIM), F32),
        jax.ShapeDtypeStruct((batch, N_HEADS, HEAD_DIM), F32),
        jax.ShapeDtypeStruct((batch, SUBLANES, LANES), F32),
    ]


def _ret_phases(lg_ref, q_ref, k_ref, v_ref, g_ref, cos_ref, sin_ref, gain_ref, mix_ref, r_ref, L):
    half = HEAD_DIM // 2
    cos = cos_ref[...]
    sin = sin_ref[...]
    cos_k = cos * K_SCALE
    sin_k = sin * K_SCALE
    t_idx = lax.broadcasted_iota(jnp.int32, (L, L), 0)
    s_idx = lax.broadcasted_iota(jnp.int32, (L, L), 1)
    diff = t_idx - s_idx
    diff_pos = jnp.maximum(diff, 0).astype(F32)
    t_col = lax.broadcasted_iota(jnp.int32, (L, 1), 0).astype(F32)

    for h in range(N_HEADS):
        lo = slice(h * HEAD_DIM, h * HEAD_DIM + half)
        hi = slice(h * HEAD_DIM + half, (h + 1) * HEAD_DIM)
        sl = slice(h * HEAD_DIM, (h + 1) * HEAD_DIM)
        lg = lg_ref[h]
        q1 = q_ref[:, lo].astype(F32)
        q2 = q_ref[:, hi].astype(F32)
        k1 = k_ref[:, lo].astype(F32)
        k2 = k_ref[:, hi].astype(F32)
        qr = jnp.concatenate([q1 * cos - q2 * sin, q2 * cos + q1 * sin], axis=-1).astype(BF16)
        kr = jnp.concatenate([k1 * cos_k - k2 * sin_k, k2 * cos_k + k1 * sin_k], axis=-1)
        vh = v_ref[:, sl]

        dmask = jnp.where(diff >= 0, jnp.exp(lg * diff_pos), 0.0)
        inter = jnp.exp(lg * (t_col + 1.0))
        kdec = jnp.exp(lg * (L - 1.0 - t_col))
        cdec = jnp.exp(lg * jnp.full((1, 1), L, F32))

        s = lax.dot_general(qr, kr.astype(BF16), _NT, preferred_element_type=F32)
        yield
        r_h = r_ref[0, h]
        o = _dot((s * dmask).astype(BF16), vh) + inter * _dot(qr, r_h.astype(BF16))
        r_ref[0, h] = cdec * r_h + lax.dot_general(
            (kr * kdec).astype(BF16), vh, _TN, preferred_element_type=F32)

        oc = o - jnp.mean(o, axis=-1, keepdims=True)
        y = oc * _rms_scale(oc) * gain_ref[:, sl] * jax.nn.silu(g_ref[:, sl].astype(F32))
        mix_ref[:, sl] = y.astype(BF16)
        yield


def _sample_mixers_kernel(lg_ref, mq_ref, mk_ref, mv_ref, mo_ref, gt_ref, gain_m_ref,
                          c0_ref, n0_ref, m0_ref, rq_ref, rk_ref, rv_ref, rg_ref, cos_ref, sin_ref,
                          gain_r_ref, r0_ref, mixm_ref, c_ref, n_ref, m_ref, mixr_ref, r_ref,
                          *, chunk, group):
    c_ref[...] = c0_ref[...]
    n_ref[...] = n0_ref[...]
    m_ref[...] = m0_ref[...]
    r_ref[...] = r0_ref[...]
    phases = []
    for s in range(group):
        (q, k, v, o, mix), (c, n, m) = _seq_views(
            s, chunk, (mq_ref, mk_ref, mv_ref, mo_ref, mixm_ref), (c_ref, n_ref, m_ref))
        phases.append(_mlstm_phases(q, k, v, o, gt_ref[s], gain_m_ref, mix, c, n, m, chunk))
        (q, k, v, g, mix), (r,) = _seq_views(
            s, chunk, (rq_ref, rk_ref, rv_ref, rg_ref, mixr_ref), (r_ref,))
        phases.append(_ret_phases(lg_ref, q, k, v, g, cos_ref, sin_ref, gain_r_ref, mix, r, chunk))
    _run(*phases)


def _sample_mixers(proj, gt3, rope, init, w, *, batch, seq, group):
    cos, sin = rope
    tok = lambda col: pl.BlockSpec((group * seq, D_GROUP), lambda b, col=col: (b, col))
    const = lambda shape: pl.BlockSpec(shape, lambda b: (0,) * len(shape))
    state4 = pl.BlockSpec((group, N_HEADS, HEAD_DIM, HEAD_DIM), lambda b: (b, 0, 0, 0))
    state_specs = _mlstm_state_specs(lambda b: (b, 0, 0), lambda b: (b, 0, 0, 0), group)
    mix_spec = pl.BlockSpec((group * seq, D_GROUP), lambda b: (b, 0))
    mix_shape = jax.ShapeDtypeStruct((batch * seq, D_GROUP), BF16)
    c0, n0, m0, r0 = init
    return pl.pallas_call(
        functools.partial(_sample_mixers_kernel, chunk=seq, group=group),
        grid=(batch // group,),
        in_specs=([pl.BlockSpec(memory_space=pltpu.SMEM), tok(0), tok(1), tok(2), tok(3),
                   pl.BlockSpec((group, SUBLANES, gt3.shape[2]), lambda b: (b, 0, 0)),
                   const((1, D_GROUP))] + state_specs
                  + [tok(4), tok(5), tok(6), tok(7), const((seq, HEAD_DIM // 2)),
                     const((seq, HEAD_DIM // 2)), const((1, D_GROUP)), state4]),
        out_specs=[mix_spec] + state_specs + [mix_spec, state4],
        out_shape=([mix_shape] + _mlstm_state_shapes(batch)
                   + [mix_shape, jax.ShapeDtypeStruct((batch, N_HEADS, HEAD_DIM, HEAD_DIM), F32)]),
        compiler_params=_params(1),
        name="sample_mixers",
    )(w["log_gamma"], proj, proj, proj, proj, gt3, w["gain_m"], c0, n0, m0,
      proj, proj, proj, proj, cos, sin, w["gain_r"], r0)


def _prompt_kernel(lg_ref, x_ref, g1_ref, w_ref, wgate_ref, bgate_ref, gain_m_ref, cos_ref,
                   sin_ref, gain_r_ref, mixm_ref, c_ref, n_ref, m_ref, mixr_ref, r_ref,
                   xn_scr, proj_scr, gt_scr, *, n_chunks, chunks_per_seq, chunk):
    t = pl.program_id(0)
    piece = D_PROJ // (2 * N_HEADS)

    def inproj_pieces(slot):
        x = x_ref[...]
        xn_scr[...] = (x * _rms_scale(x) * g1_ref[...]).astype(BF16)
        gates = _dot(xn_scr[...], wgate_ref[...]) + bgate_ref[...]
        gt_scr[slot] = gates.T[0:N_GATES, :]
        for p in range(D_PROJ // piece):
            cols = pl.ds(p * piece, piece)
            proj_scr[slot, :, cols] = _dot(xn_scr[...], w_ref[:, cols]).astype(BF16)
            yield

    def mixers(slot):
        group = lambda k: proj_scr.at[slot, :, pl.ds(k * D_GROUP, D_GROUP)]
        return [
            _mlstm_phases(group(0), group(1), group(2), group(3), gt_scr[slot], gain_m_ref,
                          mixm_ref, c_ref, n_ref, m_ref, chunk),
            _ret_phases(lg_ref, group(4), group(5), group(6), group(7), cos_ref, sin_ref,
                        gain_r_ref, mixr_ref, r_ref, chunk),
        ]

    @pl.when((t >= 1) & ((t - 1) % chunks_per_seq == 0))
    def _():
        c_ref[...] = jnp.zeros_like(c_ref)
        n_ref[...] = jnp.zeros_like(n_ref)
        m_ref[...] = jnp.zeros_like(m_ref)
        r_ref[...] = jnp.zeros_like(r_ref)

    @pl.when(t == 0)
    def _():
        _run(inproj_pieces(0))

    @pl.when((t >= 1) & (t < n_chunks))
    def _():
        slot = t % 2
        pieces = inproj_pieces(slot)
        heads = mixers(1 - slot)
        for _ in range(N_HEADS):
            for phases in heads:
                next(phases)
                next(pieces)
                next(phases)

    @pl.when(t == n_chunks)
    def _():
        _run(*mixers((n_chunks - 1) % 2))


def _prompt_front(x2d, w, rope, *, batch, seq):
    chunk = PROMPT_CHUNK
    chunks_per_seq = seq // chunk
    n_chunks = batch * chunks_per_seq
    cos, sin = rope
    const = lambda shape, **kw: pl.BlockSpec(shape, lambda t: (0,) * len(shape), **kw)
    mixed = lambda t: jnp.maximum(t - 1, 0)
    rope_spec = pl.BlockSpec((chunk, HEAD_DIM // 2), lambda t: (mixed(t) % chunks_per_seq, 0))
    mix_spec = pl.BlockSpec((chunk, D_GROUP), lambda t: (mixed(t), 0))
    mix_shape = jax.ShapeDtypeStruct((batch * seq, D_GROUP), BF16)
    seq_of = lambda t: mixed(t) // chunks_per_seq
    state4 = pl.BlockSpec((1, N_HEADS, HEAD_DIM, HEAD_DIM), lambda t: (seq_of(t), 0, 0, 0))
    state_specs = _mlstm_state_specs(lambda t: (seq_of(t), 0, 0), lambda t: (seq_of(t), 0, 0, 0))
    return pl.pallas_call(
        functools.partial(_prompt_kernel, n_chunks=n_chunks, chunks_per_seq=chunks_per_seq,
                          chunk=chunk),
        grid=(n_chunks + 1,),
        in_specs=[
            pl.BlockSpec(memory_space=pltpu.SMEM),
            pl.BlockSpec((chunk, D_MODEL), lambda t: (jnp.minimum(t, n_chunks - 1), 0)),
            const((1, D_MODEL)),
            const((D_MODEL, D_PROJ), pipeline_mode=pl.Buffered(1)),
            const((D_MODEL, LANES), pipeline_mode=pl.Buffered(1)),
            const((1, LANES)),
            const((1, D_GROUP)),
            rope_spec, rope_spec,
            const((1, D_GROUP)),
        ],
        out_specs=[mix_spec] + state_specs + [mix_spec, state4],
        out_shape=([mix_shape] + _mlstm_state_shapes(batch)
                   + [mix_shape, jax.ShapeDtypeStruct((batch, N_HEADS, HEAD_DIM, HEAD_DIM), F32)]),
        scratch_shapes=[
            pltpu.VMEM((chunk, D_MODEL), BF16),
            pltpu.VMEM((2, chunk, D_PROJ), BF16),
            pltpu.VMEM((2, N_GATES, chunk), F32),
        ],
        compiler_params=_params(1),
        name="prompt_front",
    )(w["log_gamma"], x2d, w["g1"], w["w_in"], w["w_gatecols"], w["b_gate"], w["gain_m"],
      cos, sin, w["gain_r"])


def _outproj_kernel(x_ref, mm_ref, mr_ref, wm_ref, wr_ref, g2_ref, x1_ref, hn_ref):
    x1 = x_ref[...] + _dot(mm_ref[...], wm_ref[...]) + _dot(mr_ref[...], wr_ref[...])
    x1_ref[...] = x1
    hn_ref[...] = (x1 * _rms_scale(x1) * g2_ref[...]).astype(BF16)


def _outproj(x, mix_m, mix_r, w_out, g2, *, tm):
    n = x.shape[0]
    w_spec = lambda half: pl.BlockSpec((D_GROUP, D_MODEL), lambda i, half=half: (half, 0),
                                       pipeline_mode=pl.Buffered(1))
    return pl.pallas_call(
        _outproj_kernel,
        grid=(n // tm,),
        in_specs=[
            pl.BlockSpec((tm, D_MODEL), lambda i: (i, 0)),
            pl.BlockSpec((tm, D_GROUP), lambda i: (i, 0)),
            pl.BlockSpec((tm, D_GROUP), lambda i: (i, 0)),
            w_spec(0), w_spec(1),
            pl.BlockSpec((1, D_MODEL), lambda i: (0, 0)),
        ],
        out_specs=[
            pl.BlockSpec((tm, D_MODEL), lambda i: (i, 0)),
            pl.BlockSpec((tm, D_MODEL), lambda i: (i, 0)),
        ],
        out_shape=[
            jax.ShapeDtypeStruct((n, D_MODEL), F32),
            jax.ShapeDtypeStruct((n, D_MODEL), BF16),
        ],
        compiler_params=_params(1),
        name="outproj",
    )(x, mix_m, mix_r, w_out, w_out, g2)


def _swiglu_down(hn, w_gu, w_down):
    gu = _dot(hn, w_gu)
    acts = []
    for base in range(0, w_gu.shape[1], 2 * FFN_GRANULE):
        gate = gu[:, base:base + FFN_GRANULE]
        up = gu[:, base + FFN_GRANULE:base + 2 * FFN_GRANULE]
        acts.append((jax.nn.silu(gate) * up).astype(BF16))
    act = acts[0] if len(acts) == 1 else jnp.concatenate(acts, axis=-1)
    return _dot(act, w_down)


def _ffn_pack_kernel(hn_ref, x1_ref, wg32_ref, wu32_ref, wd32_ref, gf_ref, y_ref, wgu_ref, wd_ref):
    f = pl.program_id(0)
    last = pl.num_programs(0) - 1

    def step(first, final):
        wgu_ref[:, :FFN_GRANULE] = wg32_ref[...].astype(BF16)
        wgu_ref[:, FFN_GRANULE:] = wu32_ref[...].astype(BF16)
        wd_ref[...] = wd32_ref[...].astype(BF16)
        part = _swiglu_down(hn_ref[...], wgu_ref[...], wd_ref[...])
        if first:
            y_ref[...] = x1_ref[...] + part
        else:
            y_ref[...] += part
        if final:
            x2 = y_ref[...]
            y_ref[...] = x2 * _rms_scale(x2) * gf_ref[...]

    pl.when(f == 0)(functools.partial(step, True, False))
    pl.when((f > 0) & (f < last))(functools.partial(step, False, False))
    pl.when(f == last)(functools.partial(step, False, True))


def _ffn_pack(hn, x1, w):
    n = hn.shape[0]
    tf = FFN_GRANULE
    assert D_FF // tf >= 3
    rows_spec = lambda: pl.BlockSpec((n, D_MODEL), lambda f: (0, 0), pipeline_mode=pl.Buffered(1))
    col_spec = pl.BlockSpec((D_MODEL, tf), lambda f: (0, f))
    gu_spec = pl.BlockSpec((D_MODEL, 2 * tf), lambda f: (0, f))
    down_spec = pl.BlockSpec((tf, D_MODEL), lambda f: (f, 0))
    return pl.pallas_call(
        _ffn_pack_kernel,
        grid=(D_FF // tf,),
        in_specs=[rows_spec(), rows_spec(), col_spec, col_spec, down_spec,
                  pl.BlockSpec((1, D_MODEL), lambda f: (0, 0))],
        out_specs=[pl.BlockSpec((n, D_MODEL), lambda f: (0, 0)), gu_spec, down_spec],
        out_shape=[jax.ShapeDtypeStruct((n, D_MODEL), F32),
                   jax.ShapeDtypeStruct((D_MODEL, 2 * D_FF), BF16),
                   jax.ShapeDtypeStruct((D_FF, D_MODEL), BF16)],
        compiler_params=_params(1),
        name="ffn_pack",
    )(hn, x1, w["w_gate"], w["w_up"], w["w_down"], w["g_final"])


def _tail_stream_kernel(x_hbm, mm_hbm, mr_hbm, wom_ref, wor_ref, g2_ref, wgu_hbm, wd_hbm, gf_ref,
                        y_ref, x1_buf, mm_buf, mr_buf, hn_buf, wgu_buf, wd_buf, row_sems, w_sems,
                        *, n_f, tf):
    i = pl.program_id(0)
    n_i = pl.num_programs(0)
    tm = x1_buf.shape[0]

    def row_copies(tile):
        rows = pl.ds(tile * tm, tm)
        return (
            pltpu.make_async_copy(x_hbm.at[rows, :], x1_buf, row_sems.at[0]),
            pltpu.make_async_copy(mm_hbm.at[rows, :], mm_buf, row_sems.at[1]),
            pltpu.make_async_copy(mr_hbm.at[rows, :], mr_buf, row_sems.at[2]),
        )

    def w_copies(f, slot):
        col, row = f * (2 * tf), f * tf
        if not isinstance(f, int):
            col, row = pl.multiple_of(col, 2 * tf), pl.multiple_of(row, tf)
        return (
            pltpu.make_async_copy(wgu_hbm.at[:, pl.ds(col, 2 * tf)], wgu_buf.at[slot],
                                  w_sems.at[0, slot]),
            pltpu.make_async_copy(wd_hbm.at[pl.ds(row, tf), :], wd_buf.at[slot],
                                  w_sems.at[1, slot]),
        )

    @pl.when(i == 0)
    def _():
        for copy in row_copies(0) + w_copies(0, 0):
            copy.start()

    def iteration(f, first, final):
        slot = (i * n_f + f) % 2
        for copy in w_copies(f, slot):
            copy.wait()
        if final:
            @pl.when(i + 1 < n_i)
            def _():
                for copy in w_copies(0, 1 - slot):
                    copy.start()
        else:
            for copy in w_copies(f + 1, 1 - slot):
                copy.start()
        if first:
            for copy in row_copies(i):
                copy.wait()
            for r in range(0, tm, TAIL_ROWS):
                rows = pl.ds(r, TAIL_ROWS)
                x1 = (x1_buf[rows, :] + _dot(mm_buf[rows, :], wom_ref[...])
                      + _dot(mr_buf[rows, :], wor_ref[...]))
                x1_buf[rows, :] = x1
                hn_buf[rows, :] = (x1 * _rms_scale(x1) * g2_ref[...]).astype(BF16)
        part = _swiglu_down(hn_buf[...], wgu_buf[slot], wd_buf[slot])
        if first:
            y_ref[...] = x1_buf[...] + part
        else:
            y_ref[...] += part
        if final:
            x2 = y_ref[...]
            y_ref[...] = x2 * _rms_scale(x2) * gf_ref[...]

    def middle(f, carry):
        @pl.when((f == 1) & (i + 1 < n_i))
        def _():
            for copy in row_copies(i + 1):
                copy.start()

        iteration(f, False, False)
        return carry

    iteration(0, True, False)
    lax.fori_loop(1, n_f - 1, middle, 0)
    iteration(n_f - 1, False, True)


def _tail_stream(x, mix_m, mix_r, w, *, tm, tf):
    n = x.shape[0]
    n_f = D_FF // tf
    assert n_f >= 3
    w_out_spec = lambda half: pl.BlockSpec((D_GROUP, D_MODEL), lambda i, half=half: (half, 0),
                                           pipeline_mode=pl.Buffered(1))
    return pl.pallas_call(
        functools.partial(_tail_stream_kernel, n_f=n_f, tf=tf),
        grid=(n // tm,),
        in_specs=[
            pl.BlockSpec(memory_space=pl.ANY),
            pl.BlockSpec(memory_space=pl.ANY),
            pl.BlockSpec(memory_space=pl.ANY),
            w_out_spec(0), w_out_spec(1),
            pl.BlockSpec((1, D_MODEL), lambda i: (0, 0)),
            pl.BlockSpec(memory_space=pl.ANY),
            pl.BlockSpec(memory_space=pl.ANY),
            pl.BlockSpec((1, D_MODEL), lambda i: (0, 0)),
        ],
        out_specs=pl.BlockSpec((tm, D_MODEL), lambda i: (i, 0)),
        out_shape=jax.ShapeDtypeStruct((n, D_MODEL), F32),
        scratch_shapes=[
            pltpu.VMEM((tm, D_MODEL), F32),
            pltpu.VMEM((tm, D_GROUP), BF16),
            pltpu.VMEM((tm, D_GROUP), BF16),
            pltpu.VMEM((tm, D_MODEL), BF16),
            pltpu.VMEM((2, D_MODEL, 2 * tf), BF16),
            pltpu.VMEM((2, tf, D_MODEL), BF16),
            pltpu.SemaphoreType.DMA((3,)),
            pltpu.SemaphoreType.DMA((2, 2)),
        ],
        compiler_params=_params(1, TAIL_VMEM_LIMIT),
        name="tail_stream",
    )(x, mix_m, mix_r, w["w_out"], w["w_out"], w["g2"], w["w_gu"], w["w_down"], w["g_final"])


def _rope_tables(pos):
    half = HEAD_DIM // 2
    freqs = ROPE_BASE ** (-jnp.arange(half, dtype=F32) / half)
    ang = pos[:, None] * freqs[None, :]
    return jnp.cos(ang), jnp.sin(ang)


def _prompt_trunk(x, w):
    batch, seq, _ = x.shape
    x2d = x.reshape(batch * seq, D_MODEL)
    rope = _rope_tables(jnp.arange(seq, dtype=F32))
    mix_m, c_new, n_new, m_new, mix_r, r_new = _prompt_front(x2d, w, rope, batch=batch, seq=seq)
    y = _tail_stream(x2d, mix_m, mix_r, w, tm=FFN_TM, tf=FFN_TF)
    return (y.reshape(batch, seq, D_MODEL), c_new[None], n_new[None],
            m_new[:, :N_HEADS, 0][None], r_new[None])


def _sample_trunk(x, init, w):
    batch, seq, _ = x.shape
    x2d = x.reshape(batch * seq, D_MODEL)
    proj, gt = _inproj(x2d, w["g1"], w["w_in"], w["w_gatecols"], w["b_gate"],
                       tm=INPROJ_TM, tn=INPROJ_TN)
    gt3 = gt.reshape(N_GATES, batch, seq).transpose(1, 0, 2)
    gt3 = jnp.pad(gt3, ((0, 0), (0, 0), (0, max(0, LANES - seq))))
    c0, n0, m0, r0 = init
    m0 = jnp.pad(m0, ((0, 0), (0, SUBLANES - N_HEADS)))
    m0 = jnp.broadcast_to(m0[:, :, None], (batch, SUBLANES, LANES))
    rope = _rope_tables(PAST_LEN + jnp.arange(seq, dtype=F32))
    mix_m, c_new, n_new, m_new, mix_r, r_new = _sample_mixers(
        proj, gt3, rope, (c0, n0, m0, r0), w, batch=batch, seq=seq, group=SAMPLE_GROUP)
    x1, hn = _outproj(x2d, mix_m, mix_r, w["w_out"], w["g2"], tm=OUTPROJ_TM)
    y, w_gu16, w_down16 = _ffn_pack(hn, x1, w)
    w16 = dict(w_gu=w_gu16, w_down=w_down16)
    return (y.reshape(batch, seq, D_MODEL), c_new[None], n_new[None],
            m_new[:, :N_HEADS, 0][None], r_new[None]), w16


def kernel(x_prompt, x_sample, state_mlstm_C, state_mlstm_n, state_mlstm_m, state_ret,
           g_norm1, w_in, b_gates, g_mlstm_norm, g_ret_norm, w_out, g_norm2,
           w_gate, w_up, w_down, g_final):
    w_in0 = w_in[0].astype(BF16)
    w = dict(
        g1=g_norm1[0].reshape(1, D_MODEL),
        w_in=w_in0,
        w_gatecols=jnp.pad(w_in0[:, D_PROJ:], ((0, 0), (0, LANES - N_GATES))),
        b_gate=jnp.pad(b_gates[0], (0, LANES - N_GATES)).reshape(1, LANES),
        gain_m=g_mlstm_norm[0].reshape(1, D_GROUP),
        gain_r=g_ret_norm[0].reshape(1, D_GROUP),
        w_out=w_out[0].astype(BF16),
        g2=g_norm2[0].reshape(1, D_MODEL),
        w_gate=w_gate[0],
        w_up=w_up[0],
        w_down=w_down[0],
        g_final=g_final.reshape(1, D_MODEL),
        log_gamma=jnp.log(1.0 - jnp.exp2(-5.0 - jnp.arange(N_HEADS, dtype=F32))),
    )
    init_s = (state_mlstm_C[0], state_mlstm_n[0], state_mlstm_m[0], state_ret[0])
    (y_s, c_s, n_s, m_s, r_s), w16 = _sample_trunk(x_sample, init_s, w)
    y_p, c_p, n_p, m_p, r_p = _prompt_trunk(x_prompt, {**w, **w16})
    return (y_p, y_s, c_p, n_p, m_p, r_p, c_s, n_s, m_s, r_s)
```

```python
import functools

import jax
import jax.numpy as jnp
from jax import lax
from jax.experimental import pallas as pl
from jax.experimental.pallas import tpu as pltpu

F32 = jnp.float32
BF16 = jnp.bfloat16

D_MODEL = 2048
N_HEADS = 4
HEAD_DIM = 256
D_GROUP = N_HEADS * HEAD_DIM
D_PROJ = 8 * D_GROUP
N_GATES = 2 * N_HEADS
D_FF = 5632
ROPE_BASE = 10000.0
PAST_LEN = 4096
EPS = 1e-6
K_SCALE = HEAD_DIM ** -0.5

LANES = 128
SUBLANES = 8
VMEM_LIMIT = 60 * 1024 * 1024
TAIL_VMEM_LIMIT = 62 * 1024 * 1024

PROMPT_CHUNK = 256
INPROJ_TM = 1024
INPROJ_TN = 2048
SAMPLE_GROUP = 4
FFN_TM = 1024
TAIL_ROWS = 256
FFN_TF = 512
FFN_GRANULE = 256

_NT = (((1,), (1,)), ((), ()))
_TN = (((0,), (0,)), ((), ()))


def _dot(a, b):
    return jnp.dot(a, b, preferred_element_type=F32)


def _rms_scale(x):
    return lax.rsqrt(jnp.mean(x * x, axis=-1, keepdims=True) + EPS)


def _params(n_axes, vmem_limit=VMEM_LIMIT):
    return pltpu.CompilerParams(
        dimension_semantics=("arbitrary",) * n_axes, vmem_limit_bytes=vmem_limit)


def _inproj_prologue(x_ref, g1_ref, wgate_ref, bgate_ref, gt_ref, xn_ref):
    x = x_ref[...]
    xn = (x * _rms_scale(x) * g1_ref[...]).astype(BF16)
    xn_ref[...] = xn
    gates = _dot(xn, wgate_ref[...]) + bgate_ref[...]
    gt_ref[...] = gates.T[0:N_GATES, :]


def _inproj_kernel(x_ref, g1_ref, w_ref, wgate_ref, bgate_ref, proj_ref, gt_ref, xn_ref):
    @pl.when(pl.program_id(1) == 0)
    def _():
        _inproj_prologue(x_ref, g1_ref, wgate_ref, bgate_ref, gt_ref, xn_ref)

    proj_ref[...] = _dot(xn_ref[...], w_ref[...]).astype(BF16)


def _inproj(x, g1, w_main, w_gate, b_gate, *, tm, tn):
    n = x.shape[0]
    return pl.pallas_call(
        _inproj_kernel,
        grid=(n // tm, D_PROJ // tn),
        in_specs=[
            pl.BlockSpec((tm, D_MODEL), lambda i, j: (i, 0)),
            pl.BlockSpec((1, D_MODEL), lambda i, j: (0, 0)),
            pl.BlockSpec((D_MODEL, tn), lambda i, j: (0, j)),
            pl.BlockSpec((D_MODEL, LANES), lambda i, j: (0, 0)),
            pl.BlockSpec((1, LANES), lambda i, j: (0, 0)),
        ],
        out_specs=[
            pl.BlockSpec((tm, tn), lambda i, j: (i, j)),
            pl.BlockSpec((N_GATES, tm), lambda i, j: (0, i)),
        ],
        out_shape=[
            jax.ShapeDtypeStruct((n, D_PROJ), BF16),
            jax.ShapeDtypeStruct((N_GATES, n), F32),
        ],
        scratch_shapes=[pltpu.VMEM((tm, D_MODEL), BF16)],
        compiler_params=_params(2),
        name="inproj",
    )(x, g1, w_main, w_gate, b_gate)


def _lane_scan(x, combine, identity):
    width = x.shape[1]
    lane = lax.broadcasted_iota(jnp.int32, x.shape, 1)
    shift = 1
    while shift < width:
        shifted = jnp.where(lane >= shift, pltpu.roll(x, shift, axis=1), identity)
        x = combine(x, shifted)
        shift *= 2
    return x


def _mlstm_phases(q_ref, k_ref, v_ref, o_ref, gates, gain_ref, mix_ref, c_ref, n_ref, m_ref, L):
    b = _lane_scan(jax.nn.log_sigmoid(gates), jnp.add, 0.0)
    b = pltpu.roll(b, N_HEADS, axis=0)
    u = gates - b
    m_prev = m_ref[0][:, 0:1]
    big_m = jnp.maximum(m_prev, _lane_scan(u, jnp.maximum, -jnp.inf))
    a = jnp.exp(m_prev - big_m)
    e = jnp.exp(-(b + big_m))
    m_last = big_m[:, L - 1:L]
    g = jnp.exp(u - m_last)
    dec = a[:, L - 1:L]
    m_new = b[:, L - 1:L] + m_last
    rows = jnp.concatenate(
        [big_m, a, e, g, jnp.zeros((LANES - 4 * SUBLANES, gates.shape[1]), F32)], axis=0)
    cols = rows.T

    t_idx = lax.broadcasted_iota(jnp.int32, (L, L), 0)
    s_idx = lax.broadcasted_iota(jnp.int32, (L, L), 1)
    causal = s_idx <= t_idx

    for h in range(N_HEADS):
        sl = slice(h * HEAD_DIM, (h + 1) * HEAD_DIM)
        qh = q_ref[:, sl]
        kh = k_ref[:, sl] * jnp.asarray(K_SCALE, BF16)
        vh = v_ref[:, sl]
        m_col = cols[:L, h:h + 1]
        a_col = cols[:L, SUBLANES + h:SUBLANES + h + 1]
        e_col = cols[:L, 2 * SUBLANES + h:2 * SUBLANES + h + 1]
        g_col = cols[:L, 3 * SUBLANES + h:3 * SUBLANES + h + 1]
        u_row = u[h:h + 1, :L]

        s = lax.dot_general(qh, kh, _NT, preferred_element_type=F32)
        yield
        sw = s * jnp.exp(jnp.where(causal, u_row - m_col, -jnp.inf))
        c_h = c_ref[0, h]
        n_h = n_ref[0, h:h + 1, :]
        num = a_col * _dot(qh, c_h.astype(BF16)) + _dot(sw.astype(BF16), vh)
        qn = jnp.sum(qh.astype(F32) * n_h, axis=-1, keepdims=True)
        den = a_col * qn + jnp.sum(sw, axis=-1, keepdims=True)
        hh = num * (1.0 / jnp.maximum(jnp.abs(den), e_col))
        y = hh * _rms_scale(hh) * gain_ref[:, sl] * jax.nn.sigmoid(o_ref[:, sl].astype(F32))
        mix_ref[:, sl] = y.astype(BF16)

        kg = kh.astype(F32) * g_col
        dec_h = dec[h:h + 1, :]
        c_ref[0, h] = dec_h * c_h + lax.dot_general(
            kg.astype(BF16), vh, _TN, preferred_element_type=F32)
        n_ref[0, h:h + 1, :] = dec_h * n_h + jnp.sum(kg, axis=0, keepdims=True)
        if h == N_HEADS - 1:
            m_ref[0] = jnp.broadcast_to(m_new, (SUBLANES, LANES))
        yield


def _run(*phase_generators):
    live = list(phase_generators)
    while live:
        live = [g for g in live if next(g, StopIteration) is not StopIteration]


def _seq_views(s, rows, token_refs, state_refs):
    return ([ref.at[pl.ds(s * rows, rows), :] for ref in token_refs],
            [ref.at[pl.ds(s, 1)] for ref in state_refs])


def _mlstm_state_specs(index_map3, index_map4, group=1):
    return [
        pl.BlockSpec((group, N_HEADS, HEAD_DIM, HEAD_DIM), index_map4),
        pl.BlockSpec((group, N_HEADS, HEAD_DIM), index_map3),
        pl.BlockSpec((group, SUBLANES, LANES), index_map3),
    ]


def _mlstm_state_shapes(batch):
    return [
        jax.ShapeDtypeStruct((batch, N_HEADS, HEAD_DIM, HEAD_DIM), F32),
        jax.ShapeDtypeStruct((batch, N_HEADS, HEAD_DIM), F32),
        jax.ShapeDtypeStruct((batch, SUBLANES, LANES), F32),
    ]


def _ret_phases(lg_ref, q_ref, k_ref, v_ref, g_ref, cos_ref, sin_ref, gain_ref, mix_ref, r_ref, L):
    half = HEAD_DIM // 2
    cos = cos_ref[...]
    sin = sin_ref[...]
    cos_k = cos * K_SCALE
    sin_k = sin * K_SCALE
    t_idx = lax.broadcasted_iota(jnp.int32, (L, L), 0)
    s_idx = lax.broadcasted_iota(jnp.int32, (L, L), 1)
    diff = t_idx - s_idx
    diff_pos = jnp.maximum(diff, 0).astype(F32)
    t_col = lax.broadcasted_iota(jnp.int32, (L, 1), 0).astype(F32)

    for h in range(N_HEADS):
        lo = slice(h * HEAD_DIM, h * HEAD_DIM + half)
        hi = slice(h * HEAD_DIM + half, (h + 1) * HEAD_DIM)
        sl = slice(h * HEAD_DIM, (h + 1) * HEAD_DIM)
        lg = lg_ref[h]
        q1 = q_ref[:, lo].astype(F32)
        q2 = q_ref[:, hi].astype(F32)
        k1 = k_ref[:, lo].astype(F32)
        k2 = k_ref[:, hi].astype(F32)
        qr = jnp.concatenate([q1 * cos - q2 * sin, q2 * cos + q1 * sin], axis=-1).astype(BF16)
        kr = jnp.concatenate([k1 * cos_k - k2 * sin_k, k2 * cos_k + k1 * sin_k], axis=-1)
        vh = v_ref[:, sl]

        dmask = jnp.where(diff >= 0, jnp.exp(lg * diff_pos), 0.0)
        inter = jnp.exp(lg * (t_col + 1.0))
        kdec = jnp.exp(lg * (L - 1.0 - t_col))
        cdec = jnp.exp(lg * jnp.full((1, 1), L, F32))

        s = lax.dot_general(qr, kr.astype(BF16), _NT, preferred_element_type=F32)
        yield
        r_h = r_ref[0, h]
        o = _dot((s * dmask).astype(BF16), vh) + inter * _dot(qr, r_h.astype(BF16))
        r_ref[0, h] = cdec * r_h + lax.dot_general(
            (kr * kdec).astype(BF16), vh, _TN, preferred_element_type=F32)

        oc = o - jnp.mean(o, axis=-1, keepdims=True)
        y = oc * _rms_scale(oc) * gain_ref[:, sl] * jax.nn.silu(g_ref[:, sl].astype(F32))
        mix_ref[:, sl] = y.astype(BF16)
        yield


def _sample_mixers_kernel(lg_ref, mq_ref, mk_ref, mv_ref, mo_ref, gt_ref, gain_m_ref,
                          c0_ref, n0_ref, m0_ref, rq_ref, rk_ref, rv_ref, rg_ref, cos_ref, sin_ref,
                          gain_r_ref, r0_ref, mixm_ref, c_ref, n_ref, m_ref, mixr_ref, r_ref,
                          *, chunk, group):
    c_ref[...] = c0_ref[...]
    n_ref[...] = n0_ref[...]
    m_ref[...] = m0_ref[...]
    r_ref[...] = r0_ref[...]
    phases = []
    for s in range(group):
        (q, k, v, o, mix), (c, n, m) = _seq_views(
            s, chunk, (mq_ref, mk_ref, mv_ref, mo_ref, mixm_ref), (c_ref, n_ref, m_ref))
        phases.append(_mlstm_phases(q, k, v, o, gt_ref[s], gain_m_ref, mix, c, n, m, chunk))
        (q, k, v, g, mix), (r,) = _seq_views(
            s, chunk, (rq_ref, rk_ref, rv_ref, rg_ref, mixr_ref), (r_ref,))
        phases.append(_ret_phases(lg_ref, q, k, v, g, cos_ref, sin_ref, gain_r_ref, mix, r, chunk))
    _run(*phases)


def _sample_mixers(proj, gt3, rope, init, w, *, batch, seq, group):
    cos, sin = rope
    tok = lambda col: pl.BlockSpec((group * seq, D_GROUP), lambda b, col=col: (b, col))
    const = lambda shape: pl.BlockSpec(shape, lambda b: (0,) * len(shape))
    state4 = pl.BlockSpec((group, N_HEADS, HEAD_DIM, HEAD_DIM), lambda b: (b, 0, 0, 0))
    state_specs = _mlstm_state_specs(lambda b: (b, 0, 0), lambda b: (b, 0, 0, 0), group)
    mix_spec = pl.BlockSpec((group * seq, D_GROUP), lambda b: (b, 0))
    mix_shape = jax.ShapeDtypeStruct((batch * seq, D_GROUP), BF16)
    c0, n0, m0, r0 = init
    return pl.pallas_call(
        functools.partial(_sample_mixers_kernel, chunk=seq, group=group),
        grid=(batch // group,),
        in_specs=([pl.BlockSpec(memory_space=pltpu.SMEM), tok(0), tok(1), tok(2), tok(3),
                   pl.BlockSpec((group, SUBLANES, gt3.shape[2]), lambda b: (b, 0, 0)),
                   const((1, D_GROUP))] + state_specs
                  + [tok(4), tok(5), tok(6), tok(7), const((seq, HEAD_DIM // 2)),
                     const((seq, HEAD_DIM // 2)), const((1, D_GROUP)), state4]),
        out_specs=[mix_spec] + state_specs + [mix_spec, state4],
        out_shape=([mix_shape] + _mlstm_state_shapes(batch)
                   + [mix_shape, jax.ShapeDtypeStruct((batch, N_HEADS, HEAD_DIM, HEAD_DIM), F32)]),
        compiler_params=_params(1),
        name="sample_mixers",
    )(w["log_gamma"], proj, proj, proj, proj, gt3, w["gain_m"], c0, n0, m0,
      proj, proj, proj, proj, cos, sin, w["gain_r"], r0)


def _prompt_kernel(lg_ref, x_ref, g1_ref, w_ref, wgate_ref, bgate_ref, gain_m_ref, cos_ref,
                   sin_ref, gain_r_ref, mixm_ref, c_ref, n_ref, m_ref, mixr_ref, r_ref,
                   xn_scr, proj_scr, gt_scr, *, n_chunks, chunks_per_seq, chunk):
    t = pl.program_id(0)
    piece = D_PROJ // (2 * N_HEADS)

    def inproj_pieces(slot):
        x = x_ref[...]
        xn_scr[...] = (x * _rms_scale(x) * g1_ref[...]).astype(BF16)
        gates = _dot(xn_scr[...], wgate_ref[...]) + bgate_ref[...]
        gt_scr[slot] = gates.T[0:N_GATES, :]
        for p in range(D_PROJ // piece):
            cols = pl.ds(p * piece, piece)
            proj_scr[slot, :, cols] = _dot(xn_scr[...], w_ref[:, cols]).astype(BF16)
            yield

    def mixers(slot):
        group = lambda k: proj_scr.at[slot, :, pl.ds(k * D_GROUP, D_GROUP)]
        return [
            _mlstm_phases(group(0), group(1), group(2), group(3), gt_scr[slot], gain_m_ref,
                          mixm_ref, c_ref, n_ref, m_ref, chunk),
            _ret_phases(lg_ref, group(4), group(5), group(6), group(7), cos_ref, sin_ref,
                        gain_r_ref, mixr_ref, r_ref, chunk),
        ]

    @pl.when((t >= 1) & ((t - 1) % chunks_per_seq == 0))
    def _():
        c_ref[...] = jnp.zeros_like(c_ref)
        n_ref[...] = jnp.zeros_like(n_ref)
        m_ref[...] = jnp.zeros_like(m_ref)
        r_ref[...] = jnp.zeros_like(r_ref)

    @pl.when(t == 0)
    def _():
        _run(inproj_pieces(0))

    @pl.when((t >= 1) & (t < n_chunks))
    def _():
        slot = t % 2
        pieces = inproj_pieces(slot)
        heads = mixers(1 - slot)
        for _ in range(N_HEADS):
            for phases in heads:
                next(phases)
                next(pieces)
                next(phases)

    @pl.when(t == n_chunks)
    def _():
        _run(*mixers((n_chunks - 1) % 2))


def _prompt_front(x2d, w, rope, *, batch, seq):
    chunk = PROMPT_CHUNK
    chunks_per_seq = seq // chunk
    n_chunks = batch * chunks_per_seq
    cos, sin = rope
    const = lambda shape, **kw: pl.BlockSpec(shape, lambda t: (0,) * len(shape), **kw)
    mixed = lambda t: jnp.maximum(t - 1, 0)
    rope_spec = pl.BlockSpec((chunk, HEAD_DIM // 2), lambda t: (mixed(t) % chunks_per_seq, 0))
    mix_spec = pl.BlockSpec((chunk, D_GROUP), lambda t: (mixed(t), 0))
    mix_shape = jax.ShapeDtypeStruct((batch * seq, D_GROUP), BF16)
    seq_of = lambda t: mixed(t) // chunks_per_seq
    state4 = pl.BlockSpec((1, N_HEADS, HEAD_DIM, HEAD_DIM), lambda t: (seq_of(t), 0, 0, 0))
    state_specs = _mlstm_state_specs(lambda t: (seq_of(t), 0, 0), lambda t: (seq_of(t), 0, 0, 0))
    return pl.pallas_call(
        functools.partial(_prompt_kernel, n_chunks=n_chunks, chunks_per_seq=chunks_per_seq,
                          chunk=chunk),
        grid=(n_chunks + 1,),
        in_specs=[
            pl.BlockSpec(memory_space=pltpu.SMEM),
            pl.BlockSpec((chunk, D_MODEL), lambda t: (jnp.minimum(t, n_chunks - 1), 0)),
            const((1, D_MODEL)),
            const((D_MODEL, D_PROJ), pipeline_mode=pl.Buffered(1)),
            const((D_MODEL, LANES), pipeline_mode=pl.Buffered(1)),
            const((1, LANES)),
            const((1, D_GROUP)),
            rope_spec, rope_spec,
            const((1, D_GROUP)),
        ],
        out_specs=[mix_spec] + state_specs + [mix_spec, state4],
        out_shape=([mix_shape] + _mlstm_state_shapes(batch)
                   + [mix_shape, jax.ShapeDtypeStruct((batch, N_HEADS, HEAD_DIM, HEAD_DIM), F32)]),
        scratch_shapes=[
            pltpu.VMEM((chunk, D_MODEL), BF16),
            pltpu.VMEM((2, chunk, D_PROJ), BF16),
            pltpu.VMEM((2, N_GATES, chunk), F32),
        ],
        compiler_params=_params(1),
        name="prompt_front",
    )(w["log_gamma"], x2d, w["g1"], w["w_in"], w["w_gatecols"], w["b_gate"], w["gain_m"],
      cos, sin, w["gain_r"])


def _outproj_rows(x, mix_m, mix_r, wom_ref, wor_ref, g2_ref):
    x1 = x + _dot(mix_m, wom_ref[...]) + _dot(mix_r, wor_ref[...])
    return x1, (x1 * _rms_scale(x1) * g2_ref[...]).astype(BF16)


def _swiglu_down(hn, w_gu, w_down):
    gu = _dot(hn, w_gu)
    acts = []
    for base in range(0, w_gu.shape[1], 2 * FFN_GRANULE):
        gate = gu[:, base:base + FFN_GRANULE]
        up = gu[:, base + FFN_GRANULE:base + 2 * FFN_GRANULE]
        acts.append((jax.nn.silu(gate) * up).astype(BF16))
    act = acts[0] if len(acts) == 1 else jnp.concatenate(acts, axis=-1)
    return _dot(act, w_down)


def _sample_tail_kernel(x_ref, mm_ref, mr_ref, wom_ref, wor_ref, g2_ref, wg32_ref, wu32_ref,
                        wd32_ref, gf_ref, y_ref, wgu_ref, wd_ref, hn_scr):
    f = pl.program_id(0)
    last = pl.num_programs(0) - 1

    def step(first, final):
        wgu_ref[:, :FFN_GRANULE] = wg32_ref[...].astype(BF16)
        wgu_ref[:, FFN_GRANULE:] = wu32_ref[...].astype(BF16)
        wd_ref[...] = wd32_ref[...].astype(BF16)
        if first:
            for r in range(0, x_ref.shape[0], TAIL_ROWS):
                rows = pl.ds(r, TAIL_ROWS)
                y_ref[rows, :], hn_scr[rows, :] = _outproj_rows(
                    x_ref[rows, :], mm_ref[rows, :], mr_ref[rows, :], wom_ref, wor_ref, g2_ref)
        y_ref[...] += _swiglu_down(hn_scr[...], wgu_ref[...], wd_ref[...])
        if final:
            x2 = y_ref[...]
            y_ref[...] = x2 * _rms_scale(x2) * gf_ref[...]

    pl.when(f == 0)(functools.partial(step, True, False))
    pl.when((f > 0) & (f < last))(functools.partial(step, False, False))
    pl.when(f == last)(functools.partial(step, False, True))


def _sample_tail(x, mix_m, mix_r, w):
    n = x.shape[0]
    tf = FFN_GRANULE
    assert D_FF // tf >= 3
    once = dict(pipeline_mode=pl.Buffered(1))
    rows = lambda width: pl.BlockSpec((n, width), lambda f: (0, 0), **once)
    w_out_spec = lambda half: pl.BlockSpec((D_GROUP, D_MODEL), lambda f, half=half: (half, 0),
                                           **once)
    col_spec = pl.BlockSpec((D_MODEL, tf), lambda f: (0, f))
    gu_spec = pl.BlockSpec((D_MODEL, 2 * tf), lambda f: (0, f))
    down_spec = pl.BlockSpec((tf, D_MODEL), lambda f: (f, 0))
    vec_spec = pl.BlockSpec((1, D_MODEL), lambda f: (0, 0))
    return pl.pallas_call(
        _sample_tail_kernel,
        grid=(D_FF // tf,),
        in_specs=[rows(D_MODEL), rows(D_GROUP), rows(D_GROUP), w_out_spec(0), w_out_spec(1),
                  vec_spec, col_spec, col_spec, down_spec, vec_spec],
        out_specs=[rows(D_MODEL), gu_spec, down_spec],
        out_shape=[jax.ShapeDtypeStruct((n, D_MODEL), F32),
                   jax.ShapeDtypeStruct((D_MODEL, 2 * D_FF), BF16),
                   jax.ShapeDtypeStruct((D_FF, D_MODEL), BF16)],
        scratch_shapes=[pltpu.VMEM((n, D_MODEL), BF16)],
        compiler_params=_params(1),
        name="sample_tail",
    )(x, mix_m, mix_r, w["w_out"], w["w_out"], w["g2"], w["w_gate"], w["w_up"], w["w_down"],
      w["g_final"])


def _tail_stream_kernel(x_hbm, mm_hbm, mr_hbm, wom_ref, wor_ref, g2_ref, wgu_hbm, wd_hbm, gf_ref,
                        y_ref, x1_buf, mm_buf, mr_buf, hn_buf, wgu_buf, wd_buf, row_sems, w_sems,
                        *, n_f, tf):
    i = pl.program_id(0)
    n_i = pl.num_programs(0)
    tm = x1_buf.shape[0]

    def row_copies(tile):
        rows = pl.ds(tile * tm, tm)
        return (
            pltpu.make_async_copy(x_hbm.at[rows, :], x1_buf, row_sems.at[0]),
            pltpu.make_async_copy(mm_hbm.at[rows, :], mm_buf, row_sems.at[1]),
            pltpu.make_async_copy(mr_hbm.at[rows, :], mr_buf, row_sems.at[2]),
        )

    def w_copies(f, slot):
        col, row = f * (2 * tf), f * tf
        if not isinstance(f, int):
            col, row = pl.multiple_of(col, 2 * tf), pl.multiple_of(row, tf)
        return (
            pltpu.make_async_copy(wgu_hbm.at[:, pl.ds(col, 2 * tf)], wgu_buf.at[slot],
                                  w_sems.at[0, slot]),
            pltpu.make_async_copy(wd_hbm.at[pl.ds(row, tf), :], wd_buf.at[slot],
                                  w_sems.at[1, slot]),
        )

    @pl.when(i == 0)
    def _():
        for copy in row_copies(0) + w_copies(0, 0):
            copy.start()

    def iteration(f, first, final):
        slot = (i * n_f + f) % 2
        for copy in w_copies(f, slot):
            copy.wait()
        if final:
            @pl.when(i + 1 < n_i)
            def _():
                for copy in w_copies(0, 1 - slot):
                    copy.start()
        else:
            for copy in w_copies(f + 1, 1 - slot):
                copy.start()
        if first:
            for copy in row_copies(i):
                copy.wait()
            for r in range(0, tm, TAIL_ROWS):
                rows = pl.ds(r, TAIL_ROWS)
                x1_buf[rows, :], hn_buf[rows, :] = _outproj_rows(
                    x1_buf[rows, :], mm_buf[rows, :], mr_buf[rows, :], wom_ref, wor_ref, g2_ref)
        part = _swiglu_down(hn_buf[...], wgu_buf[slot], wd_buf[slot])
        if first:
            y_ref[...] = x1_buf[...] + part
        else:
            y_ref[...] += part
        if final:
            x2 = y_ref[...]
            y_ref[...] = x2 * _rms_scale(x2) * gf_ref[...]

    def middle(f, carry):
        @pl.when((f == 1) & (i + 1 < n_i))
        def _():
            for copy in row_copies(i + 1):
                copy.start()

        iteration(f, False, False)
        return carry

    iteration(0, True, False)
    lax.fori_loop(1, n_f - 1, middle, 0)
    iteration(n_f - 1, False, True)


def _tail_stream(x, mix_m, mix_r, w, *, tm, tf):
    n = x.shape[0]
    n_f = D_FF // tf
    assert n_f >= 3
    w_out_spec = lambda half: pl.BlockSpec((D_GROUP, D_MODEL), lambda i, half=half: (half, 0),
                                           pipeline_mode=pl.Buffered(1))
    return pl.pallas_call(
        functools.partial(_tail_stream_kernel, n_f=n_f, tf=tf),
        grid=(n // tm,),
        in_specs=[
            pl.BlockSpec(memory_space=pl.ANY),
            pl.BlockSpec(memory_space=pl.ANY),
            pl.BlockSpec(memory_space=pl.ANY),
            w_out_spec(0), w_out_spec(1),
            pl.BlockSpec((1, D_MODEL), lambda i: (0, 0)),
            pl.BlockSpec(memory_space=pl.ANY),
            pl.BlockSpec(memory_space=pl.ANY),
            pl.BlockSpec((1, D_MODEL), lambda i: (0, 0)),
        ],
        out_specs=pl.BlockSpec((tm, D_MODEL), lambda i: (i, 0)),
        out_shape=jax.ShapeDtypeStruct((n, D_MODEL), F32),
        scratch_shapes=[
            pltpu.VMEM((tm, D_MODEL), F32),
            pltpu.VMEM((tm, D_GROUP), BF16),
            pltpu.VMEM((tm, D_GROUP), BF16),
            pltpu.VMEM((tm, D_MODEL), BF16),
            pltpu.VMEM((2, D_MODEL, 2 * tf), BF16),
            pltpu.VMEM((2, tf, D_MODEL), BF16),
            pltpu.SemaphoreType.DMA((3,)),
            pltpu.SemaphoreType.DMA((2, 2)),
        ],
        compiler_params=_params(1, TAIL_VMEM_LIMIT),
        name="tail_stream",
    )(x, mix_m, mix_r, w["w_out"], w["w_out"], w["g2"], w["w_gu"], w["w_down"], w["g_final"])


def _rope_tables(pos):
    half = HEAD_DIM // 2
    freqs = ROPE_BASE ** (-jnp.arange(half, dtype=F32) / half)
    ang = pos[:, None] * freqs[None, :]
    return jnp.cos(ang), jnp.sin(ang)


def _prompt_trunk(x, w):
    batch, seq, _ = x.shape
    x2d = x.reshape(batch * seq, D_MODEL)
    rope = _rope_tables(jnp.arange(seq, dtype=F32))
    mix_m, c_new, n_new, m_new, mix_r, r_new = _prompt_front(x2d, w, rope, batch=batch, seq=seq)
    y = _tail_stream(x2d, mix_m, mix_r, w, tm=FFN_TM, tf=FFN_TF)
    return (y.reshape(batch, seq, D_MODEL), c_new[None], n_new[None],
            m_new[:, :N_HEADS, 0][None], r_new[None])


def _sample_trunk(x, init, w):
    batch, seq, _ = x.shape
    x2d = x.reshape(batch * seq, D_MODEL)
    proj, gt = _inproj(x2d, w["g1"], w["w_in"], w["w_gatecols"], w["b_gate"],
                       tm=INPROJ_TM, tn=INPROJ_TN)
    gt3 = gt.reshape(N_GATES, batch, seq).transpose(1, 0, 2)
    gt3 = jnp.pad(gt3, ((0, 0), (0, 0), (0, max(0, LANES - seq))))
    c0, n0, m0, r0 = init
    m0 = jnp.pad(m0, ((0, 0), (0, SUBLANES - N_HEADS)))
    m0 = jnp.broadcast_to(m0[:, :, None], (batch, SUBLANES, LANES))
    rope = _rope_tables(PAST_LEN + jnp.arange(seq, dtype=F32))
    mix_m, c_new, n_new, m_new, mix_r, r_new = _sample_mixers(
        proj, gt3, rope, (c0, n0, m0, r0), w, batch=batch, seq=seq, group=SAMPLE_GROUP)
    y, w_gu16, w_down16 = _sample_tail(x2d, mix_m, mix_r, w)
    w16 = dict(w_gu=w_gu16, w_down=w_down16)
    return (y.reshape(batch, seq, D_MODEL), c_new[None], n_new[None],
            m_new[:, :N_HEADS, 0][None], r_new[None]), w16


def kernel(x_prompt, x_sample, state_mlstm_C, state_mlstm_n, state_mlstm_m, state_ret,
           g_norm1, w_in, b_gates, g_mlstm_norm, g_ret_norm, w_out, g_norm2,
           w_gate, w_up, w_down, g_final):
    w_in0 = w_in[0].astype(BF16)
    w = dict(
        g1=g_norm1[0].reshape(1, D_MODEL),
        w_in=w_in0,
        w_gatecols=jnp.pad(w_in0[:, D_PROJ:], ((0, 0), (0, LANES - N_GATES))),
        b_gate=jnp.pad(b_gates[0], (0, LANES - N_GATES)).reshape(1, LANES),
        gain_m=g_mlstm_norm[0].reshape(1, D_GROUP),
        gain_r=g_ret_norm[0].reshape(1, D_GROUP),
        w_out=w_out[0].astype(BF16),
        g2=g_norm2[0].reshape(1, D_MODEL),
        w_gate=w_gate[0],
        w_up=w_up[0],
        w_down=w_down[0],
        g_final=g_final.reshape(1, D_MODEL),
        log_gamma=jnp.log(1.0 - jnp.exp2(-5.0 - jnp.arange(N_HEADS, dtype=F32))),
    )
    init_s = (state_mlstm_C[0], state_mlstm_n[0], state_mlstm_m[0], state_ret[0])
    (y_s, c_s, n_s, m_s, r_s), w16 = _sample_trunk(x_sample, init_s, w)
    y_p, c_p, n_p, m_p, r_p = _prompt_trunk(x_prompt, {**w, **w16})
    return (y_p, y_s, c_p, n_p, m_p, r_p, c_s, n_s, m_s, r_s)
```

```python
import functools

import jax
import jax.numpy as jnp
from jax import lax
from jax.experimental import pallas as pl
from jax.experimental.pallas import tpu as pltpu

F32 = jnp.float32
BF16 = jnp.bfloat16

D_MODEL = 2048
N_HEADS = 4
HEAD_DIM = 256
D_GROUP = N_HEADS * HEAD_DIM
D_PROJ = 8 * D_GROUP
N_GATES = 2 * N_HEADS
D_FF = 5632
ROPE_BASE = 10000.0
PAST_LEN = 4096
EPS = 1e-6
K_SCALE = HEAD_DIM ** -0.5

LANES = 128
SUBLANES = 8
VMEM_LIMIT = 60 * 1024 * 1024
TAIL_VMEM_LIMIT = 62 * 1024 * 1024

PROMPT_CHUNK = 256
INPROJ_TM = 1024
INPROJ_TN = 2048
SAMPLE_GROUP = 4
OUTPROJ_TM = 512
FFN_TM = 1024
TAIL_ROWS = 256
FFN_TF = 512
FFN_GRANULE = 256

_NT = (((1,), (1,)), ((), ()))
_TN = (((0,), (0,)), ((), ()))


def _dot(a, b):
    return jnp.dot(a, b, preferred_element_type=F32)


def _rms_scale(x):
    return lax.rsqrt(jnp.mean(x * x, axis=-1, keepdims=True) + EPS)


def _params(n_axes, vmem_limit=VMEM_LIMIT):
    return pltpu.CompilerParams(
        dimension_semantics=("arbitrary",) * n_axes, vmem_limit_bytes=vmem_limit)


def _inproj_prologue(x_ref, g1_ref, wgate_ref, bgate_ref, gt_ref, xn_ref):
    x = x_ref[...]
    xn = (x * _rms_scale(x) * g1_ref[...]).astype(BF16)
    xn_ref[...] = xn
    gates = _dot(xn, wgate_ref[...]) + bgate_ref[...]
    gt_ref[...] = gates.T[0:N_GATES, :]


def _inproj_kernel(x_ref, g1_ref, w_ref, wgate_ref, bgate_ref, proj_ref, gt_ref, xn_ref):
    @pl.when(pl.program_id(1) == 0)
    def _():
        _inproj_prologue(x_ref, g1_ref, wgate_ref, bgate_ref, gt_ref, xn_ref)

    proj_ref[...] = _dot(xn_ref[...], w_ref[...]).astype(BF16)


def _inproj(x, g1, w_main, w_gate, b_gate, *, tm, tn):
    n = x.shape[0]
    return pl.pallas_call(
        _inproj_kernel,
        grid=(n // tm, D_PROJ // tn),
        in_specs=[
            pl.BlockSpec((tm, D_MODEL), lambda i, j: (i, 0)),
            pl.BlockSpec((1, D_MODEL), lambda i, j: (0, 0)),
            pl.BlockSpec((D_MODEL, tn), lambda i, j: (0, j)),
            pl.BlockSpec((D_MODEL, LANES), lambda i, j: (0, 0)),
            pl.BlockSpec((1, LANES), lambda i, j: (0, 0)),
        ],
        out_specs=[
            pl.BlockSpec((tm, tn), lambda i, j: (i, j)),
            pl.BlockSpec((N_GATES, tm), lambda i, j: (0, i)),
        ],
        out_shape=[
            jax.ShapeDtypeStruct((n, D_PROJ), BF16),
            jax.ShapeDtypeStruct((N_GATES, n), F32),
        ],
        scratch_shapes=[pltpu.VMEM((tm, D_MODEL), BF16)],
        compiler_params=_params(2),
        name="inproj",
    )(x, g1, w_main, w_gate, b_gate)


def _lane_scan(x, combine, identity):
    width = x.shape[1]
    lane = lax.broadcasted_iota(jnp.int32, x.shape, 1)
    shift = 1
    while shift < width:
        shifted = jnp.where(lane >= shift, pltpu.roll(x, shift, axis=1), identity)
        x = combine(x, shifted)
        shift *= 2
    return x


def _mlstm_phases(q_ref, k_ref, v_ref, o_ref, gates, gain_ref, mix_ref, c_ref, n_ref, m_ref, L):
    b = _lane_scan(jax.nn.log_sigmoid(gates), jnp.add, 0.0)
    b = pltpu.roll(b, N_HEADS, axis=0)
    u = gates - b
    m_prev = m_ref[0][:, 0:1]
    big_m = jnp.maximum(m_prev, _lane_scan(u, jnp.maximum, -jnp.inf))
    a = jnp.exp(m_prev - big_m)
    e = jnp.exp(-(b + big_m))
    m_last = big_m[:, L - 1:L]
    g = jnp.exp(u - m_last)
    dec = a[:, L - 1:L]
    m_new = b[:, L - 1:L] + m_last
    rows = jnp.concatenate(
        [big_m, a, e, g, jnp.zeros((LANES - 4 * SUBLANES, gates.shape[1]), F32)], axis=0)
    cols = rows.T

    t_idx = lax.broadcasted_iota(jnp.int32, (L, L), 0)
    s_idx = lax.broadcasted_iota(jnp.int32, (L, L), 1)
    causal = s_idx <= t_idx

    for h in range(N_HEADS):
        sl = slice(h * HEAD_DIM, (h + 1) * HEAD_DIM)
        qh = q_ref[:, sl]
        kh = k_ref[:, sl] * jnp.asarray(K_SCALE, BF16)
        vh = v_ref[:, sl]
        m_col = cols[:L, h:h + 1]
        a_col = cols[:L, SUBLANES + h:SUBLANES + h + 1]
        e_col = cols[:L, 2 * SUBLANES + h:2 * SUBLANES + h + 1]
        g_col = cols[:L, 3 * SUBLANES + h:3 * SUBLANES + h + 1]
        u_row = u[h:h + 1, :L]

        s = lax.dot_general(qh, kh, _NT, preferred_element_type=F32)
        yield
        sw = s * jnp.exp(jnp.where(causal, u_row - m_col, -jnp.inf))
        c_h = c_ref[0, h]
        n_h = n_ref[0, h:h + 1, :]
        num = a_col * _dot(qh, c_h.astype(BF16)) + _dot(sw.astype(BF16), vh)
        qn = jnp.sum(qh.astype(F32) * n_h, axis=-1, keepdims=True)
        den = a_col * qn + jnp.sum(sw, axis=-1, keepdims=True)
        hh = num * (1.0 / jnp.maximum(jnp.abs(den), e_col))
        y = hh * _rms_scale(hh) * gain_ref[:, sl] * jax.nn.sigmoid(o_ref[:, sl].astype(F32))
        mix_ref[:, sl] = y.astype(BF16)

        kg = kh.astype(F32) * g_col
        dec_h = dec[h:h + 1, :]
        c_ref[0, h] = dec_h * c_h + lax.dot_general(
            kg.astype(BF16), vh, _TN, preferred_element_type=F32)
        n_ref[0, h:h + 1, :] = dec_h * n_h + jnp.sum(kg, axis=0, keepdims=True)
        if h == N_HEADS - 1:
            m_ref[0] = jnp.broadcast_to(m_new, (SUBLANES, LANES))
        yield


def _run(*phase_generators):
    live = list(phase_generators)
    while live:
        live = [g for g in live if next(g, StopIteration) is not StopIteration]


def _seq_views(s, rows, token_refs, state_refs):
    return ([ref.at[pl.ds(s * rows, rows), :] for ref in token_refs],
            [ref.at[pl.ds(s, 1)] for ref in state_refs])


def _mlstm_state_specs(index_map3, index_map4, group=1):
    return [
        pl.BlockSpec((group, N_HEADS, HEAD_DIM, HEAD_DIM), index_map4),
        pl.BlockSpec((group, N_HEADS, HEAD_DIM), index_map3),
        pl.BlockSpec((group, SUBLANES, LANES), index_map3),
    ]


def _mlstm_state_shapes(batch):
    return [
        jax.ShapeDtypeStruct((batch, N_HEADS, HEAD_DIM, HEAD_DIM), F32),
        jax.ShapeDtypeStruct((batch, N_HEADS, HEAD_DIM), F32),
        jax.ShapeDtypeStruct((batch, SUBLANES, LANES), F32),
    ]


def _ret_phases(lg_ref, q_ref, k_ref, v_ref, g_ref, cos_ref, sin_ref, gain_ref, mix_ref, r_ref, L):
    half = HEAD_DIM // 2
    cos = cos_ref[...]
    sin = sin_ref[...]
    cos_k = cos * K_SCALE
    sin_k = sin * K_SCALE
    t_idx = lax.broadcasted_iota(jnp.int32, (L, L), 0)
    s_idx = lax.broadcasted_iota(jnp.int32, (L, L), 1)
    diff = t_idx - s_idx
    diff_pos = jnp.maximum(diff, 0).astype(F32)
    t_col = lax.broadcasted_iota(jnp.int32, (L, 1), 0).astype(F32)

    for h in range(N_HEADS):
        lo = slice(h * HEAD_DIM, h * HEAD_DIM + half)
        hi = slice(h * HEAD_DIM + half, (h + 1) * HEAD_DIM)
        sl = slice(h * HEAD_DIM, (h + 1) * HEAD_DIM)
        lg = lg_ref[h]
        q1 = q_ref[:, lo].astype(F32)
        q2 = q_ref[:, hi].astype(F32)
        k1 = k_ref[:, lo].astype(F32)
        k2 = k_ref[:, hi].astype(F32)
        qr = jnp.concatenate([q1 * cos - q2 * sin, q2 * cos + q1 * sin], axis=-1).astype(BF16)
        kr = jnp.concatenate([k1 * cos_k - k2 * sin_k, k2 * cos_k + k1 * sin_k], axis=-1)
        vh = v_ref[:, sl]

        dmask = jnp.where(diff >= 0, jnp.exp(lg * diff_pos), 0.0)
        inter = jnp.exp(lg * (t_col + 1.0))
        kdec = jnp.exp(lg * (L - 1.0 - t_col))
        cdec = jnp.exp(lg * jnp.full((1, 1), L, F32))

        s = lax.dot_general(qr, kr.astype(BF16), _NT, preferred_element_type=F32)
        yield
        r_h = r_ref[0, h]
        o = _dot((s * dmask).astype(BF16), vh) + inter * _dot(qr, r_h.astype(BF16))
        r_ref[0, h] = cdec * r_h + lax.dot_general(
            (kr * kdec).astype(BF16), vh, _TN, preferred_element_type=F32)

        oc = o - jnp.mean(o, axis=-1, keepdims=True)
        y = oc * _rms_scale(oc) * gain_ref[:, sl] * jax.nn.silu(g_ref[:, sl].astype(F32))
        mix_ref[:, sl] = y.astype(BF16)
        yield


def _sample_mixers_kernel(lg_ref, mq_ref, mk_ref, mv_ref, mo_ref, gt_ref, gain_m_ref,
                          c0_ref, n0_ref, m0_ref, rq_ref, rk_ref, rv_ref, rg_ref, cos_ref, sin_ref,
                          gain_r_ref, r0_ref, mixm_ref, c_ref, n_ref, m_ref, mixr_ref, r_ref,
                          *, chunk, group):
    c_ref[...] = c0_ref[...]
    n_ref[...] = n0_ref[...]
    m_ref[...] = m0_ref[...]
    r_ref[...] = r0_ref[...]
    phases = []
    for s in range(group):
        (q, k, v, o, mix), (c, n, m) = _seq_views(
            s, chunk, (mq_ref, mk_ref, mv_ref, mo_ref, mixm_ref), (c_ref, n_ref, m_ref))
        phases.append(_mlstm_phases(q, k, v, o, gt_ref[s], gain_m_ref, mix, c, n, m, chunk))
        (q, k, v, g, mix), (r,) = _seq_views(
            s, chunk, (rq_ref, rk_ref, rv_ref, rg_ref, mixr_ref), (r_ref,))
        phases.append(_ret_phases(lg_ref, q, k, v, g, cos_ref, sin_ref, gain_r_ref, mix, r, chunk))
    _run(*phases)


def _sample_mixers(proj, gt3, rope, init, w, *, batch, seq, group):
    cos, sin = rope
    tok = lambda col: pl.BlockSpec((group * seq, D_GROUP), lambda b, col=col: (b, col))
    const = lambda shape: pl.BlockSpec(shape, lambda b: (0,) * len(shape))
    state4 = pl.BlockSpec((group, N_HEADS, HEAD_DIM, HEAD_DIM), lambda b: (b, 0, 0, 0))
    state_specs = _mlstm_state_specs(lambda b: (b, 0, 0), lambda b: (b, 0, 0, 0), group)
    mix_spec = pl.BlockSpec((group * seq, D_GROUP), lambda b: (b, 0))
    mix_shape = jax.ShapeDtypeStruct((batch * seq, D_GROUP), BF16)
    c0, n0, m0, r0 = init
    return pl.pallas_call(
        functools.partial(_sample_mixers_kernel, chunk=seq, group=group),
        grid=(batch // group,),
        in_specs=([pl.BlockSpec(memory_space=pltpu.SMEM), tok(0), tok(1), tok(2), tok(3),
                   pl.BlockSpec((group, SUBLANES, gt3.shape[2]), lambda b: (b, 0, 0)),
                   const((1, D_GROUP))] + state_specs
                  + [tok(4), tok(5), tok(6), tok(7), const((seq, HEAD_DIM // 2)),
                     const((seq, HEAD_DIM // 2)), const((1, D_GROUP)), state4]),
        out_specs=[mix_spec] + state_specs + [mix_spec, state4],
        out_shape=([mix_shape] + _mlstm_state_shapes(batch)
                   + [mix_shape, jax.ShapeDtypeStruct((batch, N_HEADS, HEAD_DIM, HEAD_DIM), F32)]),
        compiler_params=_params(1),
        name="sample_mixers",
    )(w["log_gamma"], proj, proj, proj, proj, gt3, w["gain_m"], c0, n0, m0,
      proj, proj, proj, proj, cos, sin, w["gain_r"], r0)


def _prompt_kernel(lg_ref, x_ref, g1_ref, w_ref, wgate_ref, bgate_ref, gain_m_ref, cos_ref,
                   sin_ref, gain_r_ref, mixm_ref, c_ref, n_ref, m_ref, mixr_ref, r_ref,
                   xn_scr, proj_scr, gt_scr, *, n_chunks, chunks_per_seq, chunk):
    t = pl.program_id(0)
    piece = D_PROJ // (2 * N_HEADS)

    def inproj_pieces(slot):
        x = x_ref[...]
        xn_scr[...] = (x * _rms_scale(x) * g1_ref[...]).astype(BF16)
        gates = _dot(xn_scr[...], wgate_ref[...]) + bgate_ref[...]
        gt_scr[slot] = gates.T[0:N_GATES, :]
        for p in range(D_PROJ // piece):
            cols = pl.ds(p * piece, piece)
            proj_scr[slot, :, cols] = _dot(xn_scr[...], w_ref[:, cols]).astype(BF16)
            yield

    def mixers(slot):
        group = lambda k: proj_scr.at[slot, :, pl.ds(k * D_GROUP, D_GROUP)]
        return [
            _mlstm_phases(group(0), group(1), group(2), group(3), gt_scr[slot], gain_m_ref,
                          mixm_ref, c_ref, n_ref, m_ref, chunk),
            _ret_phases(lg_ref, group(4), group(5), group(6), group(7), cos_ref, sin_ref,
                        gain_r_ref, mixr_ref, r_ref, chunk),
        ]

    @pl.when((t >= 1) & ((t - 1) % chunks_per_seq == 0))
    def _():
        c_ref[...] = jnp.zeros_like(c_ref)
        n_ref[...] = jnp.zeros_like(n_ref)
        m_ref[...] = jnp.zeros_like(m_ref)
        r_ref[...] = jnp.zeros_like(r_ref)

    @pl.when(t == 0)
    def _():
        _run(inproj_pieces(0))

    @pl.when((t >= 1) & (t < n_chunks))
    def _():
        slot = t % 2
        pieces = inproj_pieces(slot)
        heads = mixers(1 - slot)
        for _ in range(N_HEADS):
            for phases in heads:
                next(phases)
                next(pieces)
                next(phases)

    @pl.when(t == n_chunks)
    def _():
        _run(*mixers((n_chunks - 1) % 2))


def _prompt_front(x2d, w, rope, *, batch, seq):
    chunk = PROMPT_CHUNK
    chunks_per_seq = seq // chunk
    n_chunks = batch * chunks_per_seq
    cos, sin = rope
    const = lambda shape, **kw: pl.BlockSpec(shape, lambda t: (0,) * len(shape), **kw)
    mixed = lambda t: jnp.maximum(t - 1, 0)
    rope_spec = pl.BlockSpec((chunk, HEAD_DIM // 2), lambda t: (mixed(t) % chunks_per_seq, 0))
    mix_spec = pl.BlockSpec((chunk, D_GROUP), lambda t: (mixed(t), 0))
    mix_shape = jax.ShapeDtypeStruct((batch * seq, D_GROUP), BF16)
    seq_of = lambda t: mixed(t) // chunks_per_seq
    state4 = pl.BlockSpec((1, N_HEADS, HEAD_DIM, HEAD_DIM), lambda t: (seq_of(t), 0, 0, 0))
    state_specs = _mlstm_state_specs(lambda t: (seq_of(t), 0, 0), lambda t: (seq_of(t), 0, 0, 0))
    return pl.pallas_call(
        functools.partial(_prompt_kernel, n_chunks=n_chunks, chunks_per_seq=chunks_per_seq,
                          chunk=chunk),
        grid=(n_chunks + 1,),
        in_specs=[
            pl.BlockSpec(memory_space=pltpu.SMEM),
            pl.BlockSpec((chunk, D_MODEL), lambda t: (jnp.minimum(t, n_chunks - 1), 0)),
            const((1, D_MODEL)),
            const((D_MODEL, D_PROJ), pipeline_mode=pl.Buffered(1)),
            const((D_MODEL, LANES), pipeline_mode=pl.Buffered(1)),
            const((1, LANES)),
            const((1, D_GROUP)),
            rope_spec, rope_spec,
            const((1, D_GROUP)),
        ],
        out_specs=[mix_spec] + state_specs + [mix_spec, state4],
        out_shape=([mix_shape] + _mlstm_state_shapes(batch)
                   + [mix_shape, jax.ShapeDtypeStruct((batch, N_HEADS, HEAD_DIM, HEAD_DIM), F32)]),
        scratch_shapes=[
            pltpu.VMEM((chunk, D_MODEL), BF16),
            pltpu.VMEM((2, chunk, D_PROJ), BF16),
            pltpu.VMEM((2, N_GATES, chunk), F32),
        ],
        compiler_params=_params(1),
        name="prompt_front",
    )(w["log_gamma"], x2d, w["g1"], w["w_in"], w["w_gatecols"], w["b_gate"], w["gain_m"],
      cos, sin, w["gain_r"])


def _outproj_kernel(x_ref, mm_ref, mr_ref, wm_ref, wr_ref, g2_ref, x1_ref, hn_ref):
    x1 = x_ref[...] + _dot(mm_ref[...], wm_ref[...]) + _dot(mr_ref[...], wr_ref[...])
    x1_ref[...] = x1
    hn_ref[...] = (x1 * _rms_scale(x1) * g2_ref[...]).astype(BF16)


def _outproj(x, mix_m, mix_r, w_out, g2, *, tm):
    n = x.shape[0]
    w_spec = lambda half: pl.BlockSpec((D_GROUP, D_MODEL), lambda i, half=half: (half, 0),
                                       pipeline_mode=pl.Buffered(1))
    return pl.pallas_call(
        _outproj_kernel,
        grid=(n // tm,),
        in_specs=[
            pl.BlockSpec((tm, D_MODEL), lambda i: (i, 0)),
            pl.BlockSpec((tm, D_GROUP), lambda i: (i, 0)),
            pl.BlockSpec((tm, D_GROUP), lambda i: (i, 0)),
            w_spec(0), w_spec(1),
            pl.BlockSpec((1, D_MODEL), lambda i: (0, 0)),
        ],
        out_specs=[
            pl.BlockSpec((tm, D_MODEL), lambda i: (i, 0)),
            pl.BlockSpec((tm, D_MODEL), lambda i: (i, 0)),
        ],
        out_shape=[
            jax.ShapeDtypeStruct((n, D_MODEL), F32),
            jax.ShapeDtypeStruct((n, D_MODEL), BF16),
        ],
        compiler_params=_params(1),
        name="outproj",
    )(x, mix_m, mix_r, w_out, w_out, g2)


def _swiglu_down(hn, w_gu, w_down):
    gu = _dot(hn, w_gu)
    acts = []
    for base in range(0, w_gu.shape[1], 2 * FFN_GRANULE):
        gate = gu[:, base:base + FFN_GRANULE]
        up = gu[:, base + FFN_GRANULE:base + 2 * FFN_GRANULE]
        acts.append((jax.nn.silu(gate) * up).astype(BF16))
    act = acts[0] if len(acts) == 1 else jnp.concatenate(acts, axis=-1)
    return _dot(act, w_down)


PACK_IN_SLOTS = 3
PACK_OUT_SLOTS = 2


def _ffn_pack_kernel(hn_ref, x1_ref, wg_hbm, wu_hbm, wd_hbm, gf_ref, y_ref, wgu_hbm, wd16_hbm,
                     g32_buf, u32_buf, d32_buf, gu16_buf, d16_buf, in_sems, out_sems, *, n_f, tf):
    def in_copies(f, slot):
        col, row = f * tf, f * tf
        if not isinstance(f, int):
            col, row = pl.multiple_of(col, tf), pl.multiple_of(row, tf)
        return (
            pltpu.make_async_copy(wg_hbm.at[:, pl.ds(col, tf)], g32_buf.at[slot], in_sems.at[0, slot]),
            pltpu.make_async_copy(wu_hbm.at[:, pl.ds(col, tf)], u32_buf.at[slot], in_sems.at[1, slot]),
            pltpu.make_async_copy(wd_hbm.at[pl.ds(row, tf), :], d32_buf.at[slot], in_sems.at[2, slot]),
        )

    def out_copies(f, slot):
        col, row = f * (2 * tf), f * tf
        if not isinstance(f, int):
            col, row = pl.multiple_of(col, 2 * tf), pl.multiple_of(row, tf)
        return (
            pltpu.make_async_copy(gu16_buf.at[slot], wgu_hbm.at[:, pl.ds(col, 2 * tf)],
                                  out_sems.at[0, slot]),
            pltpu.make_async_copy(d16_buf.at[slot], wd16_hbm.at[pl.ds(row, tf), :],
                                  out_sems.at[1, slot]),
        )

    def iteration(f, first=False, final=False, reuse_out=True, prefetch=True):
        islot = f % PACK_IN_SLOTS
        oslot = f % PACK_OUT_SLOTS
        for copy in in_copies(f, islot):
            copy.wait()
        if prefetch:
            for copy in in_copies(f + 2, (f + 2) % PACK_IN_SLOTS):
                copy.start()
        if reuse_out:
            for copy in out_copies(f - PACK_OUT_SLOTS, oslot):
                copy.wait()
        gu16_buf[oslot, :, :tf] = g32_buf[islot].astype(BF16)
        gu16_buf[oslot, :, tf:] = u32_buf[islot].astype(BF16)
        d16_buf[oslot] = d32_buf[islot].astype(BF16)
        for copy in out_copies(f, oslot):
            copy.start()
        part = _swiglu_down(hn_ref[...], gu16_buf[oslot], d16_buf[oslot])
        if first:
            y_ref[...] = x1_ref[...] + part
        else:
            y_ref[...] += part
        if final:
            x2 = y_ref[...]
            y_ref[...] = x2 * _rms_scale(x2) * gf_ref[...]

    for f in range(2):
        for copy in in_copies(f, f):
            copy.start()
    iteration(0, first=True, reuse_out=False)
    iteration(1, reuse_out=False)

    def middle(f, carry):
        iteration(f)
        return carry

    lax.fori_loop(2, n_f - 2, middle, 0)
    iteration(n_f - 2, prefetch=False)
    iteration(n_f - 1, final=True, prefetch=False)
    for f in (n_f - 2, n_f - 1):
        for copy in out_copies(f, f % PACK_OUT_SLOTS):
            copy.wait()


def _ffn_pack(hn, x1, w):
    n = hn.shape[0]
    tf = FFN_GRANULE
    n_f = D_FF // tf
    assert n_f >= 5
    rows_spec = lambda: pl.BlockSpec((n, D_MODEL), lambda i: (0, 0), pipeline_mode=pl.Buffered(1))
    hbm_spec = pl.BlockSpec(memory_space=pl.ANY)
    return pl.pallas_call(
        functools.partial(_ffn_pack_kernel, n_f=n_f, tf=tf),
        grid=(1,),
        in_specs=[rows_spec(), rows_spec(), hbm_spec, hbm_spec, hbm_spec,
                  pl.BlockSpec((1, D_MODEL), lambda i: (0, 0))],
        out_specs=[pl.BlockSpec((n, D_MODEL), lambda i: (0, 0)), hbm_spec, hbm_spec],
        out_shape=[jax.ShapeDtypeStruct((n, D_MODEL), F32),
                   jax.ShapeDtypeStruct((D_MODEL, 2 * D_FF), BF16),
                   jax.ShapeDtypeStruct((D_FF, D_MODEL), BF16)],
        scratch_shapes=[
            pltpu.VMEM((PACK_IN_SLOTS, D_MODEL, tf), F32),
            pltpu.VMEM((PACK_IN_SLOTS, D_MODEL, tf), F32),
            pltpu.VMEM((PACK_IN_SLOTS, tf, D_MODEL), F32),
            pltpu.VMEM((PACK_OUT_SLOTS, D_MODEL, 2 * tf), BF16),
            pltpu.VMEM((PACK_OUT_SLOTS, tf, D_MODEL), BF16),
            pltpu.SemaphoreType.DMA((3, PACK_IN_SLOTS)),
            pltpu.SemaphoreType.DMA((2, PACK_OUT_SLOTS)),
        ],
        compiler_params=_params(1),
        name="ffn_pack",
    )(hn, x1, w["w_gate"], w["w_up"], w["w_down"], w["g_final"])


def _tail_stream_kernel(x_hbm, mm_hbm, mr_hbm, wom_ref, wor_ref, g2_ref, wgu_hbm, wd_hbm, gf_ref,
                        y_ref, x1_buf, mm_buf, mr_buf, hn_buf, wgu_buf, wd_buf, row_sems, w_sems,
                        *, n_f, tf):
    i = pl.program_id(0)
    n_i = pl.num_programs(0)
    tm = x1_buf.shape[0]

    def row_copies(tile):
        rows = pl.ds(tile * tm, tm)
        return (
            pltpu.make_async_copy(x_hbm.at[rows, :], x1_buf, row_sems.at[0]),
            pltpu.make_async_copy(mm_hbm.at[rows, :], mm_buf, row_sems.at[1]),
            pltpu.make_async_copy(mr_hbm.at[rows, :], mr_buf, row_sems.at[2]),
        )

    def w_copies(f, slot):
        col, row = f * (2 * tf), f * tf
        if not isinstance(f, int):
            col, row = pl.multiple_of(col, 2 * tf), pl.multiple_of(row, tf)
        return (
            pltpu.make_async_copy(wgu_hbm.at[:, pl.ds(col, 2 * tf)], wgu_buf.at[slot],
                                  w_sems.at[0, slot]),
            pltpu.make_async_copy(wd_hbm.at[pl.ds(row, tf), :], wd_buf.at[slot],
                                  w_sems.at[1, slot]),
        )

    @pl.when(i == 0)
    def _():
        for copy in row_copies(0) + w_copies(0, 0):
            copy.start()

    def iteration(f, first, final):
        slot = (i * n_f + f) % 2
        for copy in w_copies(f, slot):
            copy.wait()
        if final:
            @pl.when(i + 1 < n_i)
            def _():
                for copy in w_copies(0, 1 - slot):
                    copy.start()
        else:
            for copy in w_copies(f + 1, 1 - slot):
                copy.start()
        if first:
            for copy in row_copies(i):
                copy.wait()
            for r in range(0, tm, TAIL_ROWS):
                rows = pl.ds(r, TAIL_ROWS)
                x1 = (x1_buf[rows, :] + _dot(mm_buf[rows, :], wom_ref[...])
                      + _dot(mr_buf[rows, :], wor_ref[...]))
                x1_buf[rows, :] = x1
                hn_buf[rows, :] = (x1 * _rms_scale(x1) * g2_ref[...]).astype(BF16)
        part = _swiglu_down(hn_buf[...], wgu_buf[slot], wd_buf[slot])
        if first:
            y_ref[...] = x1_buf[...] + part
        else:
            y_ref[...] += part
        if final:
            x2 = y_ref[...]
            y_ref[...] = x2 * _rms_scale(x2) * gf_ref[...]

    def middle(f, carry):
        @pl.when((f == 1) & (i + 1 < n_i))
        def _():
            for copy in row_copies(i + 1):
                copy.start()

        iteration(f, False, False)
        return carry

    iteration(0, True, False)
    lax.fori_loop(1, n_f - 1, middle, 0)
    iteration(n_f - 1, False, True)


def _tail_stream(x, mix_m, mix_r, w, *, tm, tf):
    n = x.shape[0]
    n_f = D_FF // tf
    assert n_f >= 3
    w_out_spec = lambda half: pl.BlockSpec((D_GROUP, D_MODEL), lambda i, half=half: (half, 0),
                                           pipeline_mode=pl.Buffered(1))
    return pl.pallas_call(
        functools.partial(_tail_stream_kernel, n_f=n_f, tf=tf),
        grid=(n // tm,),
        in_specs=[
            pl.BlockSpec(memory_space=pl.ANY),
            pl.BlockSpec(memory_space=pl.ANY),
            pl.BlockSpec(memory_space=pl.ANY),
            w_out_spec(0), w_out_spec(1),
            pl.BlockSpec((1, D_MODEL), lambda i: (0, 0)),
            pl.BlockSpec(memory_space=pl.ANY),
            pl.BlockSpec(memory_space=pl.ANY),
            pl.BlockSpec((1, D_MODEL), lambda i: (0, 0)),
        ],
        out_specs=pl.BlockSpec((tm, D_MODEL), lambda i: (i, 0)),
        out_shape=jax.ShapeDtypeStruct((n, D_MODEL), F32),
        scratch_shapes=[
            pltpu.VMEM((tm, D_MODEL), F32),
            pltpu.VMEM((tm, D_GROUP), BF16),
            pltpu.VMEM((tm, D_GROUP), BF16),
            pltpu.VMEM((tm, D_MODEL), BF16),
            pltpu.VMEM((2, D_MODEL, 2 * tf), BF16),
            pltpu.VMEM((2, tf, D_MODEL), BF16),
            pltpu.SemaphoreType.DMA((3,)),
            pltpu.SemaphoreType.DMA((2, 2)),
        ],
        compiler_params=_params(1, TAIL_VMEM_LIMIT),
        name="tail_stream",
    )(x, mix_m, mix_r, w["w_out"], w["w_out"], w["g2"], w["w_gu"], w["w_down"], w["g_final"])


def _rope_tables(pos):
    half = HEAD_DIM // 2
    freqs = ROPE_BASE ** (-jnp.arange(half, dtype=F32) / half)
    ang = pos[:, None] * freqs[None, :]
    return jnp.cos(ang), jnp.sin(ang)


def _prompt_trunk(x, w):
    batch, seq, _ = x.shape
    x2d = x.reshape(batch * seq, D_MODEL)
    rope = _rope_tables(jnp.arange(seq, dtype=F32))
    mix_m, c_new, n_new, m_new, mix_r, r_new = _prompt_front(x2d, w, rope, batch=batch, seq=seq)
    y = _tail_stream(x2d, mix_m, mix_r, w, tm=FFN_TM, tf=FFN_TF)
    return (y.reshape(batch, seq, D_MODEL), c_new[None], n_new[None],
            m_new[:, :N_HEADS, 0][None], r_new[None])


def _sample_trunk(x, init, w):
    batch, seq, _ = x.shape
    x2d = x.reshape(batch * seq, D_MODEL)
    proj, gt = _inproj(x2d, w["g1"], w["w_in"], w["w_gatecols"], w["b_gate"],
                       tm=INPROJ_TM, tn=INPROJ_TN)
    gt3 = gt.reshape(N_GATES, batch, seq).transpose(1, 0, 2)
    gt3 = jnp.pad(gt3, ((0, 0), (0, 0), (0, max(0, LANES - seq))))
    c0, n0, m0, r0 = init
    m0 = jnp.pad(m0, ((0, 0), (0, SUBLANES - N_HEADS)))
    m0 = jnp.broadcast_to(m0[:, :, None], (batch, SUBLANES, LANES))
    rope = _rope_tables(PAST_LEN + jnp.arange(seq, dtype=F32))
    mix_m, c_new, n_new, m_new, mix_r, r_new = _sample_mixers(
        proj, gt3, rope, (c0, n0, m0, r0), w, batch=batch, seq=seq, group=SAMPLE_GROUP)
    x1, hn = _outproj(x2d, mix_m, mix_r, w["w_out"], w["g2"], tm=OUTPROJ_TM)
    y, w_gu16, w_down16 = _ffn_pack(hn, x1, w)
    w16 = dict(w_gu=w_gu16, w_down=w_down16)
    return (y.reshape(batch, seq, D_MODEL), c_new[None], n_new[None],
            m_new[:, :N_HEADS, 0][None], r_new[None]), w16


def kernel(x_prompt, x_sample, state_mlstm_C, state_mlstm_n, state_mlstm_m, state_ret,
           g_norm1, w_in, b_gates, g_mlstm_norm, g_ret_norm, w_out, g_norm2,
           w_gate, w_up, w_down, g_final):
    w_in0 = w_in[0].astype(BF16)
    w = dict(
        g1=g_norm1[0].reshape(1, D_MODEL),
        w_in=w_in0,
        w_gatecols=jnp.pad(w_in0[:, D_PROJ:], ((0, 0), (0, LANES - N_GATES))),
        b_gate=jnp.pad(b_gates[0], (0, LANES - N_GATES)).reshape(1, LANES),
        gain_m=g_mlstm_norm[0].reshape(1, D_GROUP),
        gain_r=g_ret_norm[0].reshape(1, D_GROUP),
        w_out=w_out[0].astype(BF16),
        g2=g_norm2[0].reshape(1, D_MODEL),
        w_gate=w_gate[0],
        w_up=w_up[0],
        w_down=w_down[0],
        g_final=g_final.reshape(1, D_MODEL),
        log_gamma=jnp.log(1.0 - jnp.exp2(-5.0 - jnp.arange(N_HEADS, dtype=F32))),
    )
    init_s = (state_mlstm_C[0], state_mlstm_n[0], state_mlstm_m[0], state_ret[0])
    (y_s, c_s, n_s, m_s, r_s), w16 = _sample_trunk(x_sample, init_s, w)
    y_p, c_p, n_p, m_p, r_p = _prompt_trunk(x_prompt, {**w, **w16})
    return (y_p, y_s, c_p, n_p, m_p, r_p, c_s, n_s, m_s, r_s)
```

```python
import functools

import jax
import jax.numpy as jnp
from jax import lax
from jax.experimental import pallas as pl
from jax.experimental.pallas import tpu as pltpu

F32 = jnp.float32
BF16 = jnp.bfloat16

D_MODEL = 2048
N_HEADS = 4
HEAD_DIM = 256
D_GROUP = N_HEADS * HEAD_DIM
D_PROJ = 8 * D_GROUP
N_GATES = 2 * N_HEADS
D_FF = 5632
ROPE_BASE = 10000.0
PAST_LEN = 4096
EPS = 1e-6
K_SCALE = HEAD_DIM ** -0.5

LANES = 128
SUBLANES = 8
VMEM_LIMIT = 60 * 1024 * 1024
TAIL_VMEM_LIMIT = 62 * 1024 * 1024

PROMPT_CHUNK = 256
INPROJ_TM = 1024
INPROJ_TN = 2048
SAMPLE_GROUP = 4
OUTPROJ_TM = 512
FFN_TM = 1024
TAIL_ROWS = 256
FFN_TF = 512
FFN_GRANULE = 256

_NT = (((1,), (1,)), ((), ()))
_TN = (((0,), (0,)), ((), ()))


def _dot(a, b):
    return jnp.dot(a, b, preferred_element_type=F32)


def _rms_scale(x):
    return lax.rsqrt(jnp.mean(x * x, axis=-1, keepdims=True) + EPS)


def _params(n_axes, vmem_limit=VMEM_LIMIT):
    return pltpu.CompilerParams(
        dimension_semantics=("arbitrary",) * n_axes, vmem_limit_bytes=vmem_limit)


def _inproj_prologue(x_ref, g1_ref, wgate_ref, bgate_ref, gt_ref, xn_ref):
    x = x_ref[...]
    xn = (x * _rms_scale(x) * g1_ref[...]).astype(BF16)
    xn_ref[...] = xn
    gates = _dot(xn, wgate_ref[...]) + bgate_ref[...]
    gt_ref[...] = gates.T[0:N_GATES, :]


def _inproj_kernel(x_ref, g1_ref, w_ref, wgate_ref, bgate_ref, proj_ref, gt_ref, xn_ref):
    @pl.when(pl.program_id(1) == 0)
    def _():
        _inproj_prologue(x_ref, g1_ref, wgate_ref, bgate_ref, gt_ref, xn_ref)

    proj_ref[...] = _dot(xn_ref[...], w_ref[...]).astype(BF16)


def _inproj(x, g1, w_main, w_gate, b_gate, *, tm, tn):
    n = x.shape[0]
    return pl.pallas_call(
        _inproj_kernel,
        grid=(n // tm, D_PROJ // tn),
        in_specs=[
            pl.BlockSpec((tm, D_MODEL), lambda i, j: (i, 0)),
            pl.BlockSpec((1, D_MODEL), lambda i, j: (0, 0)),
            pl.BlockSpec((D_MODEL, tn), lambda i, j: (0, j)),
            pl.BlockSpec((D_MODEL, LANES), lambda i, j: (0, 0)),
            pl.BlockSpec((1, LANES), lambda i, j: (0, 0)),
        ],
        out_specs=[
            pl.BlockSpec((tm, tn), lambda i, j: (i, j)),
            pl.BlockSpec((N_GATES, tm), lambda i, j: (0, i)),
        ],
        out_shape=[
            jax.ShapeDtypeStruct((n, D_PROJ), BF16),
            jax.ShapeDtypeStruct((N_GATES, n), F32),
        ],
        scratch_shapes=[pltpu.VMEM((tm, D_MODEL), BF16)],
        compiler_params=_params(2),
        name="inproj",
    )(x, g1, w_main, w_gate, b_gate)


def _lane_scan(x, combine, identity):
    width = x.shape[1]
    lane = lax.broadcasted_iota(jnp.int32, x.shape, 1)
    shift = 1
    while shift < width:
        shifted = jnp.where(lane >= shift, pltpu.roll(x, shift, axis=1), identity)
        x = combine(x, shifted)
        shift *= 2
    return x


def _mlstm_phases(q_ref, k_ref, v_ref, o_ref, gates, gain_ref, mix_ref, c_ref, n_ref, m_ref, L):
    b = _lane_scan(jax.nn.log_sigmoid(gates), jnp.add, 0.0)
    b = pltpu.roll(b, N_HEADS, axis=0)
    u = gates - b
    m_prev = m_ref[0][:, 0:1]
    big_m = jnp.maximum(m_prev, _lane_scan(u, jnp.maximum, -jnp.inf))
    a = jnp.exp(m_prev - big_m)
    e = jnp.exp(-(b + big_m))
    m_last = big_m[:, L - 1:L]
    g = jnp.exp(u - m_last)
    dec = a[:, L - 1:L]
    m_new = b[:, L - 1:L] + m_last
    rows = jnp.concatenate(
        [big_m, a, e, g, jnp.zeros((LANES - 4 * SUBLANES, gates.shape[1]), F32)], axis=0)
    cols = rows.T

    t_idx = lax.broadcasted_iota(jnp.int32, (L, L), 0)
    s_idx = lax.broadcasted_iota(jnp.int32, (L, L), 1)
    causal = s_idx <= t_idx

    for h in range(N_HEADS):
        sl = slice(h * HEAD_DIM, (h + 1) * HEAD_DIM)
        qh = q_ref[:, sl]
        kh = k_ref[:, sl] * jnp.asarray(K_SCALE, BF16)
        vh = v_ref[:, sl]
        m_col = cols[:L, h:h + 1]
        a_col = cols[:L, SUBLANES + h:SUBLANES + h + 1]
        e_col = cols[:L, 2 * SUBLANES + h:2 * SUBLANES + h + 1]
        g_col = cols[:L, 3 * SUBLANES + h:3 * SUBLANES + h + 1]
        u_row = u[h:h + 1, :L]

        s = lax.dot_general(qh, kh, _NT, preferred_element_type=F32)
        yield
        sw = s * jnp.exp(jnp.where(causal, u_row - m_col, -jnp.inf))
        c_h = c_ref[0, h]
        n_h = n_ref[0, h:h + 1, :]
        num = a_col * _dot(qh, c_h.astype(BF16)) + _dot(sw.astype(BF16), vh)
        qn = jnp.sum(qh.astype(F32) * n_h, axis=-1, keepdims=True)
        den = a_col * qn + jnp.sum(sw, axis=-1, keepdims=True)
        hh = num * (1.0 / jnp.maximum(jnp.abs(den), e_col))
        y = hh * _rms_scale(hh) * gain_ref[:, sl] * jax.nn.sigmoid(o_ref[:, sl].astype(F32))
        mix_ref[:, sl] = y.astype(BF16)

        kg = kh.astype(F32) * g_col
        dec_h = dec[h:h + 1, :]
        c_ref[0, h] = dec_h * c_h + lax.dot_general(
            kg.astype(BF16), vh, _TN, preferred_element_type=F32)
        n_ref[0, h:h + 1, :] = dec_h * n_h + jnp.sum(kg, axis=0, keepdims=True)
        if h == N_HEADS - 1:
            m_ref[0] = jnp.broadcast_to(m_new, (SUBLANES, LANES))
        yield


def _run(*phase_generators):
    live = list(phase_generators)
    while live:
        live = [g for g in live if next(g, StopIteration) is not StopIteration]


def _seq_views(s, rows, token_refs, state_refs):
    return ([ref.at[pl.ds(s * rows, rows), :] for ref in token_refs],
            [ref.at[pl.ds(s, 1)] for ref in state_refs])


def _mlstm_state_specs(index_map3, index_map4, group=1):
    return [
        pl.BlockSpec((group, N_HEADS, HEAD_DIM, HEAD_DIM), index_map4),
        pl.BlockSpec((group, N_HEADS, HEAD_DIM), index_map3),
        pl.BlockSpec((group, SUBLANES, LANES), index_map3),
    ]


def _mlstm_state_shapes(batch):
    return [
        jax.ShapeDtypeStruct((batch, N_HEADS, HEAD_DIM, HEAD_DIM), F32),
        jax.ShapeDtypeStruct((batch, N_HEADS, HEAD_DIM), F32),
        jax.ShapeDtypeStruct((batch, SUBLANES, LANES), F32),
    ]


def _ret_phases(lg_ref, q_ref, k_ref, v_ref, g_ref, cos_ref, sin_ref, gain_ref, mix_ref, r_ref, L):
    half = HEAD_DIM // 2
    cos = cos_ref[...]
    sin = sin_ref[...]
    cos_k = cos * K_SCALE
    sin_k = sin * K_SCALE
    t_idx = lax.broadcasted_iota(jnp.int32, (L, L), 0)
    s_idx = lax.broadcasted_iota(jnp.int32, (L, L), 1)
    diff = t_idx - s_idx
    diff_pos = jnp.maximum(diff, 0).astype(F32)
    t_col = lax.broadcasted_iota(jnp.int32, (L, 1), 0).astype(F32)

    for h in range(N_HEADS):
        lo = slice(h * HEAD_DIM, h * HEAD_DIM + half)
        hi = slice(h * HEAD_DIM + half, (h + 1) * HEAD_DIM)
        sl = slice(h * HEAD_DIM, (h + 1) * HEAD_DIM)
        lg = lg_ref[h]
        q1 = q_ref[:, lo].astype(F32)
        q2 = q_ref[:, hi].astype(F32)
        k1 = k_ref[:, lo].astype(F32)
        k2 = k_ref[:, hi].astype(F32)
        qr = jnp.concatenate([q1 * cos - q2 * sin, q2 * cos + q1 * sin], axis=-1).astype(BF16)
        kr = jnp.concatenate([k1 * cos_k - k2 * sin_k, k2 * cos_k + k1 * sin_k], axis=-1)
        vh = v_ref[:, sl]

        dmask = jnp.where(diff >= 0, jnp.exp(lg * diff_pos), 0.0)
        inter = jnp.exp(lg * (t_col + 1.0))
        kdec = jnp.exp(lg * (L - 1.0 - t_col))
        cdec = jnp.exp(lg * jnp.full((1, 1), L, F32))

        s = lax.dot_general(qr, kr.astype(BF16), _NT, preferred_element_type=F32)
        yield
        r_h = r_ref[0, h]
        o = _dot((s * dmask).astype(BF16), vh) + inter * _dot(qr, r_h.astype(BF16))
        r_ref[0, h] = cdec * r_h + lax.dot_general(
            (kr * kdec).astype(BF16), vh, _TN, preferred_element_type=F32)

        oc = o - jnp.mean(o, axis=-1, keepdims=True)
        y = oc * _rms_scale(oc) * gain_ref[:, sl] * jax.nn.silu(g_ref[:, sl].astype(F32))
        mix_ref[:, sl] = y.astype(BF16)
        yield


def _sample_mixers_kernel(lg_ref, mq_ref, mk_ref, mv_ref, mo_ref, gt_ref, gain_m_ref,
                          c0_ref, n0_ref, m0_ref, rq_ref, rk_ref, rv_ref, rg_ref, cos_ref, sin_ref,
                          gain_r_ref, r0_ref, mixm_ref, c_ref, n_ref, m_ref, mixr_ref, r_ref,
                          *, chunk, group):
    c_ref[...] = c0_ref[...]
    n_ref[...] = n0_ref[...]
    m_ref[...] = m0_ref[...]
    r_ref[...] = r0_ref[...]
    phases = []
    for s in range(group):
        (q, k, v, o, mix), (c, n, m) = _seq_views(
            s, chunk, (mq_ref, mk_ref, mv_ref, mo_ref, mixm_ref), (c_ref, n_ref, m_ref))
        phases.append(_mlstm_phases(q, k, v, o, gt_ref[s], gain_m_ref, mix, c, n, m, chunk))
        (q, k, v, g, mix), (r,) = _seq_views(
            s, chunk, (rq_ref, rk_ref, rv_ref, rg_ref, mixr_ref), (r_ref,))
        phases.append(_ret_phases(lg_ref, q, k, v, g, cos_ref, sin_ref, gain_r_ref, mix, r, chunk))
    _run(*phases)


def _sample_mixers(proj, gt3, rope, init, w, *, batch, seq, group):
    cos, sin = rope
    tok = lambda col: pl.BlockSpec((group * seq, D_GROUP), lambda b, col=col: (b, col))
    const = lambda shape: pl.BlockSpec(shape, lambda b: (0,) * len(shape))
    state4 = pl.BlockSpec((group, N_HEADS, HEAD_DIM, HEAD_DIM), lambda b: (b, 0, 0, 0))
    state_specs = _mlstm_state_specs(lambda b: (b, 0, 0), lambda b: (b, 0, 0, 0), group)
    mix_spec = pl.BlockSpec((group * seq, D_GROUP), lambda b: (b, 0))
    mix_shape = jax.ShapeDtypeStruct((batch * seq, D_GROUP), BF16)
    c0, n0, m0, r0 = init
    return pl.pallas_call(
        functools.partial(_sample_mixers_kernel, chunk=seq, group=group),
        grid=(batch // group,),
        in_specs=([pl.BlockSpec(memory_space=pltpu.SMEM), tok(0), tok(1), tok(2), tok(3),
                   pl.BlockSpec((group, SUBLANES, gt3.shape[2]), lambda b: (b, 0, 0)),
                   const((1, D_GROUP))] + state_specs
                  + [tok(4), tok(5), tok(6), tok(7), const((seq, HEAD_DIM // 2)),
                     const((seq, HEAD_DIM // 2)), const((1, D_GROUP)), state4]),
        out_specs=[mix_spec] + state_specs + [mix_spec, state4],
        out_shape=([mix_shape] + _mlstm_state_shapes(batch)
                   + [mix_shape, jax.ShapeDtypeStruct((batch, N_HEADS, HEAD_DIM, HEAD_DIM), F32)]),
        compiler_params=_params(1),
        name="sample_mixers",
    )(w["log_gamma"], proj, proj, proj, proj, gt3, w["gain_m"], c0, n0, m0,
      proj, proj, proj, proj, cos, sin, w["gain_r"], r0)


def _prompt_kernel(lg_ref, x_ref, g1_ref, w_ref, wgate_ref, bgate_ref, gain_m_ref, cos_ref,
                   sin_ref, gain_r_ref, mixm_ref, c_ref, n_ref, m_ref, mixr_ref, r_ref,
                   xn_scr, proj_scr, gt_scr, *, n_chunks, chunks_per_seq, chunk):
    t = pl.program_id(0)
    piece = D_PROJ // (2 * N_HEADS)

    def inproj_pieces(slot):
        x = x_ref[...]
        xn_scr[...] = (x * _rms_scale(x) * g1_ref[...]).astype(BF16)
        gates = _dot(xn_scr[...], wgate_ref[...]) + bgate_ref[...]
        gt_scr[slot] = gates.T[0:N_GATES, :]
        for p in range(D_PROJ // piece):
            cols = pl.ds(p * piece, piece)
            proj_scr[slot, :, cols] = _dot(xn_scr[...], w_ref[:, cols]).astype(BF16)
            yield

    def mixers(slot):
        group = lambda k: proj_scr.at[slot, :, pl.ds(k * D_GROUP, D_GROUP)]
        return [
            _mlstm_phases(group(0), group(1), group(2), group(3), gt_scr[slot], gain_m_ref,
                          mixm_ref, c_ref, n_ref, m_ref, chunk),
            _ret_phases(lg_ref, group(4), group(5), group(6), group(7), cos_ref, sin_ref,
                        gain_r_ref, mixr_ref, r_ref, chunk),
        ]

    @pl.when((t >= 1) & ((t - 1) % chunks_per_seq == 0))
    def _():
        c_ref[...] = jnp.zeros_like(c_ref)
        n_ref[...] = jnp.zeros_like(n_ref)
        m_ref[...] = jnp.zeros_like(m_ref)
        r_ref[...] = jnp.zeros_like(r_ref)

    @pl.when(t == 0)
    def _():
        _run(inproj_pieces(0))

    @pl.when((t >= 1) & (t < n_chunks))
    def _():
        slot = t % 2
        pieces = inproj_pieces(slot)
        heads = mixers(1 - slot)
        for _ in range(N_HEADS):
            for phases in heads:
                next(phases)
                next(pieces)
                next(phases)

    @pl.when(t == n_chunks)
    def _():
        _run(*mixers((n_chunks - 1) % 2))


def _prompt_front(x2d, w, rope, *, batch, seq):
    chunk = PROMPT_CHUNK
    chunks_per_seq = seq // chunk
    n_chunks = batch * chunks_per_seq
    cos, sin = rope
    const = lambda shape, **kw: pl.BlockSpec(shape, lambda t: (0,) * len(shape), **kw)
    mixed = lambda t: jnp.maximum(t - 1, 0)
    rope_spec = pl.BlockSpec((chunk, HEAD_DIM // 2), lambda t: (mixed(t) % chunks_per_seq, 0))
    mix_spec = pl.BlockSpec((chunk, D_GROUP), lambda t: (mixed(t), 0))
    mix_shape = jax.ShapeDtypeStruct((batch * seq, D_GROUP), BF16)
    seq_of = lambda t: mixed(t) // chunks_per_seq
    state4 = pl.BlockSpec((1, N_HEADS, HEAD_DIM, HEAD_DIM), lambda t: (seq_of(t), 0, 0, 0))
    state_specs = _mlstm_state_specs(lambda t: (seq_of(t), 0, 0), lambda t: (seq_of(t), 0, 0, 0))
    return pl.pallas_call(
        functools.partial(_prompt_kernel, n_chunks=n_chunks, chunks_per_seq=chunks_per_seq,
                          chunk=chunk),
        grid=(n_chunks + 1,),
        in_specs=[
            pl.BlockSpec(memory_space=pltpu.SMEM),
            pl.BlockSpec((chunk, D_MODEL), lambda t: (jnp.minimum(t, n_chunks - 1), 0)),
            const((1, D_MODEL)),
            const((D_MODEL, D_PROJ), pipeline_mode=pl.Buffered(1)),
            const((D_MODEL, LANES), pipeline_mode=pl.Buffered(1)),
            const((1, LANES)),
            const((1, D_GROUP)),
            rope_spec, rope_spec,
            const((1, D_GROUP)),
        ],
        out_specs=[mix_spec] + state_specs + [mix_spec, state4],
        out_shape=([mix_shape] + _mlstm_state_shapes(batch)
                   + [mix_shape, jax.ShapeDtypeStruct((batch, N_HEADS, HEAD_DIM, HEAD_DIM), F32)]),
        scratch_shapes=[
            pltpu.VMEM((chunk, D_MODEL), BF16),
            pltpu.VMEM((2, chunk, D_PROJ), BF16),
            pltpu.VMEM((2, N_GATES, chunk), F32),
        ],
        compiler_params=_params(1),
        name="prompt_front",
    )(w["log_gamma"], x2d, w["g1"], w["w_in"], w["w_gatecols"], w["b_gate"], w["gain_m"],
      cos, sin, w["gain_r"])


def _outproj_kernel(x_ref, mm_ref, mr_ref, wm_ref, wr_ref, g2_ref, x1_ref, hn_ref):
    x1 = x_ref[...] + _dot(mm_ref[...], wm_ref[...]) + _dot(mr_ref[...], wr_ref[...])
    x1_ref[...] = x1
    hn_ref[...] = (x1 * _rms_scale(x1) * g2_ref[...]).astype(BF16)


def _outproj(x, mix_m, mix_r, w_out, g2, *, tm):
    n = x.shape[0]
    w_spec = lambda half: pl.BlockSpec((D_GROUP, D_MODEL), lambda i, half=half: (half, 0),
                                       pipeline_mode=pl.Buffered(1))
    return pl.pallas_call(
        _outproj_kernel,
        grid=(n // tm,),
        in_specs=[
            pl.BlockSpec((tm, D_MODEL), lambda i: (i, 0)),
            pl.BlockSpec((tm, D_GROUP), lambda i: (i, 0)),
            pl.BlockSpec((tm, D_GROUP), lambda i: (i, 0)),
            w_spec(0), w_spec(1),
            pl.BlockSpec((1, D_MODEL), lambda i: (0, 0)),
        ],
        out_specs=[
            pl.BlockSpec((tm, D_MODEL), lambda i: (i, 0)),
            pl.BlockSpec((tm, D_MODEL), lambda i: (i, 0)),
        ],
        out_shape=[
            jax.ShapeDtypeStruct((n, D_MODEL), F32),
            jax.ShapeDtypeStruct((n, D_MODEL), BF16),
        ],
        compiler_params=_params(1),
        name="outproj",
    )(x, mix_m, mix_r, w_out, w_out, g2)


def _swiglu_down(hn, w_gu, w_down):
    gu = _dot(hn, w_gu)
    acts = []
    for base in range(0, w_gu.shape[1], 2 * FFN_GRANULE):
        gate = gu[:, base:base + FFN_GRANULE]
        up = gu[:, base + FFN_GRANULE:base + 2 * FFN_GRANULE]
        acts.append((jax.nn.silu(gate) * up).astype(BF16))
    act = acts[0] if len(acts) == 1 else jnp.concatenate(acts, axis=-1)
    return _dot(act, w_down)


PACK_IN_SLOTS = 3
PACK_OUT_SLOTS = 2


def _ffn_pack_kernel(hn_ref, x1_ref, wg_hbm, wu_hbm, wd_hbm, gf_ref, y_ref, wgu_hbm, wd16_hbm,
                     g32_buf, u32_buf, d32_buf, gu16_buf, d16_buf, in_sems, out_sems, *, n_f, tf):
    def in_copies(f, slot):
        col, row = f * tf, f * tf
        if not isinstance(f, int):
            col, row = pl.multiple_of(col, tf), pl.multiple_of(row, tf)
        return (
            pltpu.make_async_copy(wg_hbm.at[:, pl.ds(col, tf)], g32_buf.at[slot], in_sems.at[0, slot]),
            pltpu.make_async_copy(wu_hbm.at[:, pl.ds(col, tf)], u32_buf.at[slot], in_sems.at[1, slot]),
            pltpu.make_async_copy(wd_hbm.at[pl.ds(row, tf), :], d32_buf.at[slot], in_sems.at[2, slot]),
        )

    def out_copies(f, slot):
        col, row = f * (2 * tf), f * tf
        if not isinstance(f, int):
            col, row = pl.multiple_of(col, 2 * tf), pl.multiple_of(row, tf)
        return (
            pltpu.make_async_copy(gu16_buf.at[slot], wgu_hbm.at[:, pl.ds(col, 2 * tf)],
                                  out_sems.at[0, slot]),
            pltpu.make_async_copy(d16_buf.at[slot], wd16_hbm.at[pl.ds(row, tf), :],
                                  out_sems.at[1, slot]),
        )

    def pack(f):
        islot = f % PACK_IN_SLOTS
        oslot = f % PACK_OUT_SLOTS
        gu16_buf[oslot, :, :tf] = g32_buf[islot].astype(BF16)
        gu16_buf[oslot, :, tf:] = u32_buf[islot].astype(BF16)
        d16_buf[oslot] = d32_buf[islot].astype(BF16)
        for copy in out_copies(f, oslot):
            copy.start()

    def iteration(f, first=False, final=False, pack_next=True, reuse_out=True, prefetch=True):
        if pack_next:
            for copy in in_copies(f + 1, (f + 1) % PACK_IN_SLOTS):
                copy.wait()
            if reuse_out:
                for copy in out_copies(f - 1, (f - 1) % PACK_OUT_SLOTS):
                    copy.wait()
            pack(f + 1)
        if prefetch:
            for copy in in_copies(f + 3, (f + 3) % PACK_IN_SLOTS):
                copy.start()
        oslot = f % PACK_OUT_SLOTS
        part = _swiglu_down(hn_ref[...], gu16_buf[oslot], d16_buf[oslot])
        if first:
            y_ref[...] = x1_ref[...] + part
        else:
            y_ref[...] += part
        if final:
            x2 = y_ref[...]
            y_ref[...] = x2 * _rms_scale(x2) * gf_ref[...]

    for f in range(PACK_IN_SLOTS):
        for copy in in_copies(f, f):
            copy.start()
    for copy in in_copies(0, 0):
        copy.wait()
    pack(0)
    iteration(0, first=True, reuse_out=False)

    def middle(f, carry):
        iteration(f)
        return carry

    lax.fori_loop(1, n_f - 3, middle, 0)
    iteration(n_f - 3, prefetch=False)
    iteration(n_f - 2, prefetch=False)
    iteration(n_f - 1, final=True, pack_next=False, prefetch=False)
    for f in (n_f - 2, n_f - 1):
        for copy in out_copies(f, f % PACK_OUT_SLOTS):
            copy.wait()


def _ffn_pack(hn, x1, w):
    n = hn.shape[0]
    tf = FFN_GRANULE
    n_f = D_FF // tf
    assert n_f >= 5
    rows_spec = lambda: pl.BlockSpec((n, D_MODEL), lambda i: (0, 0), pipeline_mode=pl.Buffered(1))
    hbm_spec = pl.BlockSpec(memory_space=pl.ANY)
    return pl.pallas_call(
        functools.partial(_ffn_pack_kernel, n_f=n_f, tf=tf),
        grid=(1,),
        in_specs=[rows_spec(), rows_spec(), hbm_spec, hbm_spec, hbm_spec,
                  pl.BlockSpec((1, D_MODEL), lambda i: (0, 0))],
        out_specs=[pl.BlockSpec((n, D_MODEL), lambda i: (0, 0)), hbm_spec, hbm_spec],
        out_shape=[jax.ShapeDtypeStruct((n, D_MODEL), F32),
                   jax.ShapeDtypeStruct((D_MODEL, 2 * D_FF), BF16),
                   jax.ShapeDtypeStruct((D_FF, D_MODEL), BF16)],
        scratch_shapes=[
            pltpu.VMEM((PACK_IN_SLOTS, D_MODEL, tf), F32),
            pltpu.VMEM((PACK_IN_SLOTS, D_MODEL, tf), F32),
            pltpu.VMEM((PACK_IN_SLOTS, tf, D_MODEL), F32),
            pltpu.VMEM((PACK_OUT_SLOTS, D_MODEL, 2 * tf), BF16),
            pltpu.VMEM((PACK_OUT_SLOTS, tf, D_MODEL), BF16),
            pltpu.SemaphoreType.DMA((3, PACK_IN_SLOTS)),
            pltpu.SemaphoreType.DMA((2, PACK_OUT_SLOTS)),
        ],
        compiler_params=_params(1),
        name="ffn_pack",
    )(hn, x1, w["w_gate"], w["w_up"], w["w_down"], w["g_final"])


def _tail_stream_kernel(x_hbm, mm_hbm, mr_hbm, wom_ref, wor_ref, g2_ref, wgu_hbm, wd_hbm, gf_ref,
                        y_ref, x1_buf, mm_buf, mr_buf, hn_buf, wgu_buf, wd_buf, row_sems, w_sems,
                        *, n_f, tf):
    i = pl.program_id(0)
    n_i = pl.num_programs(0)
    tm = x1_buf.shape[0]

    def row_copies(tile):
        rows = pl.ds(tile * tm, tm)
        return (
            pltpu.make_async_copy(x_hbm.at[rows, :], x1_buf, row_sems.at[0]),
            pltpu.make_async_copy(mm_hbm.at[rows, :], mm_buf, row_sems.at[1]),
            pltpu.make_async_copy(mr_hbm.at[rows, :], mr_buf, row_sems.at[2]),
        )

    def w_copies(f, slot):
        col, row = f * (2 * tf), f * tf
        if not isinstance(f, int):
            col, row = pl.multiple_of(col, 2 * tf), pl.multiple_of(row, tf)
        return (
            pltpu.make_async_copy(wgu_hbm.at[:, pl.ds(col, 2 * tf)], wgu_buf.at[slot],
                                  w_sems.at[0, slot]),
            pltpu.make_async_copy(wd_hbm.at[pl.ds(row, tf), :], wd_buf.at[slot],
                                  w_sems.at[1, slot]),
        )

    @pl.when(i == 0)
    def _():
        for copy in row_copies(0) + w_copies(0, 0):
            copy.start()

    def iteration(f, first, final):
        slot = (i * n_f + f) % 2
        for copy in w_copies(f, slot):
            copy.wait()
        if final:
            @pl.when(i + 1 < n_i)
            def _():
                for copy in w_copies(0, 1 - slot):
                    copy.start()
        else:
            for copy in w_copies(f + 1, 1 - slot):
                copy.start()
        if first:
            for copy in row_copies(i):
                copy.wait()
            for r in range(0, tm, TAIL_ROWS):
                rows = pl.ds(r, TAIL_ROWS)
                x1 = (x1_buf[rows, :] + _dot(mm_buf[rows, :], wom_ref[...])
                      + _dot(mr_buf[rows, :], wor_ref[...]))
                x1_buf[rows, :] = x1
                hn_buf[rows, :] = (x1 * _rms_scale(x1) * g2_ref[...]).astype(BF16)
        part = _swiglu_down(hn_buf[...], wgu_buf[slot], wd_buf[slot])
        if first:
            y_ref[...] = x1_buf[...] + part
        else:
            y_ref[...] += part
        if final:
            x2 = y_ref[...]
            y_ref[...] = x2 * _rms_scale(x2) * gf_ref[...]

    def middle(f, carry):
        @pl.when((f == 1) & (i + 1 < n_i))
        def _():
            for copy in row_copies(i + 1):
                copy.start()

        iteration(f, False, False)
        return carry

    iteration(0, True, False)
    lax.fori_loop(1, n_f - 1, middle, 0)
    iteration(n_f - 1, False, True)


def _tail_stream(x, mix_m, mix_r, w, *, tm, tf):
    n = x.shape[0]
    n_f = D_FF // tf
    assert n_f >= 3
    w_out_spec = lambda half: pl.BlockSpec((D_GROUP, D_MODEL), lambda i, half=half: (half, 0),
                                           pipeline_mode=pl.Buffered(1))
    return pl.pallas_call(
        functools.partial(_tail_stream_kernel, n_f=n_f, tf=tf),
        grid=(n // tm,),
        in_specs=[
            pl.BlockSpec(memory_space=pl.ANY),
            pl.BlockSpec(memory_space=pl.ANY),
            pl.BlockSpec(memory_space=pl.ANY),
            w_out_spec(0), w_out_spec(1),
            pl.BlockSpec((1, D_MODEL), lambda i: (0, 0)),
            pl.BlockSpec(memory_space=pl.ANY),
            pl.BlockSpec(memory_space=pl.ANY),
            pl.BlockSpec((1, D_MODEL), lambda i: (0, 0)),
        ],
        out_specs=pl.BlockSpec((tm, D_MODEL), lambda i: (i, 0)),
        out_shape=jax.ShapeDtypeStruct((n, D_MODEL), F32),
        scratch_shapes=[
            pltpu.VMEM((tm, D_MODEL), F32),
            pltpu.VMEM((tm, D_GROUP), BF16),
            pltpu.VMEM((tm, D_GROUP), BF16),
            pltpu.VMEM((tm, D_MODEL), BF16),
            pltpu.VMEM((2, D_MODEL, 2 * tf), BF16),
            pltpu.VMEM((2, tf, D_MODEL), BF16),
            pltpu.SemaphoreType.DMA((3,)),
            pltpu.SemaphoreType.DMA((2, 2)),
        ],
        compiler_params=_params(1, TAIL_VMEM_LIMIT),
        name="tail_stream",
    )(x, mix_m, mix_r, w["w_out"], w["w_out"], w["g2"], w["w_gu"], w["w_down"], w["g_final"])


def _rope_tables(pos):
    half = HEAD_DIM // 2
    freqs = ROPE_BASE ** (-jnp.arange(half, dtype=F32) / half)
    ang = pos[:, None] * freqs[None, :]
    return jnp.cos(ang), jnp.sin(ang)


def _prompt_trunk(x, w):
    batch, seq, _ = x.shape
    x2d = x.reshape(batch * seq, D_MODEL)
    rope = _rope_tables(jnp.arange(seq, dtype=F32))
    mix_m, c_new, n_new, m_new, mix_r, r_new = _prompt_front(x2d, w, rope, batch=batch, seq=seq)
    y = _tail_stream(x2d, mix_m, mix_r, w, tm=FFN_TM, tf=FFN_TF)
    return (y.reshape(batch, seq, D_MODEL), c_new[None], n_new[None],
            m_new[:, :N_HEADS, 0][None], r_new[None])


def _sample_trunk(x, init, w):
    batch, seq, _ = x.shape
    x2d = x.reshape(batch * seq, D_MODEL)
    proj, gt = _inproj(x2d, w["g1"], w["w_in"], w["w_gatecols"], w["b_gate"],
                       tm=INPROJ_TM, tn=INPROJ_TN)
    gt3 = gt.reshape(N_GATES, batch, seq).transpose(1, 0, 2)
    gt3 = jnp.pad(gt3, ((0, 0), (0, 0), (0, max(0, LANES - seq))))
    c0, n0, m0, r0 = init
    m0 = jnp.pad(m0, ((0, 0), (0, SUBLANES - N_HEADS)))
    m0 = jnp.broadcast_to(m0[:, :, None], (batch, SUBLANES, LANES))
    rope = _rope_tables(PAST_LEN + jnp.arange(seq, dtype=F32))
    mix_m, c_new, n_new, m_new, mix_r, r_new = _sample_mixers(
        proj, gt3, rope, (c0, n0, m0, r0), w, batch=batch, seq=seq, group=SAMPLE_GROUP)
    x1, hn = _outproj(x2d, mix_m, mix_r, w["w_out"], w["g2"], tm=OUTPROJ_TM)
    y, w_gu16, w_down16 = _ffn_pack(hn, x1, w)
    w16 = dict(w_gu=w_gu16, w_down=w_down16)
    return (y.reshape(batch, seq, D_MODEL), c_new[None], n_new[None],
            m_new[:, :N_HEADS, 0][None], r_new[None]), w16


def kernel(x_prompt, x_sample, state_mlstm_C, state_mlstm_n, state_mlstm_m, state_ret,
           g_norm1, w_in, b_gates, g_mlstm_norm, g_ret_norm, w_out, g_norm2,
           w_gate, w_up, w_down, g_final):
    w_in0 = w_in[0].astype(BF16)
    w = dict(
        g1=g_norm1[0].reshape(1, D_MODEL),
        w_in=w_in0,
        w_gatecols=jnp.pad(w_in0[:, D_PROJ:], ((0, 0), (0, LANES - N_GATES))),
        b_gate=jnp.pad(b_gates[0], (0, LANES - N_GATES)).reshape(1, LANES),
        gain_m=g_mlstm_norm[0].reshape(1, D_GROUP),
        gain_r=g_ret_norm[0].reshape(1, D_GROUP),
        w_out=w_out[0].astype(BF16),
        g2=g_norm2[0].reshape(1, D_MODEL),
        w_gate=w_gate[0],
        w_up=w_up[0],
        w_down=w_down[0],
        g_final=g_final.reshape(1, D_MODEL),
        log_gamma=jnp.log(1.0 - jnp.exp2(-5.0 - jnp.arange(N_HEADS, dtype=F32))),
    )
    init_s = (state_mlstm_C[0], state_mlstm_n[0], state_mlstm_m[0], state_ret[0])
    (y_s, c_s, n_s, m_s, r_s), w16 = _sample_trunk(x_sample, init_s, w)
    y_p, c_p, n_p, m_p, r_p = _prompt_trunk(x_prompt, {**w, **w16})
    return (y_p, y_s, c_p, n_p, m_p, r_p, c_s, n_s, m_s, r_s)
```
